```python
import jax, jax.numpy as jnp
from jax import lax
import numpy as np

D_MODEL = 4096
BATCH = 2
SEQ = 8192
DEPTH = 2

GRID_W = 64
CTX_LEN = 256
NORM_EPS = 1e-6

D_GLA = D_MODEL // 4
D_MLA = D_MODEL // 2
D_LRU = D_MODEL // 4
D_MIX = D_GLA + D_MLA + D_LRU

GLA_HEADS = 4
GLA_DK = D_GLA // (2 * GLA_HEADS)
GLA_DV = D_GLA // GLA_HEADS
GLA_GATE_RANK = 16
GLA_TAU = 16.0
GLA_CHUNK = 64
GLA_QSCALE = GLA_DK ** -0.5

MLA_HEADS = 16
MLA_NOPE = 128
MLA_ROPE = 64
MLA_V = D_MLA // MLA_HEADS
MLA_Q_RANK = 768
MLA_KV_RANK = 512
MLA_SCALE = (MLA_NOPE + MLA_ROPE) ** -0.5
ROPE_BASE = 10000.0
ATTN_BLOCK = 128

LRU_BLOCKS = 8
LRU_BW = D_LRU // LRU_BLOCKS
LRU_CONV = 4
LRU_PAD_LEFT = 2
LRU_C = 8.0

N_GROUPS = 8
EXPERTS_PER_GROUP = 8
N_EXPERTS = N_GROUPS * EXPERTS_PER_GROUP
TOP_K_IN_GROUP = 2
D_EXPERT = 256
MOE_BLOCK = 128

IN_SIZES = (GLA_HEADS * GLA_DK, GLA_HEADS * GLA_DK, D_GLA, D_GLA, 2 * GLA_GATE_RANK,
            MLA_Q_RANK, MLA_KV_RANK, MLA_ROPE,
            D_LRU, D_LRU)
D_IN = sum(IN_SIZES)

kernel_name = "hybrid_gla_mla_rglru_hmoe_dit"

F32 = jnp.float32


def rms_norm(x, g):
    xf = x.astype(F32)
    y = xf * lax.rsqrt(jnp.mean(xf * xf, axis=-1, keepdims=True) + NORM_EPS)
    return (y * g.astype(F32)).astype(x.dtype)


def modulate(h, shift, scale):
    return h * (1 + scale) + shift


def adaln(cond, w_mod, b_mod):
    m = jax.nn.silu(cond) @ w_mod + b_mod
    return [t[:, None, :] for t in jnp.split(m, 6, axis=-1)]


def split_in(u):
    offs, s = [], 0
    for n in IN_SIZES[:-1]:
        s += n
        offs.append(s)
    return jnp.split(u, offs, axis=-1)


def flip(t):
    return jnp.flip(t, axis=1)


def axial_rope_tables(rows):
    n = MLA_ROPE // 4
    inv = ROPE_BASE ** (-jnp.arange(n, dtype=F32) / n)
    r = jnp.repeat(jnp.arange(rows, dtype=F32), GRID_W)
    col = jnp.tile(jnp.arange(GRID_W, dtype=F32), rows)
    ar, ac = r[:, None] * inv, col[:, None] * inv
    return (jnp.cos(ar), jnp.sin(ar), jnp.cos(ac), jnp.sin(ac))


def _rot_half(x, cos, sin):
    x1, x2 = jnp.split(x, 2, axis=-1)
    return jnp.concatenate([x1 * cos - x2 * sin, x2 * cos + x1 * sin], axis=-1)


def axial_rope(x, tabs):
    extra = x.ndim - 3
    cr, sr, cc, sc = [t.reshape(t.shape[0], *([1] * extra), t.shape[1]) for t in tabs]
    xf = x.astype(F32)
    half = MLA_ROPE // 2
    out = jnp.concatenate([_rot_half(xf[..., :half], cr, sr), _rot_half(xf[..., half:], cc, sc)], axis=-1)
    return out.astype(x.dtype)


def gla_q(q):
    B, L, _ = q.shape
    return q.reshape(B, L, GLA_HEADS, GLA_DK).astype(F32) * GLA_QSCALE


def gla_kvd(k, v, a_low, w_a2, b_a):
    B, L, _ = k.shape
    K = k.reshape(B, L, GLA_HEADS, GLA_DK).astype(F32)
    V = v.reshape(B, L, GLA_HEADS, GLA_DV).astype(F32)
    z = jnp.einsum('bljr,jrn->bljn', a_low.reshape(B, L, 2, GLA_GATE_RANK).astype(F32),
                   w_a2.astype(F32)) + b_a.astype(F32)
    ld = (jax.nn.log_sigmoid(z) / GLA_TAU).reshape(B, L, 2, GLA_HEADS, GLA_DK)
    return K, V, ld[:, :, 0], ld[:, :, 1]


def gla_chunked(q, k, v, logd, s0):
    B, L, H, DK = q.shape
    DV = v.shape[-1]
    C = GLA_CHUNK
    N = L // C
    q, k, v, logd = [t.reshape(B, N, C, H, t.shape[-1]) for t in (q, k, v, logd)]
    b = jnp.cumsum(logd, axis=2)
    b_ref = b[:, :, C // 2:C // 2 + 1]
    att = jnp.einsum('bnthd,bnshd->bnhts', q * jnp.exp(b - b_ref), k * jnp.exp(b_ref - b))
    att = jnp.where(jnp.tril(jnp.ones((C, C), bool)), att, 0.0)
    o_intra = jnp.einsum('bnhts,bnshv->bnthv', att, v)
    b_last = b[:, :, -1:]
    q_in = q * jnp.exp(b)
    k_st = k * jnp.exp(b_last - b)
    d_last = jnp.exp(b_last[:, :, 0])

    def step(s, xs):
        qn, kn, vn, dn = xs
        o = jnp.einsum('bthd,bhdv->bthv', qn, s)
        s = dn[..., None] * s + jnp.einsum('bthd,bthv->bhdv', kn, vn)
        return s, o

    xs = tuple(jnp.moveaxis(t, 1, 0) for t in (q_in, k_st, v, d_last))
    _, o_inter = lax.scan(step, s0, xs)
    return (o_intra + jnp.moveaxis(o_inter, 0, 1)).reshape(B, L, H, DV)


def gla_final_state(k, v, logd):
    G = jnp.cumsum(logd, axis=1)
    return jnp.einsum('blhd,blhv->bhdv', k * jnp.exp(G[:, -1:] - G), v)


def gla_bidir(q, k, v, ldf, ldb, s_f, s_b):
    fwd = gla_chunked(q, k, v, ldf, s_f)
    bwd = flip(gla_chunked(flip(q), flip(k), flip(v), flip(ldb), s_b))
    return fwd + bwd


def gla_output(o, g, norm_g):
    B, L = o.shape[:2]
    y = rms_norm(o, norm_g).reshape(B, L, D_GLA) * jax.nn.silu(g.astype(F32))
    return y.astype(g.dtype)


def gla_mixer(parts, parts_c, w_a2, b_a, norm_g, ctx_out):
    q, k, v, g, a_low = parts
    qc, kc, vc, gc, a_lowc = parts_c
    Kc, Vc, LDcf, LDcb = gla_kvd(kc, vc, a_lowc, w_a2, b_a)
    s_f = gla_final_state(Kc, Vc, LDcf)
    s_b = gla_final_state(flip(Kc), flip(Vc), flip(LDcb))
    K, V, LDf, LDb = gla_kvd(k, v, a_low, w_a2, b_a)
    y = gla_output(gla_bidir(gla_q(q), K, V, LDf, LDb, s_f, s_b), g, norm_g)
    yc = None
    if ctx_out:
        z = jnp.zeros_like(s_f)
        yc = gla_output(gla_bidir(gla_q(qc), Kc, Vc, LDcf, LDcb, z, z), gc, norm_g)
    return y, yc


def mla_keys(kv_dn, k_rope, kv_norm_g, w_ukv, k_head_g):
    B, L, _ = kv_dn.shape
    kv = (rms_norm(kv_dn, kv_norm_g) @ w_ukv).reshape(B, L, MLA_HEADS, MLA_NOPE + MLA_V)
    k_nope = rms_norm(kv[..., :MLA_NOPE], k_head_g[:MLA_NOPE])
    k_r = rms_norm(k_rope, k_head_g[MLA_NOPE:])
    return k_nope, k_r, kv[..., MLA_NOPE:]


def mla_queries(q_dn, q_norm_g, w_uq, q_head_g):
    B, L, _ = q_dn.shape
    q = (rms_norm(q_dn, q_norm_g) @ w_uq).reshape(B, L, MLA_HEADS, MLA_NOPE + MLA_ROPE)
    return rms_norm(q[..., :MLA_NOPE], q_head_g[:MLA_NOPE]), rms_norm(q[..., MLA_NOPE:], q_head_g[MLA_NOPE:])


def mla_attend(qn, qr, kn, kr, v):
    s = (jnp.einsum('bqhd,bkhd->bhqk', qn, kn, preferred_element_type=F32)
         + jnp.einsum('bqhr,bkr->bhqk', qr, kr, preferred_element_type=F32)) * MLA_SCALE
    p = jax.nn.softmax(s, axis=-1)
    return jnp.einsum('bhqk,bkhv->bqhv', p.astype(v.dtype), v)


def mla_mixer(parts, parts_c, rope, q_norm_g, kv_norm_g, w_uq, w_ukv, q_head_g, k_head_g, ctx_out):
    q_dn, kv_dn, k_rope = parts
    q_dnc, kv_dnc, k_ropec = parts_c
    knc, krc, vc = mla_keys(kv_dnc, k_ropec, kv_norm_g, w_ukv, k_head_g)
    kn, kr, v = mla_keys(kv_dn, k_rope, kv_norm_g, w_ukv, k_head_g)
    kr = axial_rope(kr, rope)
    qn, qr = mla_queries(q_dn, q_norm_g, w_uq, q_head_g)
    qr = axial_rope(qr, rope)
    k_n = jnp.concatenate([knc, kn], axis=1)
    k_r = jnp.concatenate([krc, kr], axis=1)
    v_all = jnp.concatenate([vc, v], axis=1)
    B, L = qn.shape[:2]
    nb = L // ATTN_BLOCK

    def blk(t):
        return jnp.swapaxes(t.reshape(B, nb, ATTN_BLOCK, *t.shape[2:]), 0, 1)

    o = lax.map(lambda qs: mla_attend(qs[0], qs[1], k_n, k_r, v_all), (blk(qn), blk(qr)))
    y = jnp.swapaxes(o, 0, 1).reshape(B, L, MLA_HEADS * MLA_V)
    yc = None
    if ctx_out:
        qnc, qrc = mla_queries(q_dnc, q_norm_g, w_uq, q_head_g)
        yc = mla_attend(qnc, qrc, knc, krc, vc).reshape(B, knc.shape[1], MLA_HEADS * MLA_V)
    return y, yc


def short_conv(x, w, b):
    L = x.shape[1]
    xp = jnp.pad(x, ((0, 0), (LRU_PAD_LEFT, LRU_CONV - 1 - LRU_PAD_LEFT), (0, 0)))
    y = b
    for j in range(LRU_CONV):
        y = y + xp[:, j:j + L] * w[j]
    return y


def lru_gates(xc, w_gate, b_gate, lam):
    B, L, _ = xc.shape
    xf = xc.astype(F32)
    z = jnp.einsum('blnc,dgncm->bldgnm', xf.reshape(B, L, LRU_BLOCKS, LRU_BW), w_gate.astype(F32))
    z = z.reshape(B, L, 2, 2, D_LRU) + b_gate.astype(F32)
    r = jax.nn.sigmoid(z[:, :, :, 0])
    i = jax.nn.sigmoid(z[:, :, :, 1])
    log_a = -LRU_C * r * jax.nn.softplus(-lam.astype(F32))
    u = jnp.sqrt(-jnp.expm1(2.0 * log_a)) * i * xf[:, :, None, :]
    return log_a, u


def linear_scan(log_a, u, h0):
    a = jnp.exp(log_a)
    u = u.at[:, 0].add(a[:, 0] * h0)

    def comb(e1, e2):
        a1, u1 = e1
        a2, u2 = e2
        return a1 * a2, a2 * u1 + u2

    _, h = lax.associative_scan(comb, (a, u), axis=1)
    return h


def linear_final_state(log_a, u):
    G = jnp.cumsum(log_a, axis=1)
    return jnp.sum(jnp.exp(G[:, -1:] - G) * u, axis=1)


def lru_output(la, u, gate, h0f, h0b):
    hf = linear_scan(la[:, :, 0], u[:, :, 0], h0f)
    hb = flip(linear_scan(flip(la[:, :, 1]), flip(u[:, :, 1]), h0b))
    return (jax.nn.gelu(gate.astype(F32)) * (hf + hb)).astype(gate.dtype)


def lru_mixer(parts, parts_c, conv_w, conv_b, w_gate, b_gate, lam, ctx_out):
    xb, gb = parts
    xbc, gbc = parts_c
    lac, uc = lru_gates(short_conv(xbc, conv_w, conv_b), w_gate, b_gate, lam)
    h0f = linear_final_state(lac[:, :, 0], uc[:, :, 0])
    h0b = linear_final_state(flip(lac[:, :, 1]), flip(uc[:, :, 1]))
    la, u = lru_gates(short_conv(xb, conv_w, conv_b), w_gate, b_gate, lam)
    y = lru_output(la, u, gb, h0f, h0b)
    yc = None
    if ctx_out:
        z = jnp.zeros_like(h0f)
        yc = lru_output(lac, uc, gbc, z, z)
    return y, yc


def moe_ffn(h, w_grp, b_grp, w_exp, b_exp, w13, w2):
    M, D = h.shape
    rows_m = jnp.arange(M)
    g_logit = jnp.einsum('md,dg->mg', h, w_grp, preferred_element_type=F32) + b_grp.astype(F32)
    g_idx = jnp.argmax(g_logit, axis=-1)
    p_grp = jax.nn.softmax(g_logit, axis=-1)[rows_m, g_idx][:, None]
    e_logit = (jnp.einsum('md,de->me', h, w_exp, preferred_element_type=F32) + b_exp.astype(F32))
    e_logit = e_logit.reshape(M, N_GROUPS, EXPERTS_PER_GROUP)[rows_m, g_idx]
    top_v, top_i = lax.top_k(e_logit, TOP_K_IN_GROUP)
    gate = p_grp * jax.nn.softmax(top_v, axis=-1)
    expert = g_idx[:, None] * EXPERTS_PER_GROUP + top_i

    A = M * TOP_K_IN_GROUP
    e_flat = expert.reshape(-1)
    tok_flat = jnp.repeat(rows_m, TOP_K_IN_GROUP)
    order = jnp.argsort(e_flat)
    e_s, tok_s, gate_s = e_flat[order], tok_flat[order], gate.reshape(-1)[order]
    counts = jnp.bincount(e_flat, length=N_EXPERTS)
    start = jnp.cumsum(counts) - counts
    padded = ((counts + MOE_BLOCK - 1) // MOE_BLOCK) * MOE_BLOCK
    pad_end = jnp.cumsum(padded)
    dest = (pad_end - padded)[e_s] + (jnp.arange(A) - start[e_s])
    P = (-(-A // MOE_BLOCK) + N_EXPERTS) * MOE_BLOCK
    NB = P // MOE_BLOCK
    row_tok = jnp.full((P,), M, jnp.int32).at[dest].set(tok_s.astype(jnp.int32))
    row_gate = jnp.zeros((P,), F32).at[dest].set(gate_s)
    blk_exp = jnp.minimum(jnp.searchsorted(pad_end, jnp.arange(NB) * MOE_BLOCK, side='right'), N_EXPERTS - 1)
    hp = jnp.concatenate([h, jnp.zeros((1, D), h.dtype)], axis=0)

    def step(y, blk):
        e, rows, g = blk
        a, b = jnp.split(hp[rows] @ w13[e], 2, axis=-1)
        out = (jax.nn.silu(a) * b) @ w2[e]
        return y.at[rows].add((out * g[:, None]).astype(y.dtype)), None

    y, _ = lax.scan(step, jnp.zeros((M + 1, D), h.dtype),
                    (blk_exp, row_tok.reshape(NB, MOE_BLOCK), row_gate.reshape(NB, MOE_BLOCK)))
    return y[:M]


def layer(x, xc, c, c_ctx, rope, prm, last):
    (w_mod, b_mod, norm1_g, norm2_g, w_in, gla_w_a2, gla_b_a, gla_norm_g,
     mla_q_norm_g, mla_kv_norm_g, mla_w_uq, mla_w_ukv, mla_q_head_g, mla_k_head_g,
     lru_conv_w, lru_conv_b, lru_w_gate, lru_b_gate, lru_lambda, w_out,
     moe_w_grp, moe_b_grp, moe_w_exp, moe_b_exp, moe_w13, moe_w2) = prm
    ctx_out = not last
    sh_a, sc_a, g_a, sh_f, sc_f, g_f = adaln(c, w_mod, b_mod)
    csh_a, csc_a, cg_a, csh_f, csc_f, cg_f = adaln(c_ctx[None, :], w_mod, b_mod)

    h = modulate(rms_norm(x, norm1_g), sh_a, sc_a)
    hc = modulate(rms_norm(xc, norm1_g), csh_a, csc_a)
    P = split_in(h @ w_in)
    Pc = split_in(hc @ w_in)
    ya, yac = gla_mixer(P[0:5], Pc[0:5], gla_w_a2, gla_b_a, gla_norm_g, ctx_out)
    yb, ybc = mla_mixer(P[5:8], Pc[5:8], rope, mla_q_norm_g, mla_kv_norm_g, mla_w_uq, mla_w_ukv,
                        mla_q_head_g, mla_k_head_g, ctx_out)
    yl, ylc = lru_mixer(P[8:10], Pc[8:10], lru_conv_w, lru_conv_b, lru_w_gate, lru_b_gate, lru_lambda, ctx_out)
    x = x + (g_a * (jnp.concatenate([ya, yb, yl], axis=-1) @ w_out)).astype(x.dtype)
    if ctx_out:
        xc = xc + (cg_a * (jnp.concatenate([yac, ybc, ylc], axis=-1) @ w_out)).astype(xc.dtype)

    B, L, D = x.shape
    moe_w = (moe_w_grp, moe_b_grp, moe_w_exp, moe_b_exp, moe_w13, moe_w2)
    h2 = modulate(rms_norm(x, norm2_g), sh_f, sc_f)
    if ctx_out:
        Lc = xc.shape[1]
        h2c = modulate(rms_norm(xc, norm2_g), csh_f, csc_f)
        out = moe_ffn(jnp.concatenate([h2c.reshape(-1, D), h2.reshape(-1, D)], axis=0), *moe_w)
        xc = xc + (cg_f * out[:B * Lc].reshape(B, Lc, D)).astype(xc.dtype)
        x = x + (g_f * out[B * Lc:].reshape(B, L, D)).astype(x.dtype)
    else:
        x = x + (g_f * moe_ffn(h2.reshape(-1, D), *moe_w).reshape(B, L, D)).astype(x.dtype)
    return x, xc


def setup_inputs(seed: int = 0) -> dict:
    key = jax.random.key(seed)
    ks = jax.random.split(key, 32)

    def nrm(k, shape, s):
        return jax.random.normal(k, shape, F32) * s

    def gain(k, shape):
        return 1.0 + 0.05 * jax.random.normal(k, shape, F32)

    a_base = jax.random.uniform(ks[19], (DEPTH, 2, D_LRU), F32, 0.9, 0.999) ** (1.0 / LRU_C)
    lam = jnp.log(a_base) - jnp.log1p(-a_base)
    return {
        "x": nrm(ks[0], (BATCH, SEQ, D_MODEL), 1.0),
        "c": nrm(ks[1], (BATCH, D_MODEL), 1.0),
        "ctx": nrm(ks[2], (BATCH, CTX_LEN, D_MODEL), 1.0),
        "c_ctx": nrm(ks[3], (D_MODEL,), 1.0),
        "w_mod": nrm(ks[4], (DEPTH, D_MODEL, 6 * D_MODEL), D_MODEL ** -0.5),
        "b_mod": nrm(ks[5], (DEPTH, 6 * D_MODEL), 0.02),
        "norm1_g": gain(ks[6], (DEPTH, D_MODEL)),
        "norm2_g": gain(ks[7], (DEPTH, D_MODEL)),
        "w_in": nrm(ks[8], (DEPTH, D_MODEL, D_IN), D_MODEL ** -0.5),
        "gla_w_a2": nrm(ks[9], (DEPTH, 2, GLA_GATE_RANK, GLA_HEADS * GLA_DK), GLA_GATE_RANK ** -0.5),
        "gla_b_a": nrm(ks[10], (DEPTH, 2, GLA_HEADS * GLA_DK), 0.1),
        "gla_norm_g": gain(ks[11], (DEPTH, GLA_DV)),
        "mla_q_norm_g": gain(ks[12], (DEPTH, MLA_Q_RANK)),
        "mla_kv_norm_g": gain(ks[13], (DEPTH, MLA_KV_RANK)),
        "mla_w_uq": nrm(ks[14], (DEPTH, MLA_Q_RANK, MLA_HEADS * (MLA_NOPE + MLA_ROPE)), MLA_Q_RANK ** -0.5),
        "mla_w_ukv": nrm(ks[15], (DEPTH, MLA_KV_RANK, MLA_HEADS * (MLA_NOPE + MLA_V)), MLA_KV_RANK ** -0.5),
        "mla_q_head_g": gain(ks[16], (DEPTH, MLA_NOPE + MLA_ROPE)),
        "mla_k_head_g": gain(ks[17], (DEPTH, MLA_NOPE + MLA_ROPE)),
        "lru_conv_w": nrm(ks[18], (DEPTH, LRU_CONV, D_LRU), LRU_CONV ** -0.5),
        "lru_conv_b": nrm(ks[20], (DEPTH, D_LRU), 0.02),
        "lru_w_gate": nrm(ks[21], (DEPTH, 2, 2, LRU_BLOCKS, LRU_BW, LRU_BW), LRU_BW ** -0.5),
        "lru_b_gate": nrm(ks[22], (DEPTH, 2, 2, D_LRU), 0.1),
        "lru_lambda": lam,
        "w_out": nrm(ks[23], (DEPTH, D_MIX, D_MODEL), D_MIX ** -0.5),
        "moe_w_grp": nrm(ks[24], (DEPTH, D_MODEL, N_GROUPS), D_MODEL ** -0.5),
        "moe_b_grp": nrm(ks[25], (DEPTH, N_GROUPS), 0.01),
        "moe_w_exp": nrm(ks[26], (DEPTH, D_MODEL, N_EXPERTS), D_MODEL ** -0.5),
        "moe_b_exp": nrm(ks[27], (DEPTH, N_EXPERTS), 0.01),
        "moe_w13": nrm(ks[28], (DEPTH, N_EXPERTS, D_MODEL, 2 * D_EXPERT), D_MODEL ** -0.5),
        "moe_w2": nrm(ks[29], (DEPTH, N_EXPERTS, D_EXPERT, D_MODEL), D_EXPERT ** -0.5),
    }


def reference(x, c, ctx, c_ctx, w_mod, b_mod, norm1_g, norm2_g, w_in, gla_w_a2, gla_b_a, gla_norm_g,
              mla_q_norm_g, mla_kv_norm_g, mla_w_uq, mla_w_ukv, mla_q_head_g, mla_k_head_g,
              lru_conv_w, lru_conv_b, lru_w_gate, lru_b_gate, lru_lambda, w_out,
              moe_w_grp, moe_b_grp, moe_w_exp, moe_b_exp, moe_w13, moe_w2):
    ROWS = x.shape[1] // GRID_W
    rope = axial_rope_tables(ROWS)
    xc = ctx
    for i in range(DEPTH):
        prm = tuple(p[i] for p in (w_mod, b_mod, norm1_g, norm2_g, w_in, gla_w_a2, gla_b_a, gla_norm_g,
                                   mla_q_norm_g, mla_kv_norm_g, mla_w_uq, mla_w_ukv, mla_q_head_g, mla_k_head_g,
                                   lru_conv_w, lru_conv_b, lru_w_gate, lru_b_gate, lru_lambda, w_out,
                                   moe_w_grp, moe_b_grp, moe_w_exp, moe_b_exp, moe_w13, moe_w2))
        x, xc = layer(x, xc, c, c_ctx, rope, prm, last=(i == DEPTH - 1))
    return x
```

```python
import functools

import jax
import jax.numpy as jnp
from jax import lax
from jax.experimental import pallas as pl
from jax.experimental.pallas import tpu as pltpu

F32 = jnp.float32
BF16 = jnp.bfloat16
HI = lax.Precision.HIGHEST

D_MODEL = 4096
GRID_W = 64
NORM_EPS = 1e-6
D_GLA = D_MODEL // 4
D_MLA = D_MODEL // 2
D_LRU = D_MODEL // 4
GLA_HEADS = 4
GLA_DK = 128
GLA_DV = 256
GLA_GATE_RANK = 16
GLA_TAU = 16.0
GLA_CHUNK = 64
GLA_QSCALE = GLA_DK ** -0.5
MLA_HEADS = 16
MLA_NOPE = 128
MLA_ROPE = 64
MLA_V = 128
MLA_Q_RANK = 768
MLA_KV_RANK = 512
MLA_SCALE = (MLA_NOPE + MLA_ROPE) ** -0.5
ROPE_BASE = 10000.0
LRU_BLOCKS = 8
LRU_BW = 128
LRU_CONV = 4
LRU_C = 8.0
N_GROUPS = 8
EXPERTS_PER_GROUP = 8
N_EXPERTS = 64
TOP_K = 2
D_EXPERT = 256

LANES = 128
SUBLANES = 8
VMEM_BYTES = 64 << 20
VMEM_LIMIT = VMEM_BYTES - (8 << 20)
MOE_BLOCK = 256

QDN_W, AL_W, KR_W = MLA_Q_RANK, LANES, LANES
COL_QDN = 0
COL_AL = 768
COL_KR = 896
COL_V = 1024
COL_G = 2048
COL_XB = 3072
COL_GB = 4096
COL_Q = 5120
COL_K = 5632
COL_KVDN = 6144
D_INP = 6656
HEAD_W = 256


def _params(sem, vmem=VMEM_LIMIT):
    return pltpu.CompilerParams(dimension_semantics=sem, vmem_limit_bytes=vmem)


def _tile(n, pref):
    t = min(n, pref)
    while n % t or t % SUBLANES:
        t -= 1
    return t


def _silu(x):
    return x * jax.nn.sigmoid(x)


def _softplus(x):
    return jnp.maximum(x, 0.0) + jnp.log1p(jnp.exp(-jnp.abs(x)))


def _adaln_body(c_ref, w_ref, b_ref, o_ref):
    s = _silu(c_ref[...])
    o_ref[...] = jnp.dot(s, w_ref[...], preferred_element_type=F32, precision=HI) + b_ref[...]


def adaln_all(cond, w_mod, b_mod):
    depth, d, n = w_mod.shape
    tn = 512
    return pl.pallas_call(
        _adaln_body,
        grid=(depth, n // tn),
        in_specs=[pl.BlockSpec((SUBLANES, d), lambda l, j: (0, 0)),
                  pl.BlockSpec((None, d, tn), lambda l, j: (l, 0, j)),
                  pl.BlockSpec((None, 1, tn), lambda l, j: (l, 0, j))],
        out_specs=pl.BlockSpec((None, SUBLANES, tn), lambda l, j: (l, 0, j)),
        out_shape=jax.ShapeDtypeStruct((depth, SUBLANES, n), F32),
        compiler_params=_params(("arbitrary", "arbitrary")),
        name="adaln",
    )(cond, w_mod, b_mod.reshape(depth, 1, n))


def _norm_mod(x, g, shift, scale):
    y = x * lax.rsqrt(jnp.mean(x * x, axis=-1, keepdims=True) + NORM_EPS) * g
    return y * (1.0 + scale) + shift


def _norm_mod_body(x_ref, g_ref, sh_ref, sc_ref, o_ref):
    o_ref[...] = _norm_mod(x_ref[...], g_ref[...], sh_ref[...], sc_ref[...]).astype(o_ref.dtype)


def _norm_router_body(x_ref, g_ref, sh_ref, sc_ref, wr_ref, br_ref, o_ref, lg_ref):
    h = _norm_mod(x_ref[...], g_ref[...], sh_ref[...], sc_ref[...])
    o_ref[...] = h.astype(o_ref.dtype)
    lg_ref[...] = jnp.dot(h, wr_ref[...], preferred_element_type=F32, precision=HI) + br_ref[...]


def norm_mod(x, g, shift, scale, router=None):
    bx, lx, d = x.shape
    tm = _tile(lx, 256)
    row = pl.BlockSpec((None, tm, d), lambda b, i: (b, i, 0))
    vec = pl.BlockSpec((1, d), lambda b, i: (0, 0))
    mod = pl.BlockSpec((None, 1, d), lambda b, i: (b, 0, 0))
    if router is None:
        return pl.pallas_call(
            _norm_mod_body, grid=(bx, lx // tm),
            in_specs=[row, vec, mod, mod], out_specs=row,
            out_shape=jax.ShapeDtypeStruct((bx, lx, d), BF16),
            compiler_params=_params(("arbitrary", "arbitrary")), name="norm_mod",
        )(x, g.reshape(1, d), shift, scale)
    wr, br = router
    nr = wr.shape[1]
    return pl.pallas_call(
        _norm_router_body, grid=(bx, lx // tm),
        in_specs=[row, vec, mod, mod,
                  pl.BlockSpec((d, nr), lambda b, i: (0, 0)),
                  pl.BlockSpec((1, nr), lambda b, i: (0, 0))],
        out_specs=[row, pl.BlockSpec((None, tm, nr), lambda b, i: (b, i, 0))],
        out_shape=[jax.ShapeDtypeStruct((bx, lx, d), BF16), jax.ShapeDtypeStruct((bx, lx, nr), F32)],
        compiler_params=_params(("arbitrary", "arbitrary")), name="norm_router",
    )(x, g.reshape(1, d), shift, scale, wr, br)


def _mm_body(a_ref, w_ref, o_ref):
    o_ref[...] = jnp.dot(a_ref[...], w_ref[...], preferred_element_type=F32).astype(o_ref.dtype)


def matmul(a, w, out_dtype=F32, tm_pref=1024, tn=512):
    m, k = a.shape
    n = w.shape[1]
    tm = _tile(m, tm_pref)
    return pl.pallas_call(
        _mm_body, grid=(m // tm, n // tn),
        in_specs=[pl.BlockSpec((tm, k), lambda i, j: (i, 0)),
                  pl.BlockSpec((k, tn), lambda i, j: (0, j))],
        out_specs=pl.BlockSpec((tm, tn), lambda i, j: (i, j)),
        out_shape=jax.ShapeDtypeStruct((m, n), out_dtype),
        compiler_params=_params(("arbitrary", "arbitrary")), name="matmul",
    )(a, w)


def _wout_body(ya_ref, yb_ref, yl_ref, w_ref, x_ref, g_ref, o_ref):
    acc = jnp.dot(ya_ref[...], w_ref[0:D_GLA, :], preferred_element_type=F32)
    acc += jnp.dot(yb_ref[...], w_ref[D_GLA:D_GLA + D_MLA, :], preferred_element_type=F32)
    acc += jnp.dot(yl_ref[...], w_ref[D_GLA + D_MLA:, :], preferred_element_type=F32)
    o_ref[...] = x_ref[...] + g_ref[...] * acc


def out_proj(ya, yb, yl, w, x, gate):
    bx, lx, d = x.shape
    tm = _tile(lx, 1024)
    tn = 512
    return pl.pallas_call(
        _wout_body, grid=(bx, lx // tm, d // tn),
        in_specs=[pl.BlockSpec((None, tm, D_GLA), lambda b, i, j: (b, i, 0)),
                  pl.BlockSpec((None, tm, D_MLA), lambda b, i, j: (b, i, 0)),
                  pl.BlockSpec((None, tm, D_LRU), lambda b, i, j: (b, i, 0)),
                  pl.BlockSpec((d, tn), lambda b, i, j: (0, j)),
                  pl.BlockSpec((None, tm, tn), lambda b, i, j: (b, i, j)),
                  pl.BlockSpec((None, 1, tn), lambda b, i, j: (b, 0, j))],
        out_specs=pl.BlockSpec((None, tm, tn), lambda b, i, j: (b, i, j)),
        out_shape=jax.ShapeDtypeStruct((bx, lx, d), F32),
        compiler_params=_params(("arbitrary", "arbitrary", "arbitrary")), name="out_proj",
    )(ya, yb, yl, w, x, gate)


def _gla_body(*refs, reverse, final, nchunk):
    if final:
        (q_ref, k_ref, v_ref, al_ref, wa_ref, ba_ref, s0_ref, of_ref, g_ref, ng_ref,
         o_ref, sfin_ref, s_scr) = refs
    else:
        q_ref, k_ref, v_ref, al_ref, wa_ref, ba_ref, s0_ref, o_ref, sfin_ref, s_scr = refs
    C = GLA_CHUNK

    @pl.when(pl.program_id(1) == 0)
    def _():
        s_scr[...] = s0_ref[...]

    r_i = lax.broadcasted_iota(jnp.int32, (C, C), 0)
    c_i = lax.broadcasted_iota(jnp.int32, (C, C), 1)
    keep = (c_i >= r_i) if reverse else (c_i <= r_i)
    tri = keep.astype(F32)
    ref_row = C - 1 - C // 2 if reverse else C // 2
    last_row = 0 if reverse else C - 1

    order = range(nchunk - 1, -1, -1) if reverse else range(nchunk)
    for ci in order:
        rows = slice(ci * C, (ci + 1) * C)
        z = jnp.dot(al_ref[rows, :], wa_ref[...], preferred_element_type=F32, precision=HI) + ba_ref[...]
        logd = (jnp.minimum(z, 0.0) - jnp.log1p(jnp.exp(-jnp.abs(z)))) * (1.0 / GLA_TAU)
        b = jnp.dot(tri, logd, preferred_element_type=F32, precision=HI)
        b_ref = b[ref_row:ref_row + 1, :]
        b_last = b[last_row:last_row + 1, :]
        q = q_ref[rows, :] * GLA_QSCALE
        k = k_ref[rows, :]
        qe = (q * jnp.exp(b - b_ref)).astype(BF16)
        ke = (k * jnp.exp(b_ref - b)).astype(BF16)
        qin = (q * jnp.exp(b)).astype(BF16)
        kst = k * jnp.exp(b_last - b)
        for h in range(GLA_HEADS):
            hs = slice(h * GLA_DK, (h + 1) * GLA_DK)
            vs = slice(h * GLA_DV, (h + 1) * GLA_DV)
            v = v_ref[rows, vs].astype(BF16)
            att = lax.dot_general(qe[:, hs], ke[:, hs], (((1,), (1,)), ((), ())),
                                  preferred_element_type=F32)
            att = jnp.where(keep, att, 0.0).astype(BF16)
            s = s_scr[h]
            o = (jnp.dot(att, v, preferred_element_type=F32)
                 + jnp.dot(qin[:, hs], s.astype(BF16), preferred_element_type=F32))
            d_col = jnp.exp(jnp.broadcast_to(b_last[:, hs], (GLA_DK, GLA_DK))).T
            s_scr[h] = (jnp.concatenate([d_col, d_col], axis=1) * s
                        + jnp.dot(kst[:, hs].T.astype(BF16), v, preferred_element_type=F32))
            if final:
                o = o + of_ref[rows, vs]
                y = o * lax.rsqrt(jnp.mean(o * o, axis=-1, keepdims=True) + NORM_EPS) * ng_ref[...]
                o_ref[rows, vs] = (y * _silu(g_ref[rows, vs])).astype(o_ref.dtype)
            else:
                o_ref[rows, vs] = o
    sfin_ref[...] = s_scr[...]


def gla_pass(p, wa, ba, s0, reverse, o_fwd=None, norm_g=None):
    bsz, l, _ = p.shape
    t = _tile(l, 256)
    nt = l // t
    final = o_fwd is not None

    def rix(i):
        return nt - 1 - i if reverse else i

    def col(width, off):
        return pl.BlockSpec((None, t, width), lambda b, i: (b, rix(i), off // width))

    hq = GLA_HEADS * GLA_DK
    state = pl.BlockSpec((None, GLA_HEADS, GLA_DK, GLA_DV), lambda b, i: (b, 0, 0, 0))
    in_specs = [col(hq, COL_Q), col(hq, COL_K), col(D_GLA, COL_V), col(AL_W, COL_AL),
                pl.BlockSpec((AL_W, hq), lambda b, i: (0, 0)),
                pl.BlockSpec((1, hq), lambda b, i: (0, 0)), state]
    args = [p, p, p, p, wa, ba, s0]
    orow = pl.BlockSpec((None, t, D_GLA), lambda b, i: (b, rix(i), 0))
    if final:
        in_specs += [orow, col(D_GLA, COL_G), pl.BlockSpec((1, GLA_DV), lambda b, i: (0, 0))]
        args += [o_fwd, p, norm_g.reshape(1, GLA_DV)]
    return pl.pallas_call(
        functools.partial(_gla_body, reverse=reverse, final=final, nchunk=t // GLA_CHUNK),
        grid=(bsz, nt), in_specs=in_specs, out_specs=[orow, state],
        out_shape=[jax.ShapeDtypeStruct((bsz, l, D_GLA), BF16 if final else F32),
                   jax.ShapeDtypeStruct((bsz, GLA_HEADS, GLA_DK, GLA_DV), F32)],
        scratch_shapes=[pltpu.VMEM((GLA_HEADS, GLA_DK, GLA_DV), F32)],
        compiler_params=_params(("arbitrary", "arbitrary")), name="gla",
    )(*args)


def _lru_body(*refs, reverse, final, t, nt):
    if final:
        (x_ref, xp_ref, xn_ref, cw_ref, cb_ref, wg_ref, bg_ref, lam_ref, h0_ref, hf_ref, gb_ref,
         o_ref, hfin_ref, h_scr) = refs
    else:
        (x_ref, xp_ref, xn_ref, cw_ref, cb_ref, wg_ref, bg_ref, lam_ref, h0_ref,
         o_ref, hfin_ref, h_scr) = refs
    i = pl.program_id(1)
    ti = nt - 1 - i if reverse else i

    @pl.when(i == 0)
    def _():
        h_scr[...] = h0_ref[...]

    x = x_ref[...]
    xp = jnp.where(ti == 0, 0.0, xp_ref[...])
    xn = jnp.where(ti == nt - 1, 0.0, xn_ref[...])
    ext = jnp.concatenate([xp, x, xn], axis=0)
    xc = cb_ref[...]
    for j in range(LRU_CONV):
        off = SUBLANES - 2 + j
        xc = xc + ext[off:off + t, :] * cw_ref[j:j + 1, :]

    zs = []
    for n in range(LRU_BLOCKS):
        zs.append(jnp.dot(xc[:, n * LRU_BW:(n + 1) * LRU_BW], wg_ref[n],
                          preferred_element_type=F32, precision=HI))
    z_r = jnp.concatenate([z[:, :LRU_BW] for z in zs], axis=1) + bg_ref[0:1, :]
    z_i = jnp.concatenate([z[:, LRU_BW:] for z in zs], axis=1) + bg_ref[1:2, :]
    log_a = -LRU_C * jax.nn.sigmoid(z_r) * _softplus(-lam_ref[...])
    a = jnp.exp(log_a)
    th = jnp.tanh(log_a)
    one_minus_a2 = -2.0 * th / (1.0 - th)
    u = jnp.sqrt(one_minus_a2) * jax.nn.sigmoid(z_i) * xc

    row = lax.broadcasted_iota(jnp.int32, (t, 1), 0)
    s = 1
    while s < t:
        if reverse:
            ok = row < t - s
            a_s = pltpu.roll(a, t - s, axis=0)
            u_s = pltpu.roll(u, t - s, axis=0)
        else:
            ok = row >= s
            a_s = pltpu.roll(a, s, axis=0)
            u_s = pltpu.roll(u, s, axis=0)
        u = jnp.where(ok, a * u_s + u, u)
        a = jnp.where(ok, a * a_s, a)
        s *= 2
    h = a * h_scr[...] + u
    edge = 0 if reverse else t - 1
    h_scr[...] = h[edge:edge + 1, :]
    hfin_ref[...] = h[edge:edge + 1, :]
    if final:
        g = gb_ref[...]
        gelu = 0.5 * g * (1.0 + jnp.tanh(0.7978845608028654 * (g + 0.044715 * g * g * g)))
        o_ref[...] = (gelu * (hf_ref[...] + h)).astype(o_ref.dtype)
    else:
        o_ref[...] = h


def lru_pass(p, cw, cb, wg, bg, lam, h0, reverse, h_fwd=None):
    bsz, l, _ = p.shape
    t = _tile(l, 256)
    nt = l // t
    tb = t // SUBLANES
    nb8 = l // SUBLANES
    final = h_fwd is not None
    xoff = COL_XB // D_LRU

    def rix(i):
        return nt - 1 - i if reverse else i

    vec = lambda r: pl.BlockSpec((r, D_LRU), lambda b, i: (0, 0))
    one = pl.BlockSpec((None, 1, D_LRU), lambda b, i: (b, 0, 0))
    orow = pl.BlockSpec((None, t, D_LRU), lambda b, i: (b, rix(i), 0))
    in_specs = [pl.BlockSpec((None, t, D_LRU), lambda b, i: (b, rix(i), xoff)),
                pl.BlockSpec((None, SUBLANES, D_LRU),
                             lambda b, i: (b, jnp.maximum(rix(i) * tb - 1, 0), xoff)),
                pl.BlockSpec((None, SUBLANES, D_LRU),
                             lambda b, i: (b, jnp.minimum((rix(i) + 1) * tb, nb8 - 1), xoff)),
                vec(LRU_CONV), vec(1),
                pl.BlockSpec((LRU_BLOCKS, LRU_BW, 2 * LRU_BW), lambda b, i: (0, 0, 0)),
                vec(2), vec(1), one]
    args = [p, p, p, cw, cb.reshape(1, D_LRU), wg, bg, lam.reshape(1, D_LRU), h0]
    if final:
        in_specs += [orow, pl.BlockSpec((None, t, D_LRU), lambda b, i: (b, rix(i), COL_GB // D_LRU))]
        args += [h_fwd, p]
    return pl.pallas_call(
        functools.partial(_lru_body, reverse=reverse, final=final, t=t, nt=nt),
        grid=(bsz, nt), in_specs=in_specs, out_specs=[orow, one],
        out_shape=[jax.ShapeDtypeStruct((bsz, l, D_LRU), BF16 if final else F32),
                   jax.ShapeDtypeStruct((bsz, 1, D_LRU), F32)],
        scratch_shapes=[pltpu.VMEM((1, D_LRU), F32)],
        compiler_params=_params(("arbitrary", "arbitrary")), name="lru",
    )(*args)


def _rms(x, width):
    return x * lax.rsqrt(jnp.sum(x * x, axis=-1, keepdims=True) * (1.0 / width) + NORM_EPS)


def _rope(r, cos, sin):
    lane = lax.broadcasted_iota(jnp.int32, r.shape, 1)
    first = (lane % 32) < 16
    rot = jnp.where(first, pltpu.roll(r, LANES - 16, axis=1), pltpu.roll(r, 16, axis=1))
    return r * cos + rot * sin


def _qproj_body(x_ref, gn_ref, w_ref, hg_ref, cos_ref, sin_ref, o_ref, xn_scr):
    @pl.when(pl.program_id(2) == 0)
    def _():
        xn_scr[...] = (_rms(x_ref[...], MLA_Q_RANK) * gn_ref[...]).astype(BF16)

    a = jnp.dot(xn_scr[...], w_ref[...], preferred_element_type=F32)
    qn = _rms(a[:, :MLA_NOPE], MLA_NOPE) * hg_ref[:, :MLA_NOPE]
    qr = _rms(a[:, MLA_NOPE:], MLA_ROPE) * hg_ref[:, MLA_NOPE:]
    qr = _rope(qr, cos_ref[...], sin_ref[...])
    o_ref[...] = (jnp.concatenate([qn, qr], axis=1) * MLA_SCALE).astype(o_ref.dtype)


def q_proj(p, gn, w, hg, cos, sin):
    bsz, l, _ = p.shape
    tm = _tile(l, 512)
    return pl.pallas_call(
        _qproj_body, grid=(bsz, l // tm, MLA_HEADS),
        in_specs=[pl.BlockSpec((None, tm, QDN_W), lambda b, i, h: (b, i, COL_QDN // QDN_W)),
                  pl.BlockSpec((1, QDN_W), lambda b, i, h: (0, 0)),
                  pl.BlockSpec((QDN_W, HEAD_W), lambda b, i, h: (0, h)),
                  pl.BlockSpec((1, HEAD_W), lambda b, i, h: (0, 0)),
                  pl.BlockSpec((tm, LANES), lambda b, i, h: (i, 0)),
                  pl.BlockSpec((tm, LANES), lambda b, i, h: (i, 0))],
        out_specs=pl.BlockSpec((None, None, tm, HEAD_W), lambda b, i, h: (b, h, i, 0)),
        out_shape=jax.ShapeDtypeStruct((bsz, MLA_HEADS, l, HEAD_W), BF16),
        scratch_shapes=[pltpu.VMEM((tm, QDN_W), BF16)],
        compiler_params=_params(("arbitrary", "arbitrary", "arbitrary")), name="q_proj",
    )(p, gn.reshape(1, QDN_W), w, hg, cos, sin)


def _kvproj_body(x_ref, kr_ref, gn_ref, w_ref, hg_ref, cos_ref, sin_ref, k_ref, v_ref, xn_scr, kr_scr):
    @pl.when(pl.program_id(2) == 0)
    def _():
        xn_scr[...] = (_rms(x_ref[...], MLA_KV_RANK) * gn_ref[...]).astype(BF16)
        kr = _rms(kr_ref[...], MLA_ROPE) * hg_ref[:, MLA_NOPE:]
        kr_scr[...] = _rope(kr, cos_ref[...], sin_ref[...]).astype(BF16)

    a = jnp.dot(xn_scr[...], w_ref[...], preferred_element_type=F32)
    kn = _rms(a[:, :MLA_NOPE], MLA_NOPE) * hg_ref[:, :MLA_NOPE]
    k_ref[:, :MLA_NOPE] = kn.astype(k_ref.dtype)
    k_ref[:, MLA_NOPE:] = kr_scr[...]
    v_ref[...] = a[:, MLA_NOPE:].astype(v_ref.dtype)


def kv_proj(p, gn, w, hg, cos, sin):
    bsz, l, _ = p.shape
    tm = _tile(l, 512)
    return pl.pallas_call(
        _kvproj_body, grid=(bsz, l // tm, MLA_HEADS),
        in_specs=[pl.BlockSpec((None, tm, MLA_KV_RANK), lambda b, i, h: (b, i, COL_KVDN // MLA_KV_RANK)),
                  pl.BlockSpec((None, tm, KR_W), lambda b, i, h: (b, i, COL_KR // KR_W)),
                  pl.BlockSpec((1, MLA_KV_RANK), lambda b, i, h: (0, 0)),
                  pl.BlockSpec((MLA_KV_RANK, HEAD_W), lambda b, i, h: (0, h)),
                  pl.BlockSpec((1, HEAD_W), lambda b, i, h: (0, 0)),
                  pl.BlockSpec((tm, LANES), lambda b, i, h: (i, 0)),
                  pl.BlockSpec((tm, LANES), lambda b, i, h: (i, 0))],
        out_specs=[pl.BlockSpec((None, None, tm, HEAD_W), lambda b, i, h: (b, h, i, 0)),
                   pl.BlockSpec((None, None, tm, MLA_V), lambda b, i, h: (b, h, i, 0))],
        out_shape=[jax.ShapeDtypeStruct((bsz, MLA_HEADS, l, HEAD_W), BF16),
                   jax.ShapeDtypeStruct((bsz, MLA_HEADS, l, MLA_V), BF16)],
        scratch_shapes=[pltpu.VMEM((tm, MLA_KV_RANK), BF16), pltpu.VMEM((tm, KR_W), BF16)],
        compiler_params=_params(("arbitrary", "arbitrary", "arbitrary")), name="kv_proj",
    )(p, p, gn.reshape(1, MLA_KV_RANK), w, hg, cos, sin)


def _scores(q, k):
    return lax.dot_general(q, k, (((1,), (1,)), ((), ())), preferred_element_type=F32)


def _attn_body(*refs, tk, n_lat):
    if n_lat:
        q_ref, kc_ref, vc_ref, k_ref, v_ref, o_ref = refs
    else:
        q_ref, kc_ref, vc_ref, o_ref = refs
    q = q_ref[...]
    s = _scores(q, kc_ref[...])
    m = jnp.max(s, axis=-1, keepdims=True)
    p = jnp.exp(s - m)
    l = jnp.sum(p, axis=-1, keepdims=True)
    acc = jnp.dot(p.astype(BF16), vc_ref[...], preferred_element_type=F32)

    if n_lat:
        def step(j, carry):
            m, l, acc = carry
            off = pl.multiple_of(j * tk, tk)
            s = _scores(q, k_ref[pl.ds(off, tk), :])
            m_new = jnp.maximum(m, jnp.max(s, axis=-1, keepdims=True))
            alpha = jnp.exp(m - m_new)
            p = jnp.exp(s - m_new)
            l = alpha * l + jnp.sum(p, axis=-1, keepdims=True)
            acc = alpha * acc + jnp.dot(p.astype(BF16), v_ref[pl.ds(off, tk), :],
                                        preferred_element_type=F32)
            return m_new, l, acc

        m, l, acc = lax.fori_loop(0, n_lat, step, (m, l, acc))
    o_ref[...] = (acc / l).astype(o_ref.dtype)


def attention(q, kc, vc, k=None, v=None):
    bsz, nh, lq, _ = q.shape
    lc = kc.shape[2]
    tq = _tile(lq, 512)
    head = lambda n, w: pl.BlockSpec((None, None, n, w), lambda b, h, i: (b, h, 0, 0))
    in_specs = [pl.BlockSpec((None, None, tq, HEAD_W), lambda b, h, i: (b, h, i, 0)),
                head(lc, HEAD_W), head(lc, MLA_V)]
    args = [q, kc, vc]
    tk = n_lat = 0
    if k is not None:
        lk = k.shape[2]
        tk = _tile(lk, 512)
        n_lat = lk // tk
        in_specs += [head(lk, HEAD_W), head(lk, MLA_V)]
        args += [k, v]
    return pl.pallas_call(
        functools.partial(_attn_body, tk=tk, n_lat=n_lat),
        grid=(bsz, nh, lq // tq), in_specs=in_specs,
        out_specs=pl.BlockSpec((None, tq, MLA_V), lambda b, h, i: (b, i, h)),
        out_shape=jax.ShapeDtypeStruct((bsz, lq, nh * MLA_V), BF16),
        compiler_params=_params(("arbitrary", "arbitrary", "arbitrary")), name="attention",
    )(*args)


def _moe_body(be_ref, nu_ref, x_ref, w13_ref, w2_ref, g_ref, o_ref):
    j = pl.program_id(0)

    @pl.when(j < nu_ref[0])
    def _():
        hid = jnp.dot(x_ref[...], w13_ref[...].astype(BF16), preferred_element_type=F32)
        act = (_silu(hid[:, :D_EXPERT]) * hid[:, D_EXPERT:]).astype(BF16)
        out = jnp.dot(act, w2_ref[...].astype(BF16), preferred_element_type=F32)
        o_ref[...] = out * g_ref[...]

    @pl.when(j >= nu_ref[0])
    def _():
        o_ref[...] = jnp.zeros_like(o_ref)


def moe_experts(xs, gate, blk_exp, n_used, w13, w2):
    p, d = xs.shape
    nb = p // MOE_BLOCK
    grid_spec = pltpu.PrefetchScalarGridSpec(
        num_scalar_prefetch=2, grid=(nb,),
        in_specs=[pl.BlockSpec((MOE_BLOCK, d), lambda j, be, nu: (j, 0)),
                  pl.BlockSpec((None, d, 2 * D_EXPERT), lambda j, be, nu: (be[j], 0, 0)),
                  pl.BlockSpec((None, D_EXPERT, d), lambda j, be, nu: (be[j], 0, 0)),
                  pl.BlockSpec((MOE_BLOCK, 1), lambda j, be, nu: (j, 0))],
        out_specs=pl.BlockSpec((MOE_BLOCK, d), lambda j, be, nu: (j, 0)))
    return pl.pallas_call(
        _moe_body, grid_spec=grid_spec,
        out_shape=jax.ShapeDtypeStruct((p, d), F32),
        compiler_params=_params(("arbitrary",)), name="moe_experts",
    )(blk_exp, n_used, xs, w13, w2, gate)


def moe_ffn(h, logits, w13, w2):
    m, d = h.shape
    g_logit = logits[:, :N_GROUPS]
    e_logit = logits[:, N_GROUPS:N_GROUPS + N_EXPERTS].reshape(m, N_GROUPS, EXPERTS_PER_GROUP)
    g_idx = jnp.argmax(g_logit, axis=-1)
    p_grp = jnp.take_along_axis(jax.nn.softmax(g_logit, axis=-1), g_idx[:, None], axis=1)
    e_sel = jnp.take_along_axis(e_logit, g_idx[:, None, None], axis=1)[:, 0]
    top_v, top_i = lax.top_k(e_sel, TOP_K)
    gate = (p_grp * jax.nn.softmax(top_v, axis=-1)).reshape(-1)
    e_flat = (g_idx[:, None] * EXPERTS_PER_GROUP + top_i).reshape(-1).astype(jnp.int32)

    a = m * TOP_K
    nb = -(-a // MOE_BLOCK) + N_EXPERTS
    order = jnp.argsort(e_flat)
    e_s = e_flat[order]
    counts = jnp.bincount(e_flat, length=N_EXPERTS)
    start = jnp.cumsum(counts) - counts
    padded = -(-counts // MOE_BLOCK) * MOE_BLOCK
    pad_end = jnp.cumsum(padded)
    dest_s = ((pad_end - padded)[e_s] + (jnp.arange(a) - start[e_s])).astype(jnp.int32)
    tok_s = (order // TOP_K).astype(jnp.int32)
    row_tok = jnp.full((nb * MOE_BLOCK,), m, jnp.int32).at[dest_s].set(tok_s)
    row_gate = jnp.zeros((nb * MOE_BLOCK,), F32).at[dest_s].set(gate[order])
    dest = jnp.zeros((a,), jnp.int32).at[order].set(dest_s).reshape(m, TOP_K)
    blk_exp = jnp.minimum(jnp.searchsorted(pad_end, jnp.arange(nb) * MOE_BLOCK, side='right'),
                          N_EXPERTS - 1).astype(jnp.int32)
    n_used = (pad_end[-1] // MOE_BLOCK).astype(jnp.int32).reshape(1)

    hp = jnp.concatenate([h, jnp.zeros((1, d), h.dtype)], axis=0)
    out = moe_experts(hp[row_tok], row_gate[:, None], blk_exp, n_used, w13, w2)
    return out[dest[:, 0]] + out[dest[:, 1]]


def _prep_w_in(w):
    sizes = (512, 512, 1024, 1024, 32, 768, 512, 64, 1024, 1024)
    offs = [0]
    for s in sizes:
        offs.append(offs[-1] + s)
    q, k, v, g, al, qdn, kvdn, kr, xb, gb = [w[:, offs[i]:offs[i + 1]] for i in range(10)]
    zpad = lambda t, n: jnp.pad(t, ((0, 0), (0, n - t.shape[1])))
    return jnp.concatenate([qdn, zpad(al, AL_W), zpad(kr, KR_W), v, g, xb, gb, q, k, kvdn],
                           axis=1).astype(BF16)


def _prep_w_uq(w):
    w = w.reshape(MLA_Q_RANK, MLA_HEADS, MLA_NOPE + MLA_ROPE)
    w = jnp.pad(w, ((0, 0), (0, 0), (0, HEAD_W - MLA_NOPE - MLA_ROPE)))
    return w.reshape(MLA_Q_RANK, MLA_HEADS * HEAD_W).astype(BF16)


def _head_gain(g):
    return jnp.pad(g, (0, HEAD_W - g.shape[0])).reshape(1, HEAD_W)


def _rope_tables(rows):
    n = MLA_ROPE // 4
    inv = ROPE_BASE ** (-jnp.arange(n, dtype=F32) / n)
    r = jnp.repeat(jnp.arange(rows, dtype=F32), GRID_W)
    c = jnp.tile(jnp.arange(GRID_W, dtype=F32), rows)
    ar, ac = r[:, None] * inv, c[:, None] * inv
    l = ar.shape[0]
    cos = jnp.concatenate([jnp.cos(ar), jnp.cos(ar), jnp.cos(ac), jnp.cos(ac),
                           jnp.ones((l, LANES - MLA_ROPE), F32)], axis=1)
    sin = jnp.concatenate([-jnp.sin(ar), jnp.sin(ar), -jnp.sin(ac), jnp.sin(ac),
                           jnp.zeros((l, LANES - MLA_ROPE), F32)], axis=1)
    return cos, sin


def _gla_gate_w(w_a2, b_a, d):
    hq = GLA_HEADS * GLA_DK
    w = jnp.zeros((AL_W, hq), F32).at[d * GLA_GATE_RANK:(d + 1) * GLA_GATE_RANK].set(w_a2[d])
    return w, b_a[d].reshape(1, hq)


def _lru_gate_w(w_gate, d):
    return jnp.concatenate([w_gate[d, 0], w_gate[d, 1]], axis=-1)


def _layer(x, xc, mods, rope, prm, last):
    (norm1_g, norm2_g, w_in, gla_w_a2, gla_b_a, gla_norm_g,
     mla_q_norm_g, mla_kv_norm_g, mla_w_uq, mla_w_ukv, mla_q_head_g, mla_k_head_g,
     lru_conv_w, lru_conv_b, lru_w_gate, lru_b_gate, lru_lambda, w_out,
     moe_w_grp, moe_b_grp, moe_w_exp, moe_b_exp, moe_w13, moe_w2) = prm
    bsz, l, d = x.shape
    lc = xc.shape[1]
    ctx_out = not last
    lat = lambda k: mods[:bsz, k].reshape(bsz, 1, d)
    ctx = lambda k: jnp.broadcast_to(mods[bsz, k].reshape(1, 1, d), (bsz, 1, d))

    w_in_p = _prep_w_in(w_in)
    h = norm_mod(x, norm1_g, lat(0), lat(1))
    hc = norm_mod(xc, norm1_g, ctx(0), ctx(1))
    p = matmul(h.reshape(bsz * l, d), w_in_p).reshape(bsz, l, D_INP)
    pc = matmul(hc.reshape(bsz * lc, d), w_in_p).reshape(bsz, lc, D_INP)

    waf, baf = _gla_gate_w(gla_w_a2, gla_b_a, 0)
    wab, bab = _gla_gate_w(gla_w_a2, gla_b_a, 1)
    s_zero = jnp.zeros((bsz, GLA_HEADS, GLA_DK, GLA_DV), F32)
    ocf, s_f = gla_pass(pc, waf, baf, s_zero, False)
    of, _ = gla_pass(p, waf, baf, s_f, False)
    if ctx_out:
        yac, s_b = gla_pass(pc, wab, bab, s_zero, True, ocf, gla_norm_g)
    else:
        _, s_b = gla_pass(pc, wab, bab, s_zero, True)
    ya, _ = gla_pass(p, wab, bab, s_b, True, of, gla_norm_g)

    wgf, wgb = _lru_gate_w(lru_w_gate, 0), _lru_gate_w(lru_w_gate, 1)
    h_zero = jnp.zeros((bsz, 1, D_LRU), F32)
    lru = functools.partial(lru_pass, cw=lru_conv_w, cb=lru_conv_b)
    hcf, h0f = lru(pc, wg=wgf, bg=lru_b_gate[0], lam=lru_lambda[0], h0=h_zero, reverse=False)
    hf, _ = lru(p, wg=wgf, bg=lru_b_gate[0], lam=lru_lambda[0], h0=h0f, reverse=False)
    if ctx_out:
        ylc, h0b = lru(pc, wg=wgb, bg=lru_b_gate[1], lam=lru_lambda[1], h0=h_zero, reverse=True, h_fwd=hcf)
    else:
        _, h0b = lru(pc, wg=wgb, bg=lru_b_gate[1], lam=lru_lambda[1], h0=h_zero, reverse=True)
    yl, _ = lru(p, wg=wgb, bg=lru_b_gate[1], lam=lru_lambda[1], h0=h0b, reverse=True, h_fwd=hf)

    cos, sin = rope
    cos_c = jnp.ones((lc, LANES), F32)
    sin_c = jnp.zeros((lc, LANES), F32)
    w_uq = _prep_w_uq(mla_w_uq)
    w_ukv = mla_w_ukv.astype(BF16)
    qg, kg = _head_gain(mla_q_head_g), _head_gain(mla_k_head_g)
    kc, vc = kv_proj(pc, mla_kv_norm_g, w_ukv, kg, cos_c, sin_c)
    kl, vl = kv_proj(p, mla_kv_norm_g, w_ukv, kg, cos, sin)
    ql = q_proj(p, mla_q_norm_g, w_uq, qg, cos, sin)
    yb = attention(ql, kc, vc, kl, vl)

    w_out_b = w_out.astype(BF16)
    x = out_proj(ya, yb, yl, w_out_b, x, lat(2))
    if ctx_out:
        qc = q_proj(pc, mla_q_norm_g, w_uq, qg, cos_c, sin_c)
        ybc = attention(qc, kc, vc)
        xc = out_proj(yac, ybc, ylc, w_out_b, xc, ctx(2))

    nr = LANES
    wr = jnp.pad(jnp.concatenate([moe_w_grp, moe_w_exp], axis=1), ((0, 0), (0, nr - N_GROUPS - N_EXPERTS)))
    br = jnp.pad(jnp.concatenate([moe_b_grp, moe_b_exp]), (0, nr - N_GROUPS - N_EXPERTS)).reshape(1, nr)
    h2, lg = norm_mod(x, norm2_g, lat(3), lat(4), router=(wr, br))
    if ctx_out:
        h2c, lgc = norm_mod(xc, norm2_g, ctx(3), ctx(4), router=(wr, br))
        out = moe_ffn(jnp.concatenate([h2c.reshape(-1, d), h2.reshape(-1, d)], axis=0),
                      jnp.concatenate([lgc.reshape(-1, nr), lg.reshape(-1, nr)], axis=0), moe_w13, moe_w2)
        xc = xc + ctx(5) * out[:bsz * lc].reshape(bsz, lc, d)
        x = x + lat(5) * out[bsz * lc:].reshape(bsz, l, d)
    else:
        x = x + lat(5) * moe_ffn(h2.reshape(-1, d), lg.reshape(-1, nr), moe_w13, moe_w2).reshape(bsz, l, d)
    return x, xc


def kernel(x, c, ctx, c_ctx, w_mod, b_mod, norm1_g, norm2_g, w_in, gla_w_a2, gla_b_a, gla_norm_g,
           mla_q_norm_g, mla_kv_norm_g, mla_w_uq, mla_w_ukv, mla_q_head_g, mla_k_head_g,
           lru_conv_w, lru_conv_b, lru_w_gate, lru_b_gate, lru_lambda, w_out,
           moe_w_grp, moe_b_grp, moe_w_exp, moe_b_exp, moe_w13, moe_w2):
    bsz, l, d = x.shape
    depth = w_mod.shape[0]
    assert bsz + 1 <= SUBLANES and l % GRID_W == 0
    cond = jnp.zeros((SUBLANES, d), F32).at[:bsz].set(c).at[bsz].set(c_ctx)
    mods = adaln_all(cond, w_mod, b_mod).reshape(depth, SUBLANES, 6, d)
    rope = _rope_tables(l // GRID_W)
    per_layer = (norm1_g, norm2_g, w_in, gla_w_a2, gla_b_a, gla_norm_g,
                 mla_q_norm_g, mla_kv_norm_g, mla_w_uq, mla_w_ukv, mla_q_head_g, mla_k_head_g,
                 lru_conv_w, lru_conv_b, lru_w_gate, lru_b_gate, lru_lambda, w_out,
                 moe_w_grp, moe_b_grp, moe_w_exp, moe_b_exp, moe_w13, moe_w2)
    xc = ctx
    for i in range(depth):
        x, xc = _layer(x, xc, mods[i], rope, tuple(t[i] for t in per_layer), last=(i == depth - 1))
    return x
```

```python
import functools

import jax
import jax.numpy as jnp
from jax import lax
from jax.experimental import pallas as pl
from jax.experimental.pallas import tpu as pltpu

F32 = jnp.float32
BF16 = jnp.bfloat16
HI = lax.Precision.HIGHEST

D_MODEL = 4096
GRID_W = 64
NORM_EPS = 1e-6
D_GLA = D_MODEL // 4
D_MLA = D_MODEL // 2
D_LRU = D_MODEL // 4
GLA_HEADS = 4
GLA_DK = 128
GLA_DV = 256
GLA_GATE_RANK = 16
GLA_TAU = 16.0
GLA_CHUNK = 64
GLA_QSCALE = GLA_DK ** -0.5
MLA_HEADS = 16
MLA_NOPE = 128
MLA_ROPE = 64
MLA_V = 128
MLA_Q_RANK = 768
MLA_KV_RANK = 512
MLA_SCALE = (MLA_NOPE + MLA_ROPE) ** -0.5
LOG2E = 1.4426950408889634
ROPE_BASE = 10000.0
LRU_BLOCKS = 8
LRU_BW = 128
LRU_CONV = 4
LRU_C = 8.0
N_GROUPS = 8
EXPERTS_PER_GROUP = 8
N_EXPERTS = 64
TOP_K = 2
D_EXPERT = 256

LANES = 128
SUBLANES = 8
VMEM_BYTES = 64 << 20
VMEM_LIMIT = VMEM_BYTES - (8 << 20)
MOE_BLOCK = 256

QDN_W, AL_W, KR_W = MLA_Q_RANK, LANES, LANES
COL_QDN = 0
COL_AL = 768
COL_KR = 896
COL_V = 1024
COL_G = 2048
COL_XB = 3072
COL_GB = 4096
COL_Q = 5120
COL_K = 5632
COL_KVDN = 6144
D_INP = 6656
HEAD_W = 256
VT_ROWS = MLA_V + 16


def _params(sem, vmem=VMEM_LIMIT):
    return pltpu.CompilerParams(dimension_semantics=sem, vmem_limit_bytes=vmem)


def _tile(n, pref):
    t = min(n, pref)
    while n % t or t % SUBLANES:
        t -= 1
    return t


def _silu(x):
    return x * jax.nn.sigmoid(x)


def _softplus(x):
    return jnp.maximum(x, 0.0) + jnp.log1p(jnp.exp(-jnp.abs(x)))


def _adaln_body(c_ref, w_ref, b_ref, o_ref):
    s = _silu(c_ref[...])
    o_ref[...] = jnp.dot(s, w_ref[...], preferred_element_type=F32, precision=HI) + b_ref[...]


def adaln_all(cond, w_mod, b_mod):
    depth, d, n = w_mod.shape
    tn = 512
    return pl.pallas_call(
        _adaln_body,
        grid=(depth, n // tn),
        in_specs=[pl.BlockSpec((SUBLANES, d), lambda l, j: (0, 0)),
                  pl.BlockSpec((None, d, tn), lambda l, j: (l, 0, j)),
                  pl.BlockSpec((None, 1, tn), lambda l, j: (l, 0, j))],
        out_specs=pl.BlockSpec((None, SUBLANES, tn), lambda l, j: (l, 0, j)),
        out_shape=jax.ShapeDtypeStruct((depth, SUBLANES, n), F32),
        compiler_params=_params(("arbitrary", "arbitrary")),
        name="adaln",
    )(cond, w_mod, b_mod.reshape(depth, 1, n))


def _norm_mod(x, g, shift, scale):
    y = x * lax.rsqrt(jnp.mean(x * x, axis=-1, keepdims=True) + NORM_EPS) * g
    return y * (1.0 + scale) + shift


def _norm_mod_body(x_ref, g_ref, sh_ref, sc_ref, o_ref):
    o_ref[...] = _norm_mod(x_ref[...], g_ref[...], sh_ref[...], sc_ref[...]).astype(o_ref.dtype)


def _norm_router_body(x_ref, g_ref, sh_ref, sc_ref, wr_ref, br_ref, o_ref, lg_ref):
    h = _norm_mod(x_ref[...], g_ref[...], sh_ref[...], sc_ref[...])
    o_ref[...] = h.astype(o_ref.dtype)
    lg_ref[...] = jnp.dot(h, wr_ref[...], preferred_element_type=F32, precision=HI) + br_ref[...]


def norm_mod(x, g, shift, scale, router=None):
    bx, lx, d = x.shape
    tm = _tile(lx, 256)
    row = pl.BlockSpec((None, tm, d), lambda b, i: (b, i, 0))
    vec = pl.BlockSpec((1, d), lambda b, i: (0, 0))
    mod = pl.BlockSpec((None, 1, d), lambda b, i: (b, 0, 0))
    if router is None:
        return pl.pallas_call(
            _norm_mod_body, grid=(bx, lx // tm),
            in_specs=[row, vec, mod, mod], out_specs=row,
            out_shape=jax.ShapeDtypeStruct((bx, lx, d), BF16),
            compiler_params=_params(("arbitrary", "arbitrary")), name="norm_mod",
        )(x, g.reshape(1, d), shift, scale)
    wr, br = router
    nr = wr.shape[1]
    return pl.pallas_call(
        _norm_router_body, grid=(bx, lx // tm),
        in_specs=[row, vec, mod, mod,
                  pl.BlockSpec((d, nr), lambda b, i: (0, 0)),
                  pl.BlockSpec((1, nr), lambda b, i: (0, 0))],
        out_specs=[row, pl.BlockSpec((None, tm, nr), lambda b, i: (b, i, 0))],
        out_shape=[jax.ShapeDtypeStruct((bx, lx, d), BF16), jax.ShapeDtypeStruct((bx, lx, nr), F32)],
        compiler_params=_params(("arbitrary", "arbitrary")), name="norm_router",
    )(x, g.reshape(1, d), shift, scale, wr, br)


def _mm_body(a_ref, w_ref, o_ref):
    o_ref[...] = jnp.dot(a_ref[...], w_ref[...], preferred_element_type=F32).astype(o_ref.dtype)


def matmul(a, w, out_dtype=F32, tm_pref=1024, tn=512):
    m, k = a.shape
    n = w.shape[1]
    tm = _tile(m, tm_pref)
    return pl.pallas_call(
        _mm_body, grid=(m // tm, n // tn),
        in_specs=[pl.BlockSpec((tm, k), lambda i, j: (i, 0)),
                  pl.BlockSpec((k, tn), lambda i, j: (0, j))],
        out_specs=pl.BlockSpec((tm, tn), lambda i, j: (i, j)),
        out_shape=jax.ShapeDtypeStruct((m, n), out_dtype),
        compiler_params=_params(("arbitrary", "arbitrary")), name="matmul",
    )(a, w)


def _wout_body(ya_ref, yb_ref, yl_ref, w_ref, x_ref, g_ref, o_ref):
    acc = jnp.dot(ya_ref[...], w_ref[0:D_GLA, :], preferred_element_type=F32)
    acc += jnp.dot(yb_ref[...], w_ref[D_GLA:D_GLA + D_MLA, :], preferred_element_type=F32)
    acc += jnp.dot(yl_ref[...], w_ref[D_GLA + D_MLA:, :], preferred_element_type=F32)
    o_ref[...] = x_ref[...] + g_ref[...] * acc


def out_proj(ya, yb, yl, w, x, gate):
    bx, lx, d = x.shape
    tm = _tile(lx, 1024)
    tn = 512
    return pl.pallas_call(
        _wout_body, grid=(bx, lx // tm, d // tn),
        in_specs=[pl.BlockSpec((None, tm, D_GLA), lambda b, i, j: (b, i, 0)),
                  pl.BlockSpec((None, tm, D_MLA), lambda b, i, j: (b, i, 0)),
                  pl.BlockSpec((None, tm, D_LRU), lambda b, i, j: (b, i, 0)),
                  pl.BlockSpec((d, tn), lambda b, i, j: (0, j)),
                  pl.BlockSpec((None, tm, tn), lambda b, i, j: (b, i, j)),
                  pl.BlockSpec((None, 1, tn), lambda b, i, j: (b, 0, j))],
        out_specs=pl.BlockSpec((None, tm, tn), lambda b, i, j: (b, i, j)),
        out_shape=jax.ShapeDtypeStruct((bx, lx, d), F32),
        compiler_params=_params(("arbitrary", "arbitrary", "arbitrary")), name="out_proj",
    )(ya, yb, yl, w, x, gate)


def _gla_body(*refs, reverse, final, nchunk):
    if final:
        (q_ref, k_ref, v_ref, al_ref, wa_ref, ba_ref, s0_ref, of_ref, g_ref, ng_ref,
         o_ref, sfin_ref, s_scr) = refs
    else:
        q_ref, k_ref, v_ref, al_ref, wa_ref, ba_ref, s0_ref, o_ref, sfin_ref, s_scr = refs
    C = GLA_CHUNK

    @pl.when(pl.program_id(1) == 0)
    def _():
        s_scr[...] = s0_ref[...]

    r_i = lax.broadcasted_iota(jnp.int32, (C, C), 0)
    c_i = lax.broadcasted_iota(jnp.int32, (C, C), 1)
    keep = (c_i >= r_i) if reverse else (c_i <= r_i)
    tri = keep.astype(F32)
    ref_row = C - 1 - C // 2 if reverse else C // 2
    last_row = 0 if reverse else C - 1

    order = range(nchunk - 1, -1, -1) if reverse else range(nchunk)
    for ci in order:
        rows = slice(ci * C, (ci + 1) * C)
        z = jnp.dot(al_ref[rows, :], wa_ref[...], preferred_element_type=F32, precision=HI) + ba_ref[...]
        logd = (jnp.minimum(z, 0.0) - jnp.log1p(jnp.exp(-jnp.abs(z)))) * (1.0 / GLA_TAU)
        b = jnp.dot(tri, logd, preferred_element_type=F32, precision=HI)
        b_ref = b[ref_row:ref_row + 1, :]
        b_last = b[last_row:last_row + 1, :]
        q = q_ref[rows, :] * GLA_QSCALE
        k = k_ref[rows, :]
        qe = (q * jnp.exp(b - b_ref)).astype(BF16)
        ke = (k * jnp.exp(b_ref - b)).astype(BF16)
        qin = (q * jnp.exp(b)).astype(BF16)
        kst = k * jnp.exp(b_last - b)
        for h in range(GLA_HEADS):
            hs = slice(h * GLA_DK, (h + 1) * GLA_DK)
            vs = slice(h * GLA_DV, (h + 1) * GLA_DV)
            v = v_ref[rows, vs].astype(BF16)
            att = lax.dot_general(qe[:, hs], ke[:, hs], (((1,), (1,)), ((), ())),
                                  preferred_element_type=F32)
            att = jnp.where(keep, att, 0.0).astype(BF16)
            s = s_scr[h]
            o = (jnp.dot(att, v, preferred_element_type=F32)
                 + jnp.dot(qin[:, hs], s.astype(BF16), preferred_element_type=F32))
            d_col = jnp.exp(jnp.broadcast_to(b_last[:, hs], (GLA_DK, GLA_DK))).T
            s_scr[h] = (jnp.concatenate([d_col, d_col], axis=1) * s
                        + jnp.dot(kst[:, hs].T.astype(BF16), v, preferred_element_type=F32))
            if final:
                o = o + of_ref[rows, vs]
                y = o * lax.rsqrt(jnp.mean(o * o, axis=-1, keepdims=True) + NORM_EPS) * ng_ref[...]
                o_ref[rows, vs] = (y * _silu(g_ref[rows, vs])).astype(o_ref.dtype)
            else:
                o_ref[rows, vs] = o
    sfin_ref[...] = s_scr[...]


def gla_pass(p, wa, ba, s0, reverse, o_fwd=None, norm_g=None):
    bsz, l, _ = p.shape
    t = _tile(l, 256)
    nt = l // t
    final = o_fwd is not None

    def rix(i):
        return nt - 1 - i if reverse else i

    def col(width, off):
        return pl.BlockSpec((None, t, width), lambda b, i: (b, rix(i), off // width))

    hq = GLA_HEADS * GLA_DK
    state = pl.BlockSpec((None, GLA_HEADS, GLA_DK, GLA_DV), lambda b, i: (b, 0, 0, 0))
    in_specs = [col(hq, COL_Q), col(hq, COL_K), col(D_GLA, COL_V), col(AL_W, COL_AL),
                pl.BlockSpec((AL_W, hq), lambda b, i: (0, 0)),
                pl.BlockSpec((1, hq), lambda b, i: (0, 0)), state]
    args = [p, p, p, p, wa, ba, s0]
    orow = pl.BlockSpec((None, t, D_GLA), lambda b, i: (b, rix(i), 0))
    if final:
        in_specs += [orow, col(D_GLA, COL_G), pl.BlockSpec((1, GLA_DV), lambda b, i: (0, 0))]
        args += [o_fwd, p, norm_g.reshape(1, GLA_DV)]
    return pl.pallas_call(
        functools.partial(_gla_body, reverse=reverse, final=final, nchunk=t // GLA_CHUNK),
        grid=(bsz, nt), in_specs=in_specs, out_specs=[orow, state],
        out_shape=[jax.ShapeDtypeStruct((bsz, l, D_GLA), BF16 if final else F32),
                   jax.ShapeDtypeStruct((bsz, GLA_HEADS, GLA_DK, GLA_DV), F32)],
        scratch_shapes=[pltpu.VMEM((GLA_HEADS, GLA_DK, GLA_DV), F32)],
        compiler_params=_params(("arbitrary", "arbitrary")), name="gla",
    )(*args)


def _lru_body(*refs, reverse, final, t, nt):
    if final:
        (x_ref, xp_ref, xn_ref, cw_ref, cb_ref, wg_ref, bg_ref, lam_ref, h0_ref, hf_ref, gb_ref,
         o_ref, hfin_ref, h_scr, hbuf) = refs
    else:
        (x_ref, xp_ref, xn_ref, cw_ref, cb_ref, wg_ref, bg_ref, lam_ref, h0_ref,
         o_ref, hfin_ref, h_scr) = refs
        hbuf = o_ref
    i = pl.program_id(1)
    ti = nt - 1 - i if reverse else i

    @pl.when(i == 0)
    def _():
        h_scr[...] = h0_ref[...]

    x = x_ref[...]
    xp = jnp.where(ti == 0, 0.0, xp_ref[...])
    xn = jnp.where(ti == nt - 1, 0.0, xn_ref[...])
    ext = jnp.concatenate([xp, x, xn], axis=0)
    xc = cb_ref[...]
    for j in range(LRU_CONV):
        off = SUBLANES - 2 + j
        xc = xc + ext[off:off + t, :] * cw_ref[j:j + 1, :]

    zs = []
    for n in range(LRU_BLOCKS):
        zs.append(jnp.dot(xc[:, n * LRU_BW:(n + 1) * LRU_BW], wg_ref[n],
                          preferred_element_type=F32, precision=HI))
    z_r = jnp.concatenate([z[:, :LRU_BW] for z in zs], axis=1) + bg_ref[0:1, :]
    z_i = jnp.concatenate([z[:, LRU_BW:] for z in zs], axis=1) + bg_ref[1:2, :]
    log_a = -LRU_C * jax.nn.sigmoid(z_r) * _softplus(-lam_ref[...])
    a = jnp.exp(log_a)
    th = jnp.tanh(log_a)
    one_minus_a2 = -2.0 * th / (1.0 - th)
    u = jnp.sqrt(one_minus_a2) * jax.nn.sigmoid(z_i) * xc

    sub = lax.broadcasted_iota(jnp.int32, (t, 1), 0) % SUBLANES
    s = 1
    while s < SUBLANES:
        if reverse:
            ok = sub < SUBLANES - s
            a_s = pltpu.roll(a, t - s, axis=0)
            u_s = pltpu.roll(u, t - s, axis=0)
        else:
            ok = sub >= s
            a_s = pltpu.roll(a, s, axis=0)
            u_s = pltpu.roll(u, s, axis=0)
        u = jnp.where(ok, a * u_s + u, u)
        a = jnp.where(ok, a * a_s, a)
        s *= 2
    ngroup = t // SUBLANES
    edge = 0 if reverse else SUBLANES - 1
    h_prev = h_scr[...]
    for gi in (range(ngroup - 1, -1, -1) if reverse else range(ngroup)):
        rows = slice(gi * SUBLANES, (gi + 1) * SUBLANES)
        hg = a[rows, :] * h_prev + u[rows, :]
        h_prev = hg[edge:edge + 1, :]
        hbuf[rows, :] = hg
    h_scr[...] = h_prev
    hfin_ref[...] = h_prev
    if final:
        g = gb_ref[...]
        gelu = 0.5 * g * (1.0 + jnp.tanh(0.7978845608028654 * (g + 0.044715 * g * g * g)))
        o_ref[...] = (gelu * (hf_ref[...] + hbuf[...])).astype(o_ref.dtype)


def lru_pass(p, cw, cb, wg, bg, lam, h0, reverse, h_fwd=None):
    bsz, l, _ = p.shape
    t = _tile(l, 256)
    nt = l // t
    tb = t // SUBLANES
    nb8 = l // SUBLANES
    final = h_fwd is not None
    xoff = COL_XB // D_LRU

    def rix(i):
        return nt - 1 - i if reverse else i

    vec = lambda r: pl.BlockSpec((r, D_LRU), lambda b, i: (0, 0))
    one = pl.BlockSpec((None, 1, D_LRU), lambda b, i: (b, 0, 0))
    orow = pl.BlockSpec((None, t, D_LRU), lambda b, i: (b, rix(i), 0))
    in_specs = [pl.BlockSpec((None, t, D_LRU), lambda b, i: (b, rix(i), xoff)),
                pl.BlockSpec((None, SUBLANES, D_LRU),
                             lambda b, i: (b, jnp.maximum(rix(i) * tb - 1, 0), xoff)),
                pl.BlockSpec((None, SUBLANES, D_LRU),
                             lambda b, i: (b, jnp.minimum((rix(i) + 1) * tb, nb8 - 1), xoff)),
                vec(LRU_CONV), vec(1),
                pl.BlockSpec((LRU_BLOCKS, LRU_BW, 2 * LRU_BW), lambda b, i: (0, 0, 0)),
                vec(2), vec(1), one]
    args = [p, p, p, cw, cb.reshape(1, D_LRU), wg, bg, lam.reshape(1, D_LRU), h0]
    if final:
        in_specs += [orow, pl.BlockSpec((None, t, D_LRU), lambda b, i: (b, rix(i), COL_GB // D_LRU))]
        args += [h_fwd, p]
    return pl.pallas_call(
        functools.partial(_lru_body, reverse=reverse, final=final, t=t, nt=nt),
        grid=(bsz, nt), in_specs=in_specs, out_specs=[orow, one],
        out_shape=[jax.ShapeDtypeStruct((bsz, l, D_LRU), BF16 if final else F32),
                   jax.ShapeDtypeStruct((bsz, 1, D_LRU), F32)],
        scratch_shapes=[pltpu.VMEM((1, D_LRU), F32)] + ([pltpu.VMEM((t, D_LRU), F32)] if final else []),
        compiler_params=_params(("arbitrary", "arbitrary")), name="lru",
    )(*args)


def _rms(x, width):
    return x * lax.rsqrt(jnp.sum(x * x, axis=-1, keepdims=True) * (1.0 / width) + NORM_EPS)


def _rope(r, cos, sin):
    lane = lax.broadcasted_iota(jnp.int32, r.shape, 1)
    first = (lane % 32) < 16
    rot = jnp.where(first, pltpu.roll(r, LANES - 16, axis=1), pltpu.roll(r, 16, axis=1))
    return r * cos + rot * sin


def _qproj_body(x_ref, gn_ref, w_ref, hg_ref, cos_ref, sin_ref, o_ref, xn_scr):
    @pl.when(pl.program_id(2) == 0)
    def _():
        xn_scr[...] = (_rms(x_ref[...], MLA_Q_RANK) * gn_ref[...]).astype(BF16)

    a = jnp.dot(xn_scr[...], w_ref[...], preferred_element_type=F32)
    qn = _rms(a[:, :MLA_NOPE], MLA_NOPE) * hg_ref[:, :MLA_NOPE]
    qr = _rms(a[:, MLA_NOPE:], MLA_ROPE) * hg_ref[:, MLA_NOPE:]
    qr = _rope(qr, cos_ref[...], sin_ref[...])
    o_ref[...] = (jnp.concatenate([qn, qr], axis=1) * (MLA_SCALE * LOG2E)).astype(o_ref.dtype)


def q_proj(p, gn, w, hg, cos, sin):
    bsz, l, _ = p.shape
    tm = _tile(l, 512)
    return pl.pallas_call(
        _qproj_body, grid=(bsz, l // tm, MLA_HEADS),
        in_specs=[pl.BlockSpec((None, tm, QDN_W), lambda b, i, h: (b, i, COL_QDN // QDN_W)),
                  pl.BlockSpec((1, QDN_W), lambda b, i, h: (0, 0)),
                  pl.BlockSpec((QDN_W, HEAD_W), lambda b, i, h: (0, h)),
                  pl.BlockSpec((1, HEAD_W), lambda b, i, h: (0, 0)),
                  pl.BlockSpec((tm, LANES), lambda b, i, h: (i, 0)),
                  pl.BlockSpec((tm, LANES), lambda b, i, h: (i, 0))],
        out_specs=pl.BlockSpec((None, None, tm, HEAD_W), lambda b, i, h: (b, h, i, 0)),
        out_shape=jax.ShapeDtypeStruct((bsz, MLA_HEADS, l, HEAD_W), BF16),
        scratch_shapes=[pltpu.VMEM((tm, QDN_W), BF16)],
        compiler_params=_params(("arbitrary", "arbitrary", "arbitrary")), name="q_proj",
    )(p, gn.reshape(1, QDN_W), w, hg, cos, sin)


def _kvproj_body(x_ref, kr_ref, gn_ref, w_ref, hg_ref, cos_ref, sin_ref, k_ref, v_ref, xn_scr, kr_scr):
    @pl.when(pl.program_id(2) == 0)
    def _():
        xn_scr[...] = (_rms(x_ref[...], MLA_KV_RANK) * gn_ref[...]).astype(BF16)
        kr = _rms(kr_ref[...], MLA_ROPE) * hg_ref[:, MLA_NOPE:]
        kr_scr[...] = _rope(kr, cos_ref[...], sin_ref[...]).astype(BF16)

    a = jnp.dot(xn_scr[...], w_ref[...], preferred_element_type=F32)
    kn = _rms(a[:, :MLA_NOPE], MLA_NOPE) * hg_ref[:, :MLA_NOPE]
    k_ref[:, :MLA_NOPE] = kn.astype(k_ref.dtype)
    k_ref[:, MLA_NOPE:] = kr_scr[...]
    v_ref[:MLA_V, :] = a[:, MLA_NOPE:].T.astype(v_ref.dtype)
    v_ref[MLA_V:, :] = jnp.ones((VT_ROWS - MLA_V, a.shape[0]), v_ref.dtype)


def kv_proj(p, gn, w, hg, cos, sin):
    bsz, l, _ = p.shape
    tm = _tile(l, 512)
    return pl.pallas_call(
        _kvproj_body, grid=(bsz, l // tm, MLA_HEADS),
        in_specs=[pl.BlockSpec((None, tm, MLA_KV_RANK), lambda b, i, h: (b, i, COL_KVDN // MLA_KV_RANK)),
                  pl.BlockSpec((None, tm, KR_W), lambda b, i, h: (b, i, COL_KR // KR_W)),
                  pl.BlockSpec((1, MLA_KV_RANK), lambda b, i, h: (0, 0)),
                  pl.BlockSpec((MLA_KV_RANK, HEAD_W), lambda b, i, h: (0, h)),
                  pl.BlockSpec((1, HEAD_W), lambda b, i, h: (0, 0)),
                  pl.BlockSpec((tm, LANES), lambda b, i, h: (i, 0)),
                  pl.BlockSpec((tm, LANES), lambda b, i, h: (i, 0))],
        out_specs=[pl.BlockSpec((None, None, tm, HEAD_W), lambda b, i, h: (b, h, i, 0)),
                   pl.BlockSpec((None, None, VT_ROWS, tm), lambda b, i, h: (b, h, 0, i))],
        out_shape=[jax.ShapeDtypeStruct((bsz, MLA_HEADS, l, HEAD_W), BF16),
                   jax.ShapeDtypeStruct((bsz, MLA_HEADS, VT_ROWS, l), BF16)],
        scratch_shapes=[pltpu.VMEM((tm, MLA_KV_RANK), BF16), pltpu.VMEM((tm, KR_W), BF16)],
        compiler_params=_params(("arbitrary", "arbitrary", "arbitrary")), name="kv_proj",
    )(p, p, gn.reshape(1, MLA_KV_RANK), w, hg, cos, sin)


def _scores_t(k, q):
    return lax.dot_general(k, q, (((1,), (1,)), ((), ())), preferred_element_type=F32)


def _attn_body(*refs, tk, n_lat):
    if n_lat:
        q_ref, kc_ref, vc_ref, k_ref, v_ref, o_ref, m_scr, acc_scr, s_scr = refs
    else:
        q_ref, kc_ref, vc_ref, o_ref = refs

    sc = _scores_t(kc_ref[...], q_ref[...])
    m0 = jnp.max(sc, axis=0, keepdims=True)
    acc = jnp.dot(vc_ref[...], jnp.exp2(sc - m0).astype(BF16), preferred_element_type=F32)

    if n_lat:
        m_scr[...] = m0
        acc_scr[...] = acc

        def qk(slot, off):
            s_scr[slot] = _scores_t(k_ref[pl.ds(off, tk), :], q_ref[...])

        def consume(slot, off):
            s = s_scr[slot]
            m = m_scr[...]
            m_new = jnp.maximum(m, jnp.max(s, axis=0, keepdims=True))
            p = jnp.exp2(s - m_new).astype(BF16)
            m_scr[...] = m_new
            acc_scr[...] = (jnp.exp2(m - m_new) * acc_scr[...]
                            + jnp.dot(v_ref[:, pl.ds(off, tk)], p, preferred_element_type=F32))

        qk(0, 0)

        def pair(jj, carry):
            off = pl.multiple_of(jj * (2 * tk), 2 * tk)
            qk(1, off + tk)
            consume(0, off)
            qk(0, off + 2 * tk)
            consume(1, off + tk)
            return carry

        lax.fori_loop(0, n_lat // 2 - 1, pair, 0)
        off = (n_lat - 2) * tk
        qk(1, off + tk)
        consume(0, off)
        consume(1, off + tk)
        acc = acc_scr[...]
    o_ref[...] = (acc[:MLA_V] / acc[MLA_V:MLA_V + 1]).T.astype(o_ref.dtype)


def attention(q, kc, vc, k=None, v=None):
    bsz, nh, lq, _ = q.shape
    lc = kc.shape[2]
    tq = _tile(lq, 1024)
    keys = lambda n: pl.BlockSpec((None, None, n, HEAD_W), lambda b, h, i: (b, h, 0, 0))
    vals = lambda n: pl.BlockSpec((None, None, VT_ROWS, n), lambda b, h, i: (b, h, 0, 0))
    in_specs = [pl.BlockSpec((None, None, tq, HEAD_W), lambda b, h, i: (b, h, i, 0)), keys(lc), vals(lc)]
    args = [q, kc, vc]
    tk = n_lat = 0
    scratch = []
    if k is not None:
        lk = k.shape[2]
        tk = _tile(lk // 2, 1024)
        assert tk % LANES == 0 and lk % (2 * tk) == 0
        n_lat = lk // tk
        in_specs += [keys(lk), vals(lk)]
        args += [k, v]
        scratch = [pltpu.VMEM((1, tq), F32), pltpu.VMEM((VT_ROWS, tq), F32), pltpu.VMEM((2, tk, tq), F32)]
    return pl.pallas_call(
        functools.partial(_attn_body, tk=tk, n_lat=n_lat),
        grid=(bsz, nh, lq // tq), in_specs=in_specs,
        out_specs=pl.BlockSpec((None, tq, MLA_V), lambda b, h, i: (b, i, h)),
        out_shape=jax.ShapeDtypeStruct((bsz, lq, nh * MLA_V), BF16),
        scratch_shapes=scratch,
        compiler_params=_params(("arbitrary", "arbitrary", "arbitrary")), name="attention",
    )(*args)


def _moe_body(be_ref, nu_ref, x_ref, w13_ref, w2_ref, g_ref, o_ref):
    j = pl.program_id(0)

    @pl.when(j < nu_ref[0])
    def _():
        hid = jnp.dot(x_ref[...], w13_ref[...].astype(BF16), preferred_element_type=F32)
        act = (_silu(hid[:, :D_EXPERT]) * hid[:, D_EXPERT:]).astype(BF16)
        out = jnp.dot(act, w2_ref[...].astype(BF16), preferred_element_type=F32)
        o_ref[...] = out * g_ref[...]

    @pl.when(j >= nu_ref[0])
    def _():
        o_ref[...] = jnp.zeros_like(o_ref)


def moe_experts(xs, gate, blk_exp, n_used, w13, w2, layer):
    p, d = xs.shape
    nb = p // MOE_BLOCK
    grid_spec = pltpu.PrefetchScalarGridSpec(
        num_scalar_prefetch=2, grid=(nb,),
        in_specs=[pl.BlockSpec((MOE_BLOCK, d), lambda j, be, nu: (j, 0)),
                  pl.BlockSpec((None, None, d, 2 * D_EXPERT), lambda j, be, nu: (layer, be[j], 0, 0)),
                  pl.BlockSpec((None, None, D_EXPERT, d), lambda j, be, nu: (layer, be[j], 0, 0)),
                  pl.BlockSpec((MOE_BLOCK, 1), lambda j, be, nu: (j, 0))],
        out_specs=pl.BlockSpec((MOE_BLOCK, d), lambda j, be, nu: (j, 0)))
    return pl.pallas_call(
        _moe_body, grid_spec=grid_spec,
        out_shape=jax.ShapeDtypeStruct((p, d), F32),
        compiler_params=_params(("arbitrary",)), name="moe_experts",
    )(blk_exp, n_used, xs, w13, w2, gate)


def moe_ffn(h, logits, w13, w2, layer):
    m, d = h.shape
    g_logit = logits[:, :N_GROUPS]
    e_logit = logits[:, N_GROUPS:N_GROUPS + N_EXPERTS].reshape(m, N_GROUPS, EXPERTS_PER_GROUP)
    g_idx = jnp.argmax(g_logit, axis=-1)
    p_grp = jnp.take_along_axis(jax.nn.softmax(g_logit, axis=-1), g_idx[:, None], axis=1)
    e_sel = jnp.take_along_axis(e_logit, g_idx[:, None, None], axis=1)[:, 0]
    top_v, top_i = lax.top_k(e_sel, TOP_K)
    gate = (p_grp * jax.nn.softmax(top_v, axis=-1)).reshape(-1)
    e_flat = (g_idx[:, None] * EXPERTS_PER_GROUP + top_i).reshape(-1).astype(jnp.int32)

    a = m * TOP_K
    nb = -(-a // MOE_BLOCK) + N_EXPERTS
    order = jnp.argsort(e_flat).astype(jnp.int32)
    rank = jnp.argsort(order).astype(jnp.int32)
    e_s = e_flat[order]
    experts = jnp.arange(N_EXPERTS, dtype=jnp.int32)
    start = jnp.searchsorted(e_s, experts, side='left').astype(jnp.int32)
    counts = jnp.searchsorted(e_s, experts, side='right').astype(jnp.int32) - start
    padded = -(-counts // MOE_BLOCK) * MOE_BLOCK
    pad_end = jnp.cumsum(padded)
    pad_start = pad_end - padded
    blk_raw = jnp.searchsorted(pad_end, jnp.arange(nb, dtype=jnp.int32) * MOE_BLOCK, side='right')
    blk_exp = jnp.minimum(blk_raw, N_EXPERTS - 1).astype(jnp.int32)
    n_used = (pad_end[-1] // MOE_BLOCK).astype(jnp.int32).reshape(1)
    e_row = jnp.repeat(blk_exp, MOE_BLOCK)
    r = jnp.arange(nb * MOE_BLOCK, dtype=jnp.int32) - pad_start[e_row]
    valid = (r < counts[e_row]) & (jnp.repeat(blk_raw, MOE_BLOCK) < N_EXPERTS)
    src = order[jnp.where(valid, start[e_row] + r, 0)]
    row_tok = jnp.where(valid, src // TOP_K, m)
    row_gate = jnp.where(valid, gate[src], 0.0)
    dest = (pad_start[e_flat] + rank - start[e_flat]).reshape(m, TOP_K)

    hp = jnp.concatenate([h, jnp.zeros((1, d), h.dtype)], axis=0)
    out = moe_experts(hp[row_tok], row_gate[:, None], blk_exp, n_used, w13, w2, layer)
    return out[dest[:, 0]] + out[dest[:, 1]]


def _prep_w_in(w):
    sizes = (512, 512, 1024, 1024, 32, 768, 512, 64, 1024, 1024)
    offs = [0]
    for s in sizes:
        offs.append(offs[-1] + s)
    q, k, v, g, al, qdn, kvdn, kr, xb, gb = [w[:, offs[i]:offs[i + 1]] for i in range(10)]
    zpad = lambda t, n: jnp.pad(t, ((0, 0), (0, n - t.shape[1])))
    return jnp.concatenate([qdn, zpad(al, AL_W), zpad(kr, KR_W), v, g, xb, gb, q, k, kvdn],
                           axis=1).astype(BF16)


def _prep_w_uq(w):
    w = w.reshape(MLA_Q_RANK, MLA_HEADS, MLA_NOPE + MLA_ROPE)
    w = jnp.pad(w, ((0, 0), (0, 0), (0, HEAD_W - MLA_NOPE - MLA_ROPE)))
    return w.reshape(MLA_Q_RANK, MLA_HEADS * HEAD_W).astype(BF16)


def _head_gain(g):
    return jnp.pad(g, (0, HEAD_W - g.shape[0])).reshape(1, HEAD_W)


def _rope_tables(rows):
    n = MLA_ROPE // 4
    inv = ROPE_BASE ** (-jnp.arange(n, dtype=F32) / n)
    r = jnp.repeat(jnp.arange(rows, dtype=F32), GRID_W)
    c = jnp.tile(jnp.arange(GRID_W, dtype=F32), rows)
    ar, ac = r[:, None] * inv, c[:, None] * inv
    l = ar.shape[0]
    cos = jnp.concatenate([jnp.cos(ar), jnp.cos(ar), jnp.cos(ac), jnp.cos(ac),
                           jnp.ones((l, LANES - MLA_ROPE), F32)], axis=1)
    sin = jnp.concatenate([-jnp.sin(ar), jnp.sin(ar), -jnp.sin(ac), jnp.sin(ac),
                           jnp.zeros((l, LANES - MLA_ROPE), F32)], axis=1)
    return cos, sin


def _gla_gate_w(w_a2, b_a, d):
    hq = GLA_HEADS * GLA_DK
    w = jnp.zeros((AL_W, hq), F32).at[d * GLA_GATE_RANK:(d + 1) * GLA_GATE_RANK].set(w_a2[d])
    return w, b_a[d].reshape(1, hq)


def _lru_gate_w(w_gate, d):
    return jnp.concatenate([w_gate[d, 0], w_gate[d, 1]], axis=-1)


def _layer(x, xc, mods, rope, prm, moe_w13, moe_w2, layer, last):
    (norm1_g, norm2_g, w_in, gla_w_a2, gla_b_a, gla_norm_g,
     mla_q_norm_g, mla_kv_norm_g, mla_w_uq, mla_w_ukv, mla_q_head_g, mla_k_head_g,
     lru_conv_w, lru_conv_b, lru_w_gate, lru_b_gate, lru_lambda, w_out,
     moe_w_grp, moe_b_grp, moe_w_exp, moe_b_exp) = prm
    bsz, l, d = x.shape
    lc = xc.shape[1]
    ctx_out = not last
    lat = lambda k: mods[:bsz, k].reshape(bsz, 1, d)
    ctx = lambda k: jnp.broadcast_to(mods[bsz, k].reshape(1, 1, d), (bsz, 1, d))

    w_in_p = _prep_w_in(w_in)
    h = norm_mod(x, norm1_g, lat(0), lat(1))
    hc = norm_mod(xc, norm1_g, ctx(0), ctx(1))
    p = matmul(h.reshape(bsz * l, d), w_in_p).reshape(bsz, l, D_INP)
    pc = matmul(hc.reshape(bsz * lc, d), w_in_p).reshape(bsz, lc, D_INP)

    waf, baf = _gla_gate_w(gla_w_a2, gla_b_a, 0)
    wab, bab = _gla_gate_w(gla_w_a2, gla_b_a, 1)
    s_zero = jnp.zeros((bsz, GLA_HEADS, GLA_DK, GLA_DV), F32)
    ocf, s_f = gla_pass(pc, waf, baf, s_zero, False)
    of, _ = gla_pass(p, waf, baf, s_f, False)
    if ctx_out:
        yac, s_b = gla_pass(pc, wab, bab, s_zero, True, ocf, gla_norm_g)
    else:
        _, s_b = gla_pass(pc, wab, bab, s_zero, True)
    ya, _ = gla_pass(p, wab, bab, s_b, True, of, gla_norm_g)

    wgf, wgb = _lru_gate_w(lru_w_gate, 0), _lru_gate_w(lru_w_gate, 1)
    h_zero = jnp.zeros((bsz, 1, D_LRU), F32)
    lru = functools.partial(lru_pass, cw=lru_conv_w, cb=lru_conv_b)
    hcf, h0f = lru(pc, wg=wgf, bg=lru_b_gate[0], lam=lru_lambda[0], h0=h_zero, reverse=False)
    hf, _ = lru(p, wg=wgf, bg=lru_b_gate[0], lam=lru_lambda[0], h0=h0f, reverse=False)
    if ctx_out:
        ylc, h0b = lru(pc, wg=wgb, bg=lru_b_gate[1], lam=lru_lambda[1], h0=h_zero, reverse=True, h_fwd=hcf)
    else:
        _, h0b = lru(pc, wg=wgb, bg=lru_b_gate[1], lam=lru_lambda[1], h0=h_zero, reverse=True)
    yl, _ = lru(p, wg=wgb, bg=lru_b_gate[1], lam=lru_lambda[1], h0=h0b, reverse=True, h_fwd=hf)

    cos, sin = rope
    cos_c = jnp.ones((lc, LANES), F32)
    sin_c = jnp.zeros((lc, LANES), F32)
    w_uq = _prep_w_uq(mla_w_uq)
    w_ukv = mla_w_ukv.astype(BF16)
    qg, kg = _head_gain(mla_q_head_g), _head_gain(mla_k_head_g)
    kc, vc = kv_proj(pc, mla_kv_norm_g, w_ukv, kg, cos_c, sin_c)
    kl, vl = kv_proj(p, mla_kv_norm_g, w_ukv, kg, cos, sin)
    ql = q_proj(p, mla_q_norm_g, w_uq, qg, cos, sin)
    yb = attention(ql, kc, vc, kl, vl)

    w_out_b = w_out.astype(BF16)
    x = out_proj(ya, yb, yl, w_out_b, x, lat(2))
    if ctx_out:
        qc = q_proj(pc, mla_q_norm_g, w_uq, qg, cos_c, sin_c)
        ybc = attention(qc, kc, vc)
        xc = out_proj(yac, ybc, ylc, w_out_b, xc, ctx(2))

    nr = LANES
    wr = jnp.pad(jnp.concatenate([moe_w_grp, moe_w_exp], axis=1), ((0, 0), (0, nr - N_GROUPS - N_EXPERTS)))
    br = jnp.pad(jnp.concatenate([moe_b_grp, moe_b_exp]), (0, nr - N_GROUPS - N_EXPERTS)).reshape(1, nr)
    h2, lg = norm_mod(x, norm2_g, lat(3), lat(4), router=(wr, br))
    if ctx_out:
        h2c, lgc = norm_mod(xc, norm2_g, ctx(3), ctx(4), router=(wr, br))
        out = moe_ffn(jnp.concatenate([h2c.reshape(-1, d), h2.reshape(-1, d)], axis=0),
                      jnp.concatenate([lgc.reshape(-1, nr), lg.reshape(-1, nr)], axis=0),
                      moe_w13, moe_w2, layer)
        xc = xc + ctx(5) * out[:bsz * lc].reshape(bsz, lc, d)
        x = x + lat(5) * out[bsz * lc:].reshape(bsz, l, d)
    else:
        x = x + lat(5) * moe_ffn(h2.reshape(-1, d), lg.reshape(-1, nr),
                                 moe_w13, moe_w2, layer).reshape(bsz, l, d)
    return x, xc


def kernel(x, c, ctx, c_ctx, w_mod, b_mod, norm1_g, norm2_g, w_in, gla_w_a2, gla_b_a, gla_norm_g,
           mla_q_norm_g, mla_kv_norm_g, mla_w_uq, mla_w_ukv, mla_q_head_g, mla_k_head_g,
           lru_conv_w, lru_conv_b, lru_w_gate, lru_b_gate, lru_lambda, w_out,
           moe_w_grp, moe_b_grp, moe_w_exp, moe_b_exp, moe_w13, moe_w2):
    bsz, l, d = x.shape
    depth = w_mod.shape[0]
    assert bsz + 1 <= SUBLANES and l % GRID_W == 0
    cond = jnp.zeros((SUBLANES, d), F32).at[:bsz].set(c).at[bsz].set(c_ctx)
    mods = adaln_all(cond, w_mod, b_mod).reshape(depth, SUBLANES, 6, d)
    rope = _rope_tables(l // GRID_W)
    per_layer = (norm1_g, norm2_g, w_in, gla_w_a2, gla_b_a, gla_norm_g,
                 mla_q_norm_g, mla_kv_norm_g, mla_w_uq, mla_w_ukv, mla_q_head_g, mla_k_head_g,
                 lru_conv_w, lru_conv_b, lru_w_gate, lru_b_gate, lru_lambda, w_out,
                 moe_w_grp, moe_b_grp, moe_w_exp, moe_b_exp)
    xc = ctx
    for i in range(depth):
        x, xc = _layer(x, xc, mods[i], rope, tuple(t[i] for t in per_layer), moe_w13, moe_w2, i,
                       last=(i == depth - 1))
    return x
```

```python
import functools

import jax
import jax.numpy as jnp
from jax import lax
from jax.experimental import pallas as pl
from jax.experimental.pallas import tpu as pltpu

F32 = jnp.float32
BF16 = jnp.bfloat16
HI = lax.Precision.HIGHEST

D_MODEL = 4096
GRID_W = 64
NORM_EPS = 1e-6
D_GLA = D_MODEL // 4
D_MLA = D_MODEL // 2
D_LRU = D_MODEL // 4
GLA_HEADS = 4
GLA_DK = 128
GLA_DV = 256
GLA_GATE_RANK = 16
GLA_TAU = 16.0
GLA_CHUNK = 64
GLA_QSCALE = GLA_DK ** -0.5
MLA_HEADS = 16
MLA_NOPE = 128
MLA_ROPE = 64
MLA_V = 128
MLA_Q_RANK = 768
MLA_KV_RANK = 512
MLA_SCALE = (MLA_NOPE + MLA_ROPE) ** -0.5
LOG2E = 1.4426950408889634
ROPE_BASE = 10000.0
LRU_BLOCKS = 8
LRU_BW = 128
LRU_CONV = 4
LRU_C = 8.0
N_GROUPS = 8
EXPERTS_PER_GROUP = 8
N_EXPERTS = 64
TOP_K = 2
D_EXPERT = 256

LANES = 128
SUBLANES = 8
VMEM_BYTES = 64 << 20
VMEM_LIMIT = VMEM_BYTES - (8 << 20)
MOE_BLOCK = 256

QDN_W, AL_W, KR_W = MLA_Q_RANK, LANES, LANES
COL_QDN = 0
COL_AL = 768
COL_KR = 896
COL_V = 1024
COL_G = 2048
COL_XB = 3072
COL_GB = 4096
COL_Q = 5120
COL_K = 5632
COL_KVDN = 6144
D_INP = 6656
HEAD_W = 256
VT_ROWS = MLA_V + 16
HEADS_PER_STEP = 4


def _params(sem, vmem=VMEM_LIMIT):
    return pltpu.CompilerParams(dimension_semantics=sem, vmem_limit_bytes=vmem)


def _tile(n, pref):
    t = min(n, pref)
    while n % t or t % SUBLANES:
        t -= 1
    return t


def _silu(x):
    return x * jax.nn.sigmoid(x)


def _softplus(x):
    return jnp.maximum(x, 0.0) + jnp.log1p(jnp.exp(-jnp.abs(x)))


def _adaln_body(c_ref, w_ref, b_ref, o_ref):
    s = _silu(c_ref[...])
    o_ref[...] = jnp.dot(s, w_ref[...], preferred_element_type=F32, precision=HI) + b_ref[...]


def adaln_all(cond, w_mod, b_mod):
    depth, d, n = w_mod.shape
    tn = 512
    return pl.pallas_call(
        _adaln_body,
        grid=(depth, n // tn),
        in_specs=[pl.BlockSpec((SUBLANES, d), lambda l, j: (0, 0)),
                  pl.BlockSpec((None, d, tn), lambda l, j: (l, 0, j)),
                  pl.BlockSpec((None, 1, tn), lambda l, j: (l, 0, j))],
        out_specs=pl.BlockSpec((None, SUBLANES, tn), lambda l, j: (l, 0, j)),
        out_shape=jax.ShapeDtypeStruct((depth, SUBLANES, n), F32),
        compiler_params=_params(("arbitrary", "arbitrary")),
        name="adaln",
    )(cond, w_mod, b_mod.reshape(depth, 1, n))


def _norm_mod(x, g, shift, scale):
    y = x * lax.rsqrt(jnp.mean(x * x, axis=-1, keepdims=True) + NORM_EPS) * g
    return y * (1.0 + scale) + shift


def _norm_mod_body(x_ref, g_ref, sh_ref, sc_ref, o_ref):
    o_ref[...] = _norm_mod(x_ref[...], g_ref[...], sh_ref[...], sc_ref[...]).astype(o_ref.dtype)


def _norm_router_body(*refs, n_ctx):
    if n_ctx:
        (xc_ref, shc_ref, scc_ref, x_ref, sh_ref, sc_ref, g_ref, wr_ref, br_ref, o_ref, lg_ref) = refs
        is_ctx = pl.program_id(0) < n_ctx
        x = jnp.where(is_ctx, xc_ref[...], x_ref[...])
        sh = jnp.where(is_ctx, shc_ref[...], sh_ref[...])
        sc = jnp.where(is_ctx, scc_ref[...], sc_ref[...])
    else:
        x_ref, sh_ref, sc_ref, g_ref, wr_ref, br_ref, o_ref, lg_ref = refs
        x, sh, sc = x_ref[...], sh_ref[...], sc_ref[...]
    h = _norm_mod(x, g_ref[...], sh, sc)
    o_ref[...] = h
    lg_ref[...] = jnp.dot(h, wr_ref[...], preferred_element_type=F32, precision=HI) + br_ref[...]


def norm_mod(x, g, shift, scale):
    bx, lx, d = x.shape
    tm = _tile(lx, 256)
    row = pl.BlockSpec((None, tm, d), lambda b, i: (b, i, 0))
    vec = pl.BlockSpec((1, d), lambda b, i: (0, 0))
    mod = pl.BlockSpec((None, 1, d), lambda b, i: (b, 0, 0))
    return pl.pallas_call(
        _norm_mod_body, grid=(bx, lx // tm),
        in_specs=[row, vec, mod, mod], out_specs=row,
        out_shape=jax.ShapeDtypeStruct((bx, lx, d), BF16),
        compiler_params=_params(("arbitrary", "arbitrary")), name="norm_mod",
    )(x, g.reshape(1, d), shift, scale)


def norm_router(x, g, shift, scale, wr, br, ctx=None):
    bsz, l, d = x.shape
    nr = wr.shape[1]
    lc = ctx[0].shape[1] if ctx is not None else 0
    tm = _tile(l, 256) if ctx is None else _tile(lc, 256)
    assert l % tm == 0
    n_ctx = bsz * lc // tm
    n_lat = bsz * l // tm
    per_b = l // tm
    lat = lambda f: jnp.maximum(f - n_ctx, 0)
    vec = lambda w: pl.BlockSpec((1, w), lambda f: (0, 0))
    in_specs, args = [], []
    if ctx is not None:
        xc, shc, scc = ctx
        one = pl.BlockSpec((None, 1, d), lambda f: (0, 0, 0))
        in_specs += [pl.BlockSpec((tm, d), lambda f: (jnp.minimum(f, n_ctx - 1), 0)), one, one]
        args += [xc.reshape(bsz * lc, d), shc, scc]
    mod = pl.BlockSpec((None, 1, d), lambda f: (lat(f) // per_b, 0, 0))
    in_specs += [pl.BlockSpec((tm, d), lambda f: (lat(f), 0)), mod, mod, vec(d),
                 pl.BlockSpec((d, nr), lambda f: (0, 0)), vec(nr)]
    args += [x.reshape(bsz * l, d), shift, scale, g.reshape(1, d), wr, br]
    total = bsz * (lc + l)
    return pl.pallas_call(
        functools.partial(_norm_router_body, n_ctx=n_ctx), grid=(n_ctx + n_lat,), in_specs=in_specs,
        out_specs=[pl.BlockSpec((tm, d), lambda f: (f, 0)), pl.BlockSpec((tm, nr), lambda f: (f, 0))],
        out_shape=[jax.ShapeDtypeStruct((total, d), F32), jax.ShapeDtypeStruct((total, nr), F32)],
        compiler_params=_params(("arbitrary",)), name="norm_router",
    )(*args)


def _mm_body(a_ref, w_ref, o_ref):
    o_ref[...] = jnp.dot(a_ref[...], w_ref[...], preferred_element_type=F32).astype(o_ref.dtype)


def matmul(a, w, out_dtype=F32, tm_pref=1024, tn=512):
    m, k = a.shape
    n = w.shape[1]
    tm = _tile(m, tm_pref)
    return pl.pallas_call(
        _mm_body, grid=(m // tm, n // tn),
        in_specs=[pl.BlockSpec((tm, k), lambda i, j: (i, 0)),
                  pl.BlockSpec((k, tn), lambda i, j: (0, j))],
        out_specs=pl.BlockSpec((tm, tn), lambda i, j: (i, j)),
        out_shape=jax.ShapeDtypeStruct((m, n), out_dtype),
        compiler_params=_params(("arbitrary", "arbitrary")), name="matmul",
    )(a, w)


def _wout_body(ya_ref, yb_ref, yl_ref, w_ref, x_ref, g_ref, o_ref):
    acc = jnp.dot(ya_ref[...], w_ref[0:D_GLA, :], preferred_element_type=F32)
    acc += jnp.dot(yb_ref[...], w_ref[D_GLA:D_GLA + D_MLA, :], preferred_element_type=F32)
    acc += jnp.dot(yl_ref[...], w_ref[D_GLA + D_MLA:, :], preferred_element_type=F32)
    o_ref[...] = x_ref[...] + g_ref[...] * acc


def out_proj(ya, yb, yl, w, x, gate):
    bx, lx, d = x.shape
    tm = _tile(lx, 1024)
    tn = 512
    return pl.pallas_call(
        _wout_body, grid=(bx, lx // tm, d // tn),
        in_specs=[pl.BlockSpec((None, tm, D_GLA), lambda b, i, j: (b, i, 0)),
                  pl.BlockSpec((None, tm, D_MLA), lambda b, i, j: (b, i, 0)),
                  pl.BlockSpec((None, tm, D_LRU), lambda b, i, j: (b, i, 0)),
                  pl.BlockSpec((d, tn), lambda b, i, j: (0, j)),
                  pl.BlockSpec((None, tm, tn), lambda b, i, j: (b, i, j)),
                  pl.BlockSpec((None, 1, tn), lambda b, i, j: (b, 0, j))],
        out_specs=pl.BlockSpec((None, tm, tn), lambda b, i, j: (b, i, j)),
        out_shape=jax.ShapeDtypeStruct((bx, lx, d), F32),
        compiler_params=_params(("arbitrary", "arbitrary", "arbitrary")), name="out_proj",
    )(ya, yb, yl, w, x, gate)


def _gla_body(*refs, reverse, final, nchunk):
    if final:
        (q_ref, k_ref, v_ref, al_ref, wa_ref, ba_ref, s0_ref, of_ref, g_ref, ng_ref,
         o_ref, sfin_ref, s_scr) = refs
    else:
        q_ref, k_ref, v_ref, al_ref, wa_ref, ba_ref, s0_ref, o_ref, sfin_ref, s_scr = refs
    C = GLA_CHUNK

    @pl.when(pl.program_id(1) == 0)
    def _():
        s_scr[...] = s0_ref[...]

    r_i = lax.broadcasted_iota(jnp.int32, (C, C), 0)
    c_i = lax.broadcasted_iota(jnp.int32, (C, C), 1)
    keep = (c_i >= r_i) if reverse else (c_i <= r_i)
    tri = keep.astype(F32)
    ref_row = C - 1 - C // 2 if reverse else C // 2
    last_row = 0 if reverse else C - 1

    order = range(nchunk - 1, -1, -1) if reverse else range(nchunk)
    for ci in order:
        rows = slice(ci * C, (ci + 1) * C)
        z = jnp.dot(al_ref[rows, :], wa_ref[...], preferred_element_type=F32, precision=HI) + ba_ref[...]
        logd = (jnp.minimum(z, 0.0) - jnp.log1p(jnp.exp(-jnp.abs(z)))) * (1.0 / GLA_TAU)
        b = jnp.dot(tri, logd, preferred_element_type=F32, precision=HI)
        b_ref = b[ref_row:ref_row + 1, :]
        b_last = b[last_row:last_row + 1, :]
        q = q_ref[rows, :] * GLA_QSCALE
        k = k_ref[rows, :]
        qe = (q * jnp.exp(b - b_ref)).astype(BF16)
        ke = (k * jnp.exp(b_ref - b)).astype(BF16)
        qin = (q * jnp.exp(b)).astype(BF16)
        kst = k * jnp.exp(b_last - b)
        for h in range(GLA_HEADS):
            hs = slice(h * GLA_DK, (h + 1) * GLA_DK)
            vs = slice(h * GLA_DV, (h + 1) * GLA_DV)
            v = v_ref[rows, vs].astype(BF16)
            att = lax.dot_general(qe[:, hs], ke[:, hs], (((1,), (1,)), ((), ())),
                                  preferred_element_type=F32)
            att = jnp.where(keep, att, 0.0).astype(BF16)
            s = s_scr[h]
            o = (jnp.dot(att, v, preferred_element_type=F32)
                 + jnp.dot(qin[:, hs], s.astype(BF16), preferred_element_type=F32))
            d_col = jnp.exp(jnp.broadcast_to(b_last[:, hs], (GLA_DK, GLA_DK))).T
            s_scr[h] = (jnp.concatenate([d_col, d_col], axis=1) * s
                        + jnp.dot(kst[:, hs].T.astype(BF16), v, preferred_element_type=F32))
            if final:
                o = o + of_ref[rows, vs]
                y = o * lax.rsqrt(jnp.mean(o * o, axis=-1, keepdims=True) + NORM_EPS) * ng_ref[...]
                o_ref[rows, vs] = (y * _silu(g_ref[rows, vs])).astype(o_ref.dtype)
            else:
                o_ref[rows, vs] = o
    sfin_ref[...] = s_scr[...]


def gla_pass(p, wa, ba, s0, reverse, o_fwd=None, norm_g=None):
    bsz, l, _ = p.shape
    t = _tile(l, 256)
    nt = l // t
    final = o_fwd is not None

    def rix(i):
        return nt - 1 - i if reverse else i

    def col(width, off):
        return pl.BlockSpec((None, t, width), lambda b, i: (b, rix(i), off // width))

    hq = GLA_HEADS * GLA_DK
    state = pl.BlockSpec((None, GLA_HEADS, GLA_DK, GLA_DV), lambda b, i: (b, 0, 0, 0))
    in_specs = [col(hq, COL_Q), col(hq, COL_K), col(D_GLA, COL_V), col(AL_W, COL_AL),
                pl.BlockSpec((AL_W, hq), lambda b, i: (0, 0)),
                pl.BlockSpec((1, hq), lambda b, i: (0, 0)), state]
    args = [p, p, p, p, wa, ba, s0]
    orow = pl.BlockSpec((None, t, D_GLA), lambda b, i: (b, rix(i), 0))
    if final:
        in_specs += [orow, col(D_GLA, COL_G), pl.BlockSpec((1, GLA_DV), lambda b, i: (0, 0))]
        args += [o_fwd, p, norm_g.reshape(1, GLA_DV)]
    return pl.pallas_call(
        functools.partial(_gla_body, reverse=reverse, final=final, nchunk=t // GLA_CHUNK),
        grid=(bsz, nt), in_specs=in_specs, out_specs=[orow, state],
        out_shape=[jax.ShapeDtypeStruct((bsz, l, D_GLA), BF16 if final else F32),
                   jax.ShapeDtypeStruct((bsz, GLA_HEADS, GLA_DK, GLA_DV), F32)],
        scratch_shapes=[pltpu.VMEM((GLA_HEADS, GLA_DK, GLA_DV), F32)],
        compiler_params=_params(("arbitrary", "arbitrary")), name="gla",
    )(*args)


def _lru_body(*refs, reverse, final, t, nt):
    if final:
        (x_ref, xp_ref, xn_ref, cw_ref, cb_ref, wg_ref, bg_ref, lam_ref, h0_ref, hf_ref, gb_ref,
         o_ref, hfin_ref, h_scr, hbuf) = refs
    else:
        (x_ref, xp_ref, xn_ref, cw_ref, cb_ref, wg_ref, bg_ref, lam_ref, h0_ref,
         o_ref, hfin_ref, h_scr) = refs
        hbuf = o_ref
    i = pl.program_id(1)
    ti = nt - 1 - i if reverse else i

    @pl.when(i == 0)
    def _():
        h_scr[...] = h0_ref[...]

    x = x_ref[...]
    xp = jnp.where(ti == 0, 0.0, xp_ref[...])
    xn = jnp.where(ti == nt - 1, 0.0, xn_ref[...])
    ext = jnp.concatenate([xp, x, xn], axis=0)
    xc = cb_ref[...]
    for j in range(LRU_CONV):
        off = SUBLANES - 2 + j
        xc = xc + ext[off:off + t, :] * cw_ref[j:j + 1, :]

    zs = []
    for n in range(LRU_BLOCKS):
        zs.append(jnp.dot(xc[:, n * LRU_BW:(n + 1) * LRU_BW], wg_ref[n],
                          preferred_element_type=F32, precision=HI))
    z_r = jnp.concatenate([z[:, :LRU_BW] for z in zs], axis=1) + bg_ref[0:1, :]
    z_i = jnp.concatenate([z[:, LRU_BW:] for z in zs], axis=1) + bg_ref[1:2, :]
    log_a = -LRU_C * jax.nn.sigmoid(z_r) * _softplus(-lam_ref[...])
    a = jnp.exp(log_a)
    th = jnp.tanh(log_a)
    one_minus_a2 = -2.0 * th / (1.0 - th)
    u = jnp.sqrt(one_minus_a2) * jax.nn.sigmoid(z_i) * xc

    sub = lax.broadcasted_iota(jnp.int32, (t, 1), 0) % SUBLANES
    s = 1
    while s < SUBLANES:
        if reverse:
            ok = sub < SUBLANES - s
            a_s = pltpu.roll(a, t - s, axis=0)
            u_s = pltpu.roll(u, t - s, axis=0)
        else:
            ok = sub >= s
            a_s = pltpu.roll(a, s, axis=0)
            u_s = pltpu.roll(u, s, axis=0)
        u = jnp.where(ok, a * u_s + u, u)
        a = jnp.where(ok, a * a_s, a)
        s *= 2
    ngroup = t // SUBLANES
    edge = 0 if reverse else SUBLANES - 1
    h_prev = h_scr[...]
    for gi in (range(ngroup - 1, -1, -1) if reverse else range(ngroup)):
        rows = slice(gi * SUBLANES, (gi + 1) * SUBLANES)
        hg = a[rows, :] * h_prev + u[rows, :]
        h_prev = hg[edge:edge + 1, :]
        hbuf[rows, :] = hg
    h_scr[...] = h_prev
    hfin_ref[...] = h_prev
    if final:
        g = gb_ref[...]
        gelu = 0.5 * g * (1.0 + jnp.tanh(0.7978845608028654 * (g + 0.044715 * g * g * g)))
        o_ref[...] = (gelu * (hf_ref[...] + hbuf[...])).astype(o_ref.dtype)


def lru_pass(p, cw, cb, wg, bg, lam, h0, reverse, h_fwd=None):
    bsz, l, _ = p.shape
    t = _tile(l, 256)
    nt = l // t
    tb = t // SUBLANES
    nb8 = l // SUBLANES
    final = h_fwd is not None
    xoff = COL_XB // D_LRU

    def rix(i):
        return nt - 1 - i if reverse else i

    vec = lambda r: pl.BlockSpec((r, D_LRU), lambda b, i: (0, 0))
    one = pl.BlockSpec((None, 1, D_LRU), lambda b, i: (b, 0, 0))
    orow = pl.BlockSpec((None, t, D_LRU), lambda b, i: (b, rix(i), 0))
    in_specs = [pl.BlockSpec((None, t, D_LRU), lambda b, i: (b, rix(i), xoff)),
                pl.BlockSpec((None, SUBLANES, D_LRU),
                             lambda b, i: (b, jnp.maximum(rix(i) * tb - 1, 0), xoff)),
                pl.BlockSpec((None, SUBLANES, D_LRU),
                             lambda b, i: (b, jnp.minimum((rix(i) + 1) * tb, nb8 - 1), xoff)),
                vec(LRU_CONV), vec(1),
                pl.BlockSpec((LRU_BLOCKS, LRU_BW, 2 * LRU_BW), lambda b, i: (0, 0, 0)),
                vec(2), vec(1), one]
    args = [p, p, p, cw, cb.reshape(1, D_LRU), wg, bg, lam.reshape(1, D_LRU), h0]
    if final:
        in_specs += [orow, pl.BlockSpec((None, t, D_LRU), lambda b, i: (b, rix(i), COL_GB // D_LRU))]
        args += [h_fwd, p]
    return pl.pallas_call(
        functools.partial(_lru_body, reverse=reverse, final=final, t=t, nt=nt),
        grid=(bsz, nt), in_specs=in_specs, out_specs=[orow, one],
        out_shape=[jax.ShapeDtypeStruct((bsz, l, D_LRU), BF16 if final else F32),
                   jax.ShapeDtypeStruct((bsz, 1, D_LRU), F32)],
        scratch_shapes=[pltpu.VMEM((1, D_LRU), F32)] + ([pltpu.VMEM((t, D_LRU), F32)] if final else []),
        compiler_params=_params(("arbitrary", "arbitrary")), name="lru",
    )(*args)


def _rms(x, width):
    return x * lax.rsqrt(jnp.sum(x * x, axis=-1, keepdims=True) * (1.0 / width) + NORM_EPS)


def _rope(r, cos, sin):
    lane = lax.broadcasted_iota(jnp.int32, r.shape, 1)
    first = (lane % 32) < 16
    rot = jnp.where(first, pltpu.roll(r, LANES - 16, axis=1), pltpu.roll(r, 16, axis=1))
    return r * cos + rot * sin


def _qproj_body(x_ref, gn_ref, w_ref, hg_ref, cos_ref, sin_ref, o_ref, xn_scr):
    @pl.when(pl.program_id(2) == 0)
    def _():
        xn_scr[...] = (_rms(x_ref[...], MLA_Q_RANK) * gn_ref[...]).astype(BF16)

    a = jnp.dot(xn_scr[...], w_ref[...], preferred_element_type=F32)
    for hh in range(HEADS_PER_STEP):
        c0 = hh * HEAD_W
        qn = _rms(a[:, c0:c0 + MLA_NOPE], MLA_NOPE) * hg_ref[:, :MLA_NOPE]
        qr = _rms(a[:, c0 + MLA_NOPE:c0 + HEAD_W], MLA_ROPE) * hg_ref[:, MLA_NOPE:]
        qr = _rope(qr, cos_ref[...], sin_ref[...])
        o_ref[hh] = (jnp.concatenate([qn, qr], axis=1) * (MLA_SCALE * LOG2E)).astype(o_ref.dtype)


def q_proj(p, gn, w, hg, cos, sin):
    bsz, l, _ = p.shape
    tm = _tile(l, 512)
    hw = HEADS_PER_STEP * HEAD_W
    return pl.pallas_call(
        _qproj_body, grid=(bsz, l // tm, MLA_HEADS // HEADS_PER_STEP),
        in_specs=[pl.BlockSpec((None, tm, QDN_W), lambda b, i, h: (b, i, COL_QDN // QDN_W)),
                  pl.BlockSpec((1, QDN_W), lambda b, i, h: (0, 0)),
                  pl.BlockSpec((QDN_W, hw), lambda b, i, h: (0, h)),
                  pl.BlockSpec((1, HEAD_W), lambda b, i, h: (0, 0)),
                  pl.BlockSpec((tm, LANES), lambda b, i, h: (i, 0)),
                  pl.BlockSpec((tm, LANES), lambda b, i, h: (i, 0))],
        out_specs=pl.BlockSpec((None, HEADS_PER_STEP, tm, HEAD_W), lambda b, i, h: (b, h, i, 0)),
        out_shape=jax.ShapeDtypeStruct((bsz, MLA_HEADS, l, HEAD_W), BF16),
        scratch_shapes=[pltpu.VMEM((tm, QDN_W), BF16)],
        compiler_params=_params(("arbitrary", "arbitrary", "arbitrary")), name="q_proj",
    )(p, gn.reshape(1, QDN_W), w, hg, cos, sin)


def _kvproj_body(x_ref, kr_ref, gn_ref, w_ref, hg_ref, cos_ref, sin_ref, k_ref, v_ref, xn_scr, kr_scr):
    @pl.when(pl.program_id(2) == 0)
    def _():
        xn_scr[...] = (_rms(x_ref[...], MLA_KV_RANK) * gn_ref[...]).astype(BF16)
        kr = _rms(kr_ref[...], MLA_ROPE) * hg_ref[:, MLA_NOPE:]
        kr_scr[...] = _rope(kr, cos_ref[...], sin_ref[...]).astype(BF16)

    a = jnp.dot(xn_scr[...], w_ref[...], preferred_element_type=F32)
    for hh in range(HEADS_PER_STEP):
        c0 = hh * HEAD_W
        kn = _rms(a[:, c0:c0 + MLA_NOPE], MLA_NOPE) * hg_ref[:, :MLA_NOPE]
        k_ref[hh, :, :MLA_NOPE] = kn.astype(k_ref.dtype)
        k_ref[hh, :, MLA_NOPE:] = kr_scr[...]
        v_ref[hh, :MLA_V, :] = a[:, c0 + MLA_NOPE:c0 + HEAD_W].T.astype(v_ref.dtype)
        v_ref[hh, MLA_V:, :] = jnp.ones((VT_ROWS - MLA_V, a.shape[0]), v_ref.dtype)


def kv_proj(p, gn, w, hg, cos, sin):
    bsz, l, _ = p.shape
    tm = _tile(l, 512)
    hw = HEADS_PER_STEP * HEAD_W
    return pl.pallas_call(
        _kvproj_body, grid=(bsz, l // tm, MLA_HEADS // HEADS_PER_STEP),
        in_specs=[pl.BlockSpec((None, tm, MLA_KV_RANK), lambda b, i, h: (b, i, COL_KVDN // MLA_KV_RANK)),
                  pl.BlockSpec((None, tm, KR_W), lambda b, i, h: (b, i, COL_KR // KR_W)),
                  pl.BlockSpec((1, MLA_KV_RANK), lambda b, i, h: (0, 0)),
                  pl.BlockSpec((MLA_KV_RANK, hw), lambda b, i, h: (0, h)),
                  pl.BlockSpec((1, HEAD_W), lambda b, i, h: (0, 0)),
                  pl.BlockSpec((tm, LANES), lambda b, i, h: (i, 0)),
                  pl.BlockSpec((tm, LANES), lambda b, i, h: (i, 0))],
        out_specs=[pl.BlockSpec((None, HEADS_PER_STEP, tm, HEAD_W), lambda b, i, h: (b, h, i, 0)),
                   pl.BlockSpec((None, HEADS_PER_STEP, VT_ROWS, tm), lambda b, i, h: (b, h, 0, i))],
        out_shape=[jax.ShapeDtypeStruct((bsz, MLA_HEADS, l, HEAD_W), BF16),
                   jax.ShapeDtypeStruct((bsz, MLA_HEADS, VT_ROWS, l), BF16)],
        scratch_shapes=[pltpu.VMEM((tm, MLA_KV_RANK), BF16), pltpu.VMEM((tm, KR_W), BF16)],
        compiler_params=_params(("arbitrary", "arbitrary", "arbitrary")), name="kv_proj",
    )(p, p, gn.reshape(1, MLA_KV_RANK), w, hg, cos, sin)


def _scores_t(k, q):
    return lax.dot_general(k, q, (((1,), (1,)), ((), ())), preferred_element_type=F32)


def _attn_body(*refs, tk, n_lat):
    if n_lat:
        q_ref, kc_ref, vc_ref, k_ref, v_ref, o_ref, m_scr, acc_scr, s_scr = refs
    else:
        q_ref, kc_ref, vc_ref, o_ref = refs

    sc = _scores_t(kc_ref[...], q_ref[...])
    m0 = jnp.max(sc, axis=0, keepdims=True)
    acc = jnp.dot(vc_ref[...], jnp.exp2(sc - m0).astype(BF16), preferred_element_type=F32)

    if n_lat:
        m_scr[...] = m0
        acc_scr[...] = acc

        def qk(slot, off):
            s_scr[slot] = _scores_t(k_ref[pl.ds(off, tk), :], q_ref[...])

        def consume(slot, off):
            s = s_scr[slot]
            m = m_scr[...]
            m_new = jnp.maximum(m, jnp.max(s, axis=0, keepdims=True))
            p = jnp.exp2(s - m_new).astype(BF16)
            m_scr[...] = m_new
            acc_scr[...] = (jnp.exp2(m - m_new) * acc_scr[...]
                            + jnp.dot(v_ref[:, pl.ds(off, tk)], p, preferred_element_type=F32))

        qk(0, 0)

        def pair(jj, carry):
            off = pl.multiple_of(jj * (2 * tk), 2 * tk)
            qk(1, off + tk)
            consume(0, off)
            qk(0, off + 2 * tk)
            consume(1, off + tk)
            return carry

        lax.fori_loop(0, n_lat // 2 - 1, pair, 0)
        off = (n_lat - 2) * tk
        qk(1, off + tk)
        consume(0, off)
        consume(1, off + tk)
        acc = acc_scr[...]
    o_ref[...] = (acc[:MLA_V] / acc[MLA_V:MLA_V + 1]).T.astype(o_ref.dtype)


def attention(q, kc, vc, k=None, v=None):
    bsz, nh, lq, _ = q.shape
    lc = kc.shape[2]
    tq = _tile(lq, 1024)
    keys = lambda n: pl.BlockSpec((None, None, n, HEAD_W), lambda b, h, i: (b, h, 0, 0))
    vals = lambda n: pl.BlockSpec((None, None, VT_ROWS, n), lambda b, h, i: (b, h, 0, 0))
    in_specs = [pl.BlockSpec((None, None, tq, HEAD_W), lambda b, h, i: (b, h, i, 0)), keys(lc), vals(lc)]
    args = [q, kc, vc]
    tk = n_lat = 0
    scratch = []
    if k is not None:
        lk = k.shape[2]
        tk = _tile(lk // 2, 1024)
        assert tk % LANES == 0 and lk % (2 * tk) == 0
        n_lat = lk // tk
        in_specs += [keys(lk), vals(lk)]
        args += [k, v]
        scratch = [pltpu.VMEM((1, tq), F32), pltpu.VMEM((VT_ROWS, tq), F32), pltpu.VMEM((2, tk, tq), F32)]
    return pl.pallas_call(
        functools.partial(_attn_body, tk=tk, n_lat=n_lat),
        grid=(bsz, nh, lq // tq), in_specs=in_specs,
        out_specs=pl.BlockSpec((None, tq, MLA_V), lambda b, h, i: (b, i, h)),
        out_shape=jax.ShapeDtypeStruct((bsz, lq, nh * MLA_V), BF16),
        scratch_shapes=scratch,
        compiler_params=_params(("arbitrary", "arbitrary", "arbitrary")), name="attention",
    )(*args)


def _row_copy(src_hbm, row, buf, slot, r, sem):
    return pltpu.make_async_copy(src_hbm.at[pl.ds(row, 1), :], buf.at[slot, pl.ds(r, 1), :], sem.at[slot])


def _moe_body(be_ref, nu_ref, tok_ref, h_hbm, w13_ref, w2_ref, g_ref, o_ref, xbuf, sem):
    j = pl.program_id(0)
    nu = nu_ref[0]
    slot = j % 2

    def rows(blk, slot, start):
        base = blk * MOE_BLOCK

        def body(r, carry):
            cp = _row_copy(h_hbm, tok_ref[base + r], xbuf, slot, r, sem)
            if start:
                cp.start()
            else:
                cp.wait()
            return carry

        lax.fori_loop(0, MOE_BLOCK, body, 0, unroll=8)

    @pl.when((j == 0) & (nu > 0))
    def _():
        rows(0, 0, True)

    @pl.when(j + 1 < nu)
    def _():
        rows(j + 1, 1 - slot, True)

    @pl.when(j < nu)
    def _():
        rows(j, slot, False)
        hid = jnp.dot(xbuf[slot].astype(BF16), w13_ref[...].astype(BF16), preferred_element_type=F32)
        act = (_silu(hid[:, :D_EXPERT]) * hid[:, D_EXPERT:]).astype(BF16)
        out = jnp.dot(act, w2_ref[...].astype(BF16), preferred_element_type=F32)
        o_ref[...] = out * g_ref[...]

    @pl.when(j >= nu)
    def _():
        o_ref[...] = jnp.zeros_like(o_ref)


def moe_experts(h, row_tok, gate, blk_exp, n_used, w13, w2, layer):
    m, d = h.shape
    p = row_tok.shape[0]
    nb = p // MOE_BLOCK
    grid_spec = pltpu.PrefetchScalarGridSpec(
        num_scalar_prefetch=3, grid=(nb,),
        in_specs=[pl.BlockSpec(memory_space=pl.ANY),
                  pl.BlockSpec((None, None, d, 2 * D_EXPERT), lambda j, be, nu, tok: (layer, be[j], 0, 0)),
                  pl.BlockSpec((None, None, D_EXPERT, d), lambda j, be, nu, tok: (layer, be[j], 0, 0)),
                  pl.BlockSpec((MOE_BLOCK, 1), lambda j, be, nu, tok: (j, 0))],
        out_specs=pl.BlockSpec((MOE_BLOCK, d), lambda j, be, nu, tok: (j, 0)),
        scratch_shapes=[pltpu.VMEM((2, MOE_BLOCK, d), F32), pltpu.SemaphoreType.DMA((2,))])
    return pl.pallas_call(
        _moe_body, grid_spec=grid_spec,
        out_shape=jax.ShapeDtypeStruct((p, d), F32),
        compiler_params=_params(("arbitrary",)), name="moe_experts",
    )(blk_exp, n_used, row_tok, h, w13, w2, gate)


def _combine_body(dest_ref, out_hbm, x_ref, g_ref, o_ref, buf_a, buf_b, sem_a, sem_b, *, tok_off, t, n_tiles):
    f = pl.program_id(0) * pl.num_programs(1) + pl.program_id(1)
    slot = f % 2

    def rows(tile, slot, start):
        base = (tok_off + tile * t) * TOP_K

        def body(r, carry):
            ca = _row_copy(out_hbm, dest_ref[base + TOP_K * r], buf_a, slot, r, sem_a)
            cb = _row_copy(out_hbm, dest_ref[base + TOP_K * r + 1], buf_b, slot, r, sem_b)
            if start:
                ca.start()
                cb.start()
            else:
                ca.wait()
                cb.wait()
            return carry

        lax.fori_loop(0, t, body, 0, unroll=8)

    @pl.when(f == 0)
    def _():
        rows(0, 0, True)

    @pl.when(f + 1 < n_tiles)
    def _():
        rows(f + 1, 1 - slot, True)

    rows(f, slot, False)
    o_ref[...] = x_ref[...] + g_ref[...] * (buf_a[slot] + buf_b[slot])


def moe_combine(x, gate, out, dest, tok_off):
    bx, lx, d = x.shape
    t = _tile(lx, 128)
    nt = lx // t
    grid_spec = pltpu.PrefetchScalarGridSpec(
        num_scalar_prefetch=1, grid=(bx, nt),
        in_specs=[pl.BlockSpec(memory_space=pl.ANY),
                  pl.BlockSpec((None, t, d), lambda b, i, dst: (b, i, 0)),
                  pl.BlockSpec((None, 1, d), lambda b, i, dst: (b, 0, 0))],
        out_specs=pl.BlockSpec((None, t, d), lambda b, i, dst: (b, i, 0)),
        scratch_shapes=[pltpu.VMEM((2, t, d), F32), pltpu.VMEM((2, t, d), F32),
                        pltpu.SemaphoreType.DMA((2,)), pltpu.SemaphoreType.DMA((2,))])
    return pl.pallas_call(
        functools.partial(_combine_body, tok_off=tok_off, t=t, n_tiles=bx * nt), grid_spec=grid_spec,
        out_shape=jax.ShapeDtypeStruct((bx, lx, d), F32),
        compiler_params=_params(("arbitrary", "arbitrary")), name="moe_combine",
    )(dest, out, x, gate)


def moe_route(h, logits, w13, w2, layer):
    m, d = h.shape
    g_logit = logits[:, :N_GROUPS]
    e_logit = logits[:, N_GROUPS:N_GROUPS + N_EXPERTS].reshape(m, N_GROUPS, EXPERTS_PER_GROUP)
    g_idx = jnp.argmax(g_logit, axis=-1)
    p_grp = jnp.take_along_axis(jax.nn.softmax(g_logit, axis=-1), g_idx[:, None], axis=1)
    e_sel = jnp.take_along_axis(e_logit, g_idx[:, None, None], axis=1)[:, 0]
    top_v, top_i = lax.top_k(e_sel, TOP_K)
    gate = (p_grp * jax.nn.softmax(top_v, axis=-1)).reshape(-1)
    e_flat = (g_idx[:, None] * EXPERTS_PER_GROUP + top_i).reshape(-1).astype(jnp.int32)

    a = m * TOP_K
    nb = -(-a // MOE_BLOCK) + N_EXPERTS
    order = jnp.argsort(e_flat).astype(jnp.int32)
    rank = jnp.argsort(order).astype(jnp.int32)
    e_s = e_flat[order]
    experts = jnp.arange(N_EXPERTS, dtype=jnp.int32)
    start = jnp.searchsorted(e_s, experts, side='left').astype(jnp.int32)
    counts = jnp.searchsorted(e_s, experts, side='right').astype(jnp.int32) - start
    padded = -(-counts // MOE_BLOCK) * MOE_BLOCK
    pad_end = jnp.cumsum(padded)
    pad_start = pad_end - padded
    blk_ids = jnp.arange(nb, dtype=jnp.int32)
    blk_raw = jnp.searchsorted(pad_end, blk_ids * MOE_BLOCK, side='right')
    blk_exp = jnp.minimum(blk_raw, N_EXPERTS - 1).astype(jnp.int32)
    n_used = (pad_end[-1] // MOE_BLOCK).astype(jnp.int32).reshape(1)
    r = (blk_ids * MOE_BLOCK - pad_start[blk_exp])[:, None] + jnp.arange(MOE_BLOCK, dtype=jnp.int32)[None, :]
    valid = ((r < counts[blk_exp][:, None]) & (blk_raw < N_EXPERTS)[:, None]).reshape(-1)
    src = order[jnp.where(valid, (start[blk_exp][:, None] + r).reshape(-1), 0)]
    row_tok = jnp.where(valid, src // TOP_K, 0).astype(jnp.int32)
    row_gate = jnp.where(valid, gate[src], 0.0)
    dest = (pad_start[e_flat] + rank - start[e_flat]).astype(jnp.int32)

    out = moe_experts(h, row_tok, row_gate[:, None], blk_exp, n_used, w13, w2, layer)
    return out, dest


def _prep_w_in(w):
    sizes = (512, 512, 1024, 1024, 32, 768, 512, 64, 1024, 1024)
    offs = [0]
    for s in sizes:
        offs.append(offs[-1] + s)
    q, k, v, g, al, qdn, kvdn, kr, xb, gb = [w[:, offs[i]:offs[i + 1]] for i in range(10)]
    zpad = lambda t, n: jnp.pad(t, ((0, 0), (0, n - t.shape[1])))
    return jnp.concatenate([qdn, zpad(al, AL_W), zpad(kr, KR_W), v, g, xb, gb, q, k, kvdn],
                           axis=1).astype(BF16)


def _prep_w_uq(w):
    w = w.reshape(MLA_Q_RANK, MLA_HEADS, MLA_NOPE + MLA_ROPE)
    w = jnp.pad(w, ((0, 0), (0, 0), (0, HEAD_W - MLA_NOPE - MLA_ROPE)))
    return w.reshape(MLA_Q_RANK, MLA_HEADS * HEAD_W).astype(BF16)


def _head_gain(g):
    return jnp.pad(g, (0, HEAD_W - g.shape[0])).reshape(1, HEAD_W)


def _rope_tables(rows):
    n = MLA_ROPE // 4
    inv = ROPE_BASE ** (-jnp.arange(n, dtype=F32) / n)
    r = jnp.repeat(jnp.arange(rows, dtype=F32), GRID_W)
    c = jnp.tile(jnp.arange(GRID_W, dtype=F32), rows)
    ar, ac = r[:, None] * inv, c[:, None] * inv
    l = ar.shape[0]
    cos = jnp.concatenate([jnp.cos(ar), jnp.cos(ar), jnp.cos(ac), jnp.cos(ac),
                           jnp.ones((l, LANES - MLA_ROPE), F32)], axis=1)
    sin = jnp.concatenate([-jnp.sin(ar), jnp.sin(ar), -jnp.sin(ac), jnp.sin(ac),
                           jnp.zeros((l, LANES - MLA_ROPE), F32)], axis=1)
    return cos, sin


def _gla_gate_w(w_a2, b_a, d):
    hq = GLA_HEADS * GLA_DK
    w = jnp.zeros((AL_W, hq), F32).at[d * GLA_GATE_RANK:(d + 1) * GLA_GATE_RANK].set(w_a2[d])
    return w, b_a[d].reshape(1, hq)


def _lru_gate_w(w_gate, d):
    return jnp.concatenate([w_gate[d, 0], w_gate[d, 1]], axis=-1)


def _layer(x, xc, mods, rope, prm, moe_w13, moe_w2, layer, last):
    (norm1_g, norm2_g, w_in, gla_w_a2, gla_b_a, gla_norm_g,
     mla_q_norm_g, mla_kv_norm_g, mla_w_uq, mla_w_ukv, mla_q_head_g, mla_k_head_g,
     lru_conv_w, lru_conv_b, lru_w_gate, lru_b_gate, lru_lambda, w_out,
     moe_w_grp, moe_b_grp, moe_w_exp, moe_b_exp) = prm
    bsz, l, d = x.shape
    lc = xc.shape[1]
    ctx_out = not last
    lat = lambda k: mods[:bsz, k].reshape(bsz, 1, d)
    ctx = lambda k: jnp.broadcast_to(mods[bsz, k].reshape(1, 1, d), (bsz, 1, d))

    w_in_p = _prep_w_in(w_in)
    h = norm_mod(x, norm1_g, lat(0), lat(1))
    hc = norm_mod(xc, norm1_g, ctx(0), ctx(1))
    p = matmul(h.reshape(bsz * l, d), w_in_p).reshape(bsz, l, D_INP)
    pc = matmul(hc.reshape(bsz * lc, d), w_in_p).reshape(bsz, lc, D_INP)

    waf, baf = _gla_gate_w(gla_w_a2, gla_b_a, 0)
    wab, bab = _gla_gate_w(gla_w_a2, gla_b_a, 1)
    s_zero = jnp.zeros((bsz, GLA_HEADS, GLA_DK, GLA_DV), F32)
    ocf, s_f = gla_pass(pc, waf, baf, s_zero, False)
    of, _ = gla_pass(p, waf, baf, s_f, False)
    if ctx_out:
        yac, s_b = gla_pass(pc, wab, bab, s_zero, True, ocf, gla_norm_g)
    else:
        _, s_b = gla_pass(pc, wab, bab, s_zero, True)
    ya, _ = gla_pass(p, wab, bab, s_b, True, of, gla_norm_g)

    wgf, wgb = _lru_gate_w(lru_w_gate, 0), _lru_gate_w(lru_w_gate, 1)
    h_zero = jnp.zeros((bsz, 1, D_LRU), F32)
    lru = functools.partial(lru_pass, cw=lru_conv_w, cb=lru_conv_b)
    hcf, h0f = lru(pc, wg=wgf, bg=lru_b_gate[0], lam=lru_lambda[0], h0=h_zero, reverse=False)
    hf, _ = lru(p, wg=wgf, bg=lru_b_gate[0], lam=lru_lambda[0], h0=h0f, reverse=False)
    if ctx_out:
        ylc, h0b = lru(pc, wg=wgb, bg=lru_b_gate[1], lam=lru_lambda[1], h0=h_zero, reverse=True, h_fwd=hcf)
    else:
        _, h0b = lru(pc, wg=wgb, bg=lru_b_gate[1], lam=lru_lambda[1], h0=h_zero, reverse=True)
    yl, _ = lru(p, wg=wgb, bg=lru_b_gate[1], lam=lru_lambda[1], h0=h0b, reverse=True, h_fwd=hf)

    cos, sin = rope
    cos_c = jnp.ones((lc, LANES), F32)
    sin_c = jnp.zeros((lc, LANES), F32)
    w_uq = _prep_w_uq(mla_w_uq)
    w_ukv = mla_w_ukv.astype(BF16)
    qg, kg = _head_gain(mla_q_head_g), _head_gain(mla_k_head_g)
    kc, vc = kv_proj(pc, mla_kv_norm_g, w_ukv, kg, cos_c, sin_c)
    kl, vl = kv_proj(p, mla_kv_norm_g, w_ukv, kg, cos, sin)
    ql = q_proj(p, mla_q_norm_g, w_uq, qg, cos, sin)
    yb = attention(ql, kc, vc, kl, vl)

    w_out_b = w_out.astype(BF16)
    x = out_proj(ya, yb, yl, w_out_b, x, lat(2))
    if ctx_out:
        qc = q_proj(pc, mla_q_norm_g, w_uq, qg, cos_c, sin_c)
        ybc = attention(qc, kc, vc)
        xc = out_proj(yac, ybc, ylc, w_out_b, xc, ctx(2))

    nr = LANES
    wr = jnp.pad(jnp.concatenate([moe_w_grp, moe_w_exp], axis=1), ((0, 0), (0, nr - N_GROUPS - N_EXPERTS)))
    br = jnp.pad(jnp.concatenate([moe_b_grp, moe_b_exp]), (0, nr - N_GROUPS - N_EXPERTS)).reshape(1, nr)
    ctx1 = lambda k: mods[bsz, k].reshape(1, 1, d)
    h2, lg = norm_router(x, norm2_g, lat(3), lat(4), wr, br,
                         ctx=(xc, ctx1(3), ctx1(4)) if ctx_out else None)
    out, dest = moe_route(h2, lg, moe_w13, moe_w2, layer)
    if ctx_out:
        xc = moe_combine(xc, ctx(5), out, dest, 0)
        x = moe_combine(x, lat(5), out, dest, bsz * lc)
    else:
        x = moe_combine(x, lat(5), out, dest, 0)
    return x, xc


def kernel(x, c, ctx, c_ctx, w_mod, b_mod, norm1_g, norm2_g, w_in, gla_w_a2, gla_b_a, gla_norm_g,
           mla_q_norm_g, mla_kv_norm_g, mla_w_uq, mla_w_ukv, mla_q_head_g, mla_k_head_g,
           lru_conv_w, lru_conv_b, lru_w_gate, lru_b_gate, lru_lambda, w_out,
           moe_w_grp, moe_b_grp, moe_w_exp, moe_b_exp, moe_w13, moe_w2):
    bsz, l, d = x.shape
    depth = w_mod.shape[0]
    assert bsz + 1 <= SUBLANES and l % GRID_W == 0
    cond = jnp.zeros((SUBLANES, d), F32).at[:bsz].set(c).at[bsz].set(c_ctx)
    mods = adaln_all(cond, w_mod, b_mod).reshape(depth, SUBLANES, 6, d)
    rope = _rope_tables(l // GRID_W)
    per_layer = (norm1_g, norm2_g, w_in, gla_w_a2, gla_b_a, gla_norm_g,
                 mla_q_norm_g, mla_kv_norm_g, mla_w_uq, mla_w_ukv, mla_q_head_g, mla_k_head_g,
                 lru_conv_w, lru_conv_b, lru_w_gate, lru_b_gate, lru_lambda, w_out,
                 moe_w_grp, moe_b_grp, moe_w_exp, moe_b_exp)
    xc = ctx
    for i in range(depth):
        x, xc = _layer(x, xc, mods[i], rope, tuple(t[i] for t in per_layer), moe_w13, moe_w2, i,
                       last=(i == depth - 1))
    return x
```

```python
import functools

import jax
import jax.numpy as jnp
from jax import lax
from jax.experimental import pallas as pl
from jax.experimental.pallas import tpu as pltpu

F32 = jnp.float32
BF16 = jnp.bfloat16
HI = lax.Precision.HIGHEST

D_MODEL = 4096
GRID_W = 64
NORM_EPS = 1e-6
D_GLA = D_MODEL // 4
D_MLA = D_MODEL // 2
D_LRU = D_MODEL // 4
GLA_HEADS = 4
GLA_DK = 128
GLA_DV = 256
GLA_GATE_RANK = 16
GLA_TAU = 16.0
GLA_CHUNK = 64
GLA_QSCALE = GLA_DK ** -0.5
MLA_HEADS = 16
MLA_NOPE = 128
MLA_ROPE = 64
MLA_V = 128
MLA_Q_RANK = 768
MLA_KV_RANK = 512
MLA_SCALE = (MLA_NOPE + MLA_ROPE) ** -0.5
LOG2E = 1.4426950408889634
ROPE_BASE = 10000.0
LRU_BLOCKS = 8
LRU_BW = 128
LRU_CONV = 4
LRU_C = 8.0
N_GROUPS = 8
EXPERTS_PER_GROUP = 8
N_EXPERTS = 64
TOP_K = 2
D_EXPERT = 256

LANES = 128
SUBLANES = 8
VMEM_BYTES = 64 << 20
VMEM_LIMIT = VMEM_BYTES - (8 << 20)
MOE_BLOCK = 256

QDN_W, AL_W, KR_W = MLA_Q_RANK, LANES, LANES
COL_QDN = 0
COL_AL = 768
COL_KR = 896
COL_V = 1024
COL_G = 2048
COL_XB = 3072
COL_GB = 4096
COL_Q = 5120
COL_K = 5632
COL_KVDN = 6144
D_INP = 6656
HEAD_W = 256
VT_ROWS = MLA_V + 16
HEADS_PER_STEP = 4


def _params(sem, vmem=VMEM_LIMIT):
    return pltpu.CompilerParams(dimension_semantics=sem, vmem_limit_bytes=vmem)


def _tile(n, pref):
    t = min(n, pref)
    while n % t or t % SUBLANES:
        t -= 1
    return t


def _silu(x):
    return x * jax.nn.sigmoid(x)


def _softplus(x):
    return jnp.maximum(x, 0.0) + jnp.log1p(jnp.exp(-jnp.abs(x)))


def _pack_halves(x):
    w = x.shape[1] // 2
    bits = lax.bitcast_convert_type(x.astype(BF16).astype(F32), jnp.int32)
    return bits[:, w:] | lax.shift_right_logical(bits[:, :w], jnp.int32(16))


def _unpack_halves(p):
    lo = lax.bitcast_convert_type(lax.shift_left(p, jnp.int32(16)), F32)
    hi = lax.bitcast_convert_type(p & jnp.int32(-65536), F32)
    return lo, hi


def _adaln_body(c_ref, w_ref, b_ref, o_ref):
    s = _silu(c_ref[...])
    o_ref[...] = jnp.dot(s, w_ref[...], preferred_element_type=F32, precision=HI) + b_ref[...]


def adaln_all(cond, w_mod, b_mod):
    depth, d, n = w_mod.shape
    tn = 512
    return pl.pallas_call(
        _adaln_body,
        grid=(depth, n // tn),
        in_specs=[pl.BlockSpec((SUBLANES, d), lambda l, j: (0, 0)),
                  pl.BlockSpec((None, d, tn), lambda l, j: (l, 0, j)),
                  pl.BlockSpec((None, 1, tn), lambda l, j: (l, 0, j))],
        out_specs=pl.BlockSpec((None, SUBLANES, tn), lambda l, j: (l, 0, j)),
        out_shape=jax.ShapeDtypeStruct((depth, SUBLANES, n), F32),
        compiler_params=_params(("arbitrary", "arbitrary")),
        name="adaln",
    )(cond, w_mod, b_mod.reshape(depth, 1, n))


def _norm_mod(x, g, shift, scale):
    y = x * lax.rsqrt(jnp.mean(x * x, axis=-1, keepdims=True) + NORM_EPS) * g
    return y * (1.0 + scale) + shift


def _norm_mod_body(x_ref, g_ref, sh_ref, sc_ref, o_ref):
    o_ref[...] = _norm_mod(x_ref[...], g_ref[...], sh_ref[...], sc_ref[...]).astype(o_ref.dtype)


def _norm_router_body(*refs, n_ctx):
    if n_ctx:
        (xc_ref, shc_ref, scc_ref, x_ref, sh_ref, sc_ref, g_ref, wr_ref, br_ref, o_ref, lg_ref) = refs
        is_ctx = pl.program_id(0) < n_ctx
        x = jnp.where(is_ctx, xc_ref[...], x_ref[...])
        sh = jnp.where(is_ctx, shc_ref[...], sh_ref[...])
        sc = jnp.where(is_ctx, scc_ref[...], sc_ref[...])
    else:
        x_ref, sh_ref, sc_ref, g_ref, wr_ref, br_ref, o_ref, lg_ref = refs
        x, sh, sc = x_ref[...], sh_ref[...], sc_ref[...]
    h = _norm_mod(x, g_ref[...], sh, sc)
    o_ref[...] = _pack_halves(h)
    lg_ref[...] = jnp.dot(h, wr_ref[...], preferred_element_type=F32, precision=HI) + br_ref[...]


def norm_mod(x, g, shift, scale):
    bx, lx, d = x.shape
    tm = _tile(lx, 256)
    row = pl.BlockSpec((None, tm, d), lambda b, i: (b, i, 0))
    vec = pl.BlockSpec((1, d), lambda b, i: (0, 0))
    mod = pl.BlockSpec((None, 1, d), lambda b, i: (b, 0, 0))
    return pl.pallas_call(
        _norm_mod_body, grid=(bx, lx // tm),
        in_specs=[row, vec, mod, mod], out_specs=row,
        out_shape=jax.ShapeDtypeStruct((bx, lx, d), BF16),
        compiler_params=_params(("arbitrary", "arbitrary")), name="norm_mod",
    )(x, g.reshape(1, d), shift, scale)


def norm_router(x, g, shift, scale, wr, br, ctx=None):
    bsz, l, d = x.shape
    nr = wr.shape[1]
    lc = ctx[0].shape[1] if ctx is not None else 0
    tm = _tile(l, 256) if ctx is None else _tile(lc, 256)
    assert l % tm == 0
    n_ctx = bsz * lc // tm
    n_lat = bsz * l // tm
    per_b = l // tm
    lat = lambda f: jnp.maximum(f - n_ctx, 0)
    vec = lambda w: pl.BlockSpec((1, w), lambda f: (0, 0))
    in_specs, args = [], []
    if ctx is not None:
        xc, shc, scc = ctx
        one = pl.BlockSpec((None, 1, d), lambda f: (0, 0, 0))
        in_specs += [pl.BlockSpec((tm, d), lambda f: (jnp.minimum(f, n_ctx - 1), 0)), one, one]
        args += [xc.reshape(bsz * lc, d), shc, scc]
    mod = pl.BlockSpec((None, 1, d), lambda f: (lat(f) // per_b, 0, 0))
    in_specs += [pl.BlockSpec((tm, d), lambda f: (lat(f), 0)), mod, mod, vec(d),
                 pl.BlockSpec((d, nr), lambda f: (0, 0)), vec(nr)]
    args += [x.reshape(bsz * l, d), shift, scale, g.reshape(1, d), wr, br]
    total = bsz * (lc + l)
    return pl.pallas_call(
        functools.partial(_norm_router_body, n_ctx=n_ctx), grid=(n_ctx + n_lat,), in_specs=in_specs,
        out_specs=[pl.BlockSpec((tm, d // 2), lambda f: (f, 0)), pl.BlockSpec((tm, nr), lambda f: (f, 0))],
        out_shape=[jax.ShapeDtypeStruct((total, d // 2), jnp.int32), jax.ShapeDtypeStruct((total, nr), F32)],
        compiler_params=_params(("arbitrary",)), name="norm_router",
    )(*args)


def _mm_body(a_ref, w_ref, o_ref):
    o_ref[...] = jnp.dot(a_ref[...], w_ref[...], preferred_element_type=F32).astype(o_ref.dtype)


def matmul(a, w, out_dtype=F32, tm_pref=1024, tn=512):
    m, k = a.shape
    n = w.shape[1]
    tm = _tile(m, tm_pref)
    return pl.pallas_call(
        _mm_body, grid=(m // tm, n // tn),
        in_specs=[pl.BlockSpec((tm, k), lambda i, j: (i, 0)),
                  pl.BlockSpec((k, tn), lambda i, j: (0, j))],
        out_specs=pl.BlockSpec((tm, tn), lambda i, j: (i, j)),
        out_shape=jax.ShapeDtypeStruct((m, n), out_dtype),
        compiler_params=_params(("arbitrary", "arbitrary")), name="matmul",
    )(a, w)


def _wout_body(ya_ref, yb_ref, yl_ref, w_ref, x_ref, g_ref, o_ref):
    acc = jnp.dot(ya_ref[...], w_ref[0:D_GLA, :], preferred_element_type=F32)
    acc += jnp.dot(yb_ref[...], w_ref[D_GLA:D_GLA + D_MLA, :], preferred_element_type=F32)
    acc += jnp.dot(yl_ref[...], w_ref[D_GLA + D_MLA:, :], preferred_element_type=F32)
    o_ref[...] = x_ref[...] + g_ref[...] * acc


def out_proj(ya, yb, yl, w, x, gate):
    bx, lx, d = x.shape
    tm = _tile(lx, 1024)
    tn = 512
    return pl.pallas_call(
        _wout_body, grid=(bx, lx // tm, d // tn),
        in_specs=[pl.BlockSpec((None, tm, D_GLA), lambda b, i, j: (b, i, 0)),
                  pl.BlockSpec((None, tm, D_MLA), lambda b, i, j: (b, i, 0)),
                  pl.BlockSpec((None, tm, D_LRU), lambda b, i, j: (b, i, 0)),
                  pl.BlockSpec((d, tn), lambda b, i, j: (0, j)),
                  pl.BlockSpec((None, tm, tn), lambda b, i, j: (b, i, j)),
                  pl.BlockSpec((None, 1, tn), lambda b, i, j: (b, 0, j))],
        out_specs=pl.BlockSpec((None, tm, tn), lambda b, i, j: (b, i, j)),
        out_shape=jax.ShapeDtypeStruct((bx, lx, d), F32),
        compiler_params=_params(("arbitrary", "arbitrary", "arbitrary")), name="out_proj",
    )(ya, yb, yl, w, x, gate)


def _gla_body(*refs, reverse, final, nchunk):
    if final:
        (q_ref, k_ref, v_ref, al_ref, wa_ref, ba_ref, s0_ref, of_ref, g_ref, ng_ref,
         o_ref, sfin_ref, s_scr) = refs
    else:
        q_ref, k_ref, v_ref, al_ref, wa_ref, ba_ref, s0_ref, o_ref, sfin_ref, s_scr = refs
    C = GLA_CHUNK

    @pl.when(pl.program_id(1) == 0)
    def _():
        s_scr[...] = s0_ref[...]

    r_i = lax.broadcasted_iota(jnp.int32, (C, C), 0)
    c_i = lax.broadcasted_iota(jnp.int32, (C, C), 1)
    keep = (c_i >= r_i) if reverse else (c_i <= r_i)
    tri = jnp.where(keep, 1.0, 0.0).astype(BF16)
    ref_row = C - 1 - C // 2 if reverse else C // 2
    last_row = 0 if reverse else C - 1

    order = range(nchunk - 1, -1, -1) if reverse else range(nchunk)
    for ci in order:
        rows = slice(ci * C, (ci + 1) * C)
        z = jnp.dot(al_ref[rows, :].astype(BF16), wa_ref[...], preferred_element_type=F32) + ba_ref[...]
        logd = (jnp.minimum(z, 0.0) - jnp.log1p(jnp.exp(-jnp.abs(z)))) * (1.0 / GLA_TAU)
        l_hi = logd.astype(BF16)
        rem = logd - l_hi.astype(F32)
        l_mid = rem.astype(BF16)
        l_lo = (rem - l_mid.astype(F32)).astype(BF16)
        b = (jnp.dot(tri, l_hi, preferred_element_type=F32) + jnp.dot(tri, l_mid, preferred_element_type=F32)
             + jnp.dot(tri, l_lo, preferred_element_type=F32))
        b_ref = b[ref_row:ref_row + 1, :]
        b_last = b[last_row:last_row + 1, :]
        q = q_ref[rows, :] * GLA_QSCALE
        k = k_ref[rows, :]
        qe = (q * jnp.exp(b - b_ref)).astype(BF16)
        ke = (k * jnp.exp(b_ref - b)).astype(BF16)
        qin = (q * jnp.exp(b)).astype(BF16)
        kst = k * jnp.exp(b_last - b)
        for h in range(GLA_HEADS):
            hs = slice(h * GLA_DK, (h + 1) * GLA_DK)
            vs = slice(h * GLA_DV, (h + 1) * GLA_DV)
            v = v_ref[rows, vs].astype(BF16)
            att = lax.dot_general(qe[:, hs], ke[:, hs], (((1,), (1,)), ((), ())),
                                  preferred_element_type=F32)
            att = jnp.where(keep, att, 0.0).astype(BF16)
            s = s_scr[h]
            o = (jnp.dot(att, v, preferred_element_type=F32)
                 + jnp.dot(qin[:, hs], s.astype(BF16), preferred_element_type=F32))
            d_col = jnp.exp(jnp.broadcast_to(b_last[:, hs], (GLA_DK, GLA_DK))).T
            s_scr[h] = (jnp.concatenate([d_col, d_col], axis=1) * s
                        + jnp.dot(kst[:, hs].T.astype(BF16), v, preferred_element_type=F32))
            if final:
                o = o + of_ref[rows, vs]
                y = o * lax.rsqrt(jnp.mean(o * o, axis=-1, keepdims=True) + NORM_EPS) * ng_ref[...]
                o_ref[rows, vs] = (y * _silu(g_ref[rows, vs])).astype(o_ref.dtype)
            else:
                o_ref[rows, vs] = o
    sfin_ref[...] = s_scr[...]


def gla_pass(p, wa, ba, s0, reverse, o_fwd=None, norm_g=None):
    bsz, l, _ = p.shape
    t = _tile(l, 256)
    nt = l // t
    final = o_fwd is not None

    def rix(i):
        return nt - 1 - i if reverse else i

    def col(width, off):
        return pl.BlockSpec((None, t, width), lambda b, i: (b, rix(i), off // width))

    hq = GLA_HEADS * GLA_DK
    state = pl.BlockSpec((None, GLA_HEADS, GLA_DK, GLA_DV), lambda b, i: (b, 0, 0, 0))
    in_specs = [col(hq, COL_Q), col(hq, COL_K), col(D_GLA, COL_V), col(AL_W, COL_AL),
                pl.BlockSpec((AL_W, hq), lambda b, i: (0, 0)),
                pl.BlockSpec((1, hq), lambda b, i: (0, 0)), state]
    args = [p, p, p, p, wa, ba, s0]
    orow = pl.BlockSpec((None, t, D_GLA), lambda b, i: (b, rix(i), 0))
    if final:
        in_specs += [orow, col(D_GLA, COL_G), pl.BlockSpec((1, GLA_DV), lambda b, i: (0, 0))]
        args += [o_fwd, p, norm_g.reshape(1, GLA_DV)]
    return pl.pallas_call(
        functools.partial(_gla_body, reverse=reverse, final=final, nchunk=t // GLA_CHUNK),
        grid=(bsz, nt), in_specs=in_specs, out_specs=[orow, state],
        out_shape=[jax.ShapeDtypeStruct((bsz, l, D_GLA), BF16 if final else F32),
                   jax.ShapeDtypeStruct((bsz, GLA_HEADS, GLA_DK, GLA_DV), F32)],
        scratch_shapes=[pltpu.VMEM((GLA_HEADS, GLA_DK, GLA_DV), F32)],
        compiler_params=_params(("arbitrary", "arbitrary")), name="gla",
    )(*args)


def _lru_body(*refs, reverse, final, t, nt):
    if final:
        (x_ref, xp_ref, xn_ref, cw_ref, cb_ref, wg_ref, bg_ref, lam_ref, h0_ref, hf_ref, gb_ref,
         o_ref, hfin_ref, h_scr, hbuf) = refs
    else:
        (x_ref, xp_ref, xn_ref, cw_ref, cb_ref, wg_ref, bg_ref, lam_ref, h0_ref,
         o_ref, hfin_ref, h_scr) = refs
        hbuf = o_ref
    i = pl.program_id(1)
    ti = nt - 1 - i if reverse else i

    @pl.when(i == 0)
    def _():
        h_scr[...] = h0_ref[...]

    x = x_ref[...]
    xp = jnp.where(ti == 0, 0.0, xp_ref[...])
    xn = jnp.where(ti == nt - 1, 0.0, xn_ref[...])
    ext = jnp.concatenate([xp, x, xn], axis=0)
    xc = cb_ref[...]
    for j in range(LRU_CONV):
        off = SUBLANES - 2 + j
        xc = xc + ext[off:off + t, :] * cw_ref[j:j + 1, :]

    zs = []
    for n in range(LRU_BLOCKS):
        zs.append(jnp.dot(xc[:, n * LRU_BW:(n + 1) * LRU_BW].astype(BF16), wg_ref[n],
                          preferred_element_type=F32))
    z_r = jnp.concatenate([z[:, :LRU_BW] for z in zs], axis=1) + bg_ref[0:1, :]
    z_i = jnp.concatenate([z[:, LRU_BW:] for z in zs], axis=1) + bg_ref[1:2, :]
    log_a = -LRU_C * jax.nn.sigmoid(z_r) * _softplus(-lam_ref[...])
    a = jnp.exp(log_a)
    th = jnp.tanh(log_a)
    one_minus_a2 = -2.0 * th / (1.0 - th)
    u = jnp.sqrt(one_minus_a2) * jax.nn.sigmoid(z_i) * xc

    sub = lax.broadcasted_iota(jnp.int32, (t, 1), 0) % SUBLANES
    s = 1
    while s < SUBLANES:
        if reverse:
            ok = sub < SUBLANES - s
            a_s = pltpu.roll(a, t - s, axis=0)
            u_s = pltpu.roll(u, t - s, axis=0)
        else:
            ok = sub >= s
            a_s = pltpu.roll(a, s, axis=0)
            u_s = pltpu.roll(u, s, axis=0)
        u = jnp.where(ok, a * u_s + u, u)
        a = jnp.where(ok, a * a_s, a)
        s *= 2
    ngroup = t // SUBLANES
    edge = 0 if reverse else SUBLANES - 1
    h_prev = h_scr[...]
    for gi in (range(ngroup - 1, -1, -1) if reverse else range(ngroup)):
        rows = slice(gi * SUBLANES, (gi + 1) * SUBLANES)
        hg = a[rows, :] * h_prev + u[rows, :]
        h_prev = hg[edge:edge + 1, :]
        hbuf[rows, :] = hg
    h_scr[...] = h_prev
    hfin_ref[...] = h_prev
    if final:
        g = gb_ref[...]
        gelu = 0.5 * g * (1.0 + jnp.tanh(0.7978845608028654 * (g + 0.044715 * g * g * g)))
        o_ref[...] = (gelu * (hf_ref[...] + hbuf[...])).astype(o_ref.dtype)


def lru_pass(p, cw, cb, wg, bg, lam, h0, reverse, h_fwd=None):
    bsz, l, _ = p.shape
    t = _tile(l, 256)
    nt = l // t
    tb = t // SUBLANES
    nb8 = l // SUBLANES
    final = h_fwd is not None
    xoff = COL_XB // D_LRU

    def rix(i):
        return nt - 1 - i if reverse else i

    vec = lambda r: pl.BlockSpec((r, D_LRU), lambda b, i: (0, 0))
    one = pl.BlockSpec((None, 1, D_LRU), lambda b, i: (b, 0, 0))
    orow = pl.BlockSpec((None, t, D_LRU), lambda b, i: (b, rix(i), 0))
    in_specs = [pl.BlockSpec((None, t, D_LRU), lambda b, i: (b, rix(i), xoff)),
                pl.BlockSpec((None, SUBLANES, D_LRU),
                             lambda b, i: (b, jnp.maximum(rix(i) * tb - 1, 0), xoff)),
                pl.BlockSpec((None, SUBLANES, D_LRU),
                             lambda b, i: (b, jnp.minimum((rix(i) + 1) * tb, nb8 - 1), xoff)),
                vec(LRU_CONV), vec(1),
                pl.BlockSpec((LRU_BLOCKS, LRU_BW, 2 * LRU_BW), lambda b, i: (0, 0, 0)),
                vec(2), vec(1), one]
    args = [p, p, p, cw, cb.reshape(1, D_LRU), wg, bg, lam.reshape(1, D_LRU), h0]
    if final:
        in_specs += [orow, pl.BlockSpec((None, t, D_LRU), lambda b, i: (b, rix(i), COL_GB // D_LRU))]
        args += [h_fwd, p]
    return pl.pallas_call(
        functools.partial(_lru_body, reverse=reverse, final=final, t=t, nt=nt),
        grid=(bsz, nt), in_specs=in_specs, out_specs=[orow, one],
        out_shape=[jax.ShapeDtypeStruct((bsz, l, D_LRU), BF16 if final else F32),
                   jax.ShapeDtypeStruct((bsz, 1, D_LRU), F32)],
        scratch_shapes=[pltpu.VMEM((1, D_LRU), F32)] + ([pltpu.VMEM((t, D_LRU), F32)] if final else []),
        compiler_params=_params(("arbitrary", "arbitrary")), name="lru",
    )(*args)


def _rms(x, width):
    return x * lax.rsqrt(jnp.sum(x * x, axis=-1, keepdims=True) * (1.0 / width) + NORM_EPS)


def _rope(r, cos, sin):
    lane = lax.broadcasted_iota(jnp.int32, r.shape, 1)
    first = (lane % 32) < 16
    rot = jnp.where(first, pltpu.roll(r, LANES - 16, axis=1), pltpu.roll(r, 16, axis=1))
    return r * cos + rot * sin


def _qproj_body(x_ref, gn_ref, w_ref, hg_ref, cos_ref, sin_ref, o_ref, xn_scr):
    @pl.when(pl.program_id(2) == 0)
    def _():
        xn_scr[...] = (_rms(x_ref[...], MLA_Q_RANK) * gn_ref[...]).astype(BF16)

    a = jnp.dot(xn_scr[...], w_ref[...], preferred_element_type=F32)
    for hh in range(HEADS_PER_STEP):
        c0 = hh * HEAD_W
        qn = _rms(a[:, c0:c0 + MLA_NOPE], MLA_NOPE) * hg_ref[:, :MLA_NOPE]
        qr = _rms(a[:, c0 + MLA_NOPE:c0 + HEAD_W], MLA_ROPE) * hg_ref[:, MLA_NOPE:]
        qr = _rope(qr, cos_ref[...], sin_ref[...])
        o_ref[hh] = (jnp.concatenate([qn, qr], axis=1) * (MLA_SCALE * LOG2E)).astype(o_ref.dtype)


def q_proj(p, gn, w, hg, cos, sin):
    bsz, l, _ = p.shape
    tm = _tile(l, 512)
    hw = HEADS_PER_STEP * HEAD_W
    return pl.pallas_call(
        _qproj_body, grid=(bsz, l // tm, MLA_HEADS // HEADS_PER_STEP),
        in_specs=[pl.BlockSpec((None, tm, QDN_W), lambda b, i, h: (b, i, COL_QDN // QDN_W)),
                  pl.BlockSpec((1, QDN_W), lambda b, i, h: (0, 0)),
                  pl.BlockSpec((QDN_W, hw), lambda b, i, h: (0, h)),
                  pl.BlockSpec((1, HEAD_W), lambda b, i, h: (0, 0)),
                  pl.BlockSpec((tm, LANES), lambda b, i, h: (i, 0)),
                  pl.BlockSpec((tm, LANES), lambda b, i, h: (i, 0))],
        out_specs=pl.BlockSpec((None, HEADS_PER_STEP, tm, HEAD_W), lambda b, i, h: (b, h, i, 0)),
        out_shape=jax.ShapeDtypeStruct((bsz, MLA_HEADS, l, HEAD_W), BF16),
        scratch_shapes=[pltpu.VMEM((tm, QDN_W), BF16)],
        compiler_params=_params(("arbitrary", "arbitrary", "arbitrary")), name="q_proj",
    )(p, gn.reshape(1, QDN_W), w, hg, cos, sin)


def _kvproj_body(x_ref, kr_ref, gn_ref, w_ref, hg_ref, cos_ref, sin_ref, k_ref, v_ref, xn_scr, kr_scr):
    @pl.when(pl.program_id(2) == 0)
    def _():
        xn_scr[...] = (_rms(x_ref[...], MLA_KV_RANK) * gn_ref[...]).astype(BF16)
        kr = _rms(kr_ref[...], MLA_ROPE) * hg_ref[:, MLA_NOPE:]
        kr_scr[...] = _rope(kr, cos_ref[...], sin_ref[...]).astype(BF16)

    a = jnp.dot(xn_scr[...], w_ref[...], preferred_element_type=F32)
    for hh in range(HEADS_PER_STEP):
        c0 = hh * HEAD_W
        kn = _rms(a[:, c0:c0 + MLA_NOPE], MLA_NOPE) * hg_ref[:, :MLA_NOPE]
        k_ref[hh, :, :MLA_NOPE] = kn.astype(k_ref.dtype)
        k_ref[hh, :, MLA_NOPE:] = kr_scr[...]
        v_ref[hh, :MLA_V, :] = a[:, c0 + MLA_NOPE:c0 + HEAD_W].T.astype(v_ref.dtype)
        v_ref[hh, MLA_V:, :] = jnp.ones((VT_ROWS - MLA_V, a.shape[0]), v_ref.dtype)


def kv_proj(p, gn, w, hg, cos, sin):
    bsz, l, _ = p.shape
    tm = _tile(l, 512)
    hw = HEADS_PER_STEP * HEAD_W
    return pl.pallas_call(
        _kvproj_body, grid=(bsz, l // tm, MLA_HEADS // HEADS_PER_STEP),
        in_specs=[pl.BlockSpec((None, tm, MLA_KV_RANK), lambda b, i, h: (b, i, COL_KVDN // MLA_KV_RANK)),
                  pl.BlockSpec((None, tm, KR_W), lambda b, i, h: (b, i, COL_KR // KR_W)),
                  pl.BlockSpec((1, MLA_KV_RANK), lambda b, i, h: (0, 0)),
                  pl.BlockSpec((MLA_KV_RANK, hw), lambda b, i, h: (0, h)),
                  pl.BlockSpec((1, HEAD_W), lambda b, i, h: (0, 0)),
                  pl.BlockSpec((tm, LANES), lambda b, i, h: (i, 0)),
                  pl.BlockSpec((tm, LANES), lambda b, i, h: (i, 0))],
        out_specs=[pl.BlockSpec((None, HEADS_PER_STEP, tm, HEAD_W), lambda b, i, h: (b, h, i, 0)),
                   pl.BlockSpec((None, HEADS_PER_STEP, VT_ROWS, tm), lambda b, i, h: (b, h, 0, i))],
        out_shape=[jax.ShapeDtypeStruct((bsz, MLA_HEADS, l, HEAD_W), BF16),
                   jax.ShapeDtypeStruct((bsz, MLA_HEADS, VT_ROWS, l), BF16)],
        scratch_shapes=[pltpu.VMEM((tm, MLA_KV_RANK), BF16), pltpu.VMEM((tm, KR_W), BF16)],
        compiler_params=_params(("arbitrary", "arbitrary", "arbitrary")), name="kv_proj",
    )(p, p, gn.reshape(1, MLA_KV_RANK), w, hg, cos, sin)


def _scores_t(k, q):
    return lax.dot_general(k, q, (((1,), (1,)), ((), ())), preferred_element_type=F32)


def _attn_body(*refs, tk, n_lat):
    if n_lat:
        q_ref, kc_ref, vc_ref, k_ref, v_ref, o_ref, m_scr, acc_scr, s_scr = refs
    else:
        q_ref, kc_ref, vc_ref, o_ref = refs

    sc = _scores_t(kc_ref[...], q_ref[...])
    m0 = jnp.max(sc, axis=0, keepdims=True)
    acc = jnp.dot(vc_ref[...], jnp.exp2(sc - m0).astype(BF16), preferred_element_type=F32)

    if n_lat:
        m_scr[...] = m0
        acc_scr[...] = acc

        def qk(slot, off):
            s_scr[slot] = _scores_t(k_ref[pl.ds(off, tk), :], q_ref[...])

        def consume(slot, off):
            s = s_scr[slot]
            m = m_scr[...]
            m_new = jnp.maximum(m, jnp.max(s, axis=0, keepdims=True))
            p = jnp.exp2(s - m_new).astype(BF16)
            m_scr[...] = m_new
            acc_scr[...] = (jnp.exp2(m - m_new) * acc_scr[...]
                            + jnp.dot(v_ref[:, pl.ds(off, tk)], p, preferred_element_type=F32))

        qk(0, 0)

        def pair(jj, carry):
            off = pl.multiple_of(jj * (2 * tk), 2 * tk)
            qk(1, off + tk)
            consume(0, off)
            qk(0, off + 2 * tk)
            consume(1, off + tk)
            return carry

        lax.fori_loop(0, n_lat // 2 - 1, pair, 0)
        off = (n_lat - 2) * tk
        qk(1, off + tk)
        consume(0, off)
        consume(1, off + tk)
        acc = acc_scr[...]
    o_ref[...] = (acc[:MLA_V] / acc[MLA_V:MLA_V + 1]).T.astype(o_ref.dtype)


def attention(q, kc, vc, k=None, v=None):
    bsz, nh, lq, _ = q.shape
    lc = kc.shape[2]
    tq = _tile(lq, 1024)
    keys = lambda n: pl.BlockSpec((None, None, n, HEAD_W), lambda b, h, i: (b, h, 0, 0))
    vals = lambda n: pl.BlockSpec((None, None, VT_ROWS, n), lambda b, h, i: (b, h, 0, 0))
    in_specs = [pl.BlockSpec((None, None, tq, HEAD_W), lambda b, h, i: (b, h, i, 0)), keys(lc), vals(lc)]
    args = [q, kc, vc]
    tk = n_lat = 0
    scratch = []
    if k is not None:
        lk = k.shape[2]
        tk = _tile(lk // 2, 1024)
        assert tk % LANES == 0 and lk % (2 * tk) == 0
        n_lat = lk // tk
        in_specs += [keys(lk), vals(lk)]
        args += [k, v]
        scratch = [pltpu.VMEM((1, tq), F32), pltpu.VMEM((VT_ROWS, tq), F32), pltpu.VMEM((2, tk, tq), F32)]
    return pl.pallas_call(
        functools.partial(_attn_body, tk=tk, n_lat=n_lat),
        grid=(bsz, nh, lq // tq), in_specs=in_specs,
        out_specs=pl.BlockSpec((None, tq, MLA_V), lambda b, h, i: (b, i, h)),
        out_shape=jax.ShapeDtypeStruct((bsz, lq, nh * MLA_V), BF16),
        scratch_shapes=scratch,
        compiler_params=_params(("arbitrary", "arbitrary", "arbitrary")), name="attention",
    )(*args)


def _row_copy(src_hbm, row, buf, slot, r, sem):
    return pltpu.make_async_copy(src_hbm.at[pl.ds(row, 1), :], buf.at[slot, pl.ds(r, 1), :], sem.at[slot])


def _moe_body(be_ref, nu_ref, tok_ref, h_hbm, w13_ref, w2_ref, g_ref, o_ref, xbuf, sem):
    j = pl.program_id(0)
    nu = nu_ref[0]
    slot = j % 2

    def rows(blk, slot, start):
        base = blk * MOE_BLOCK

        def body(r, carry):
            cp = _row_copy(h_hbm, tok_ref[base + r], xbuf, slot, r, sem)
            if start:
                cp.start()
            else:
                cp.wait()
            return carry

        lax.fori_loop(0, MOE_BLOCK, body, 0, unroll=8)

    @pl.when((j == 0) & (nu > 0))
    def _():
        rows(0, 0, True)

    @pl.when(j + 1 < nu)
    def _():
        rows(j + 1, 1 - slot, True)

    @pl.when(j < nu)
    def _():
        rows(j, slot, False)
        x_lo, x_hi = _unpack_halves(xbuf[slot])
        half = w13_ref.shape[0] // 2
        hid = (jnp.dot(x_lo.astype(BF16), w13_ref[:half, :].astype(BF16), preferred_element_type=F32)
               + jnp.dot(x_hi.astype(BF16), w13_ref[half:, :].astype(BF16), preferred_element_type=F32))
        act = (_silu(hid[:, :D_EXPERT]) * hid[:, D_EXPERT:]).astype(BF16)
        out = jnp.dot(act, w2_ref[...].astype(BF16), preferred_element_type=F32)
        o_ref[...] = _pack_halves(out * g_ref[...])

    @pl.when(j >= nu)
    def _():
        o_ref[...] = jnp.zeros_like(o_ref)


def moe_experts(h, row_tok, gate, blk_exp, n_used, w13, w2, layer):
    m, dp = h.shape
    d = 2 * dp
    p = row_tok.shape[0]
    nb = p // MOE_BLOCK
    grid_spec = pltpu.PrefetchScalarGridSpec(
        num_scalar_prefetch=3, grid=(nb,),
        in_specs=[pl.BlockSpec(memory_space=pl.ANY),
                  pl.BlockSpec((None, None, d, 2 * D_EXPERT), lambda j, be, nu, tok: (layer, be[j], 0, 0)),
                  pl.BlockSpec((None, None, D_EXPERT, d), lambda j, be, nu, tok: (layer, be[j], 0, 0)),
                  pl.BlockSpec((MOE_BLOCK, 1), lambda j, be, nu, tok: (j, 0))],
        out_specs=pl.BlockSpec((MOE_BLOCK, dp), lambda j, be, nu, tok: (j, 0)),
        scratch_shapes=[pltpu.VMEM((2, MOE_BLOCK, dp), jnp.int32), pltpu.SemaphoreType.DMA((2,))])
    return pl.pallas_call(
        _moe_body, grid_spec=grid_spec,
        out_shape=jax.ShapeDtypeStruct((p, dp), jnp.int32),
        compiler_params=_params(("arbitrary",)), name="moe_experts",
    )(blk_exp, n_used, row_tok, h, w13, w2, gate)


def _combine_body(dest_ref, out_hbm, x_ref, g_ref, o_ref, buf_a, buf_b, sem_a, sem_b, *, tok_off, t, n_tiles):
    f = pl.program_id(0) * pl.num_programs(1) + pl.program_id(1)
    slot = f % 2

    def rows(tile, slot, start):
        base = (tok_off + tile * t) * TOP_K

        def body(r, carry):
            ca = _row_copy(out_hbm, dest_ref[base + TOP_K * r], buf_a, slot, r, sem_a)
            cb = _row_copy(out_hbm, dest_ref[base + TOP_K * r + 1], buf_b, slot, r, sem_b)
            if start:
                ca.start()
                cb.start()
            else:
                ca.wait()
                cb.wait()
            return carry

        lax.fori_loop(0, t, body, 0, unroll=8)

    @pl.when(f == 0)
    def _():
        rows(0, 0, True)

    @pl.when(f + 1 < n_tiles)
    def _():
        rows(f + 1, 1 - slot, True)

    rows(f, slot, False)
    a_lo, a_hi = _unpack_halves(buf_a[slot])
    b_lo, b_hi = _unpack_halves(buf_b[slot])
    half = x_ref.shape[1] // 2
    o_ref[:, :half] = x_ref[:, :half] + g_ref[:, :half] * (a_lo + b_lo)
    o_ref[:, half:] = x_ref[:, half:] + g_ref[:, half:] * (a_hi + b_hi)


def moe_combine(x, gate, out, dest, tok_off):
    bx, lx, d = x.shape
    t = _tile(lx, 128)
    nt = lx // t
    grid_spec = pltpu.PrefetchScalarGridSpec(
        num_scalar_prefetch=1, grid=(bx, nt),
        in_specs=[pl.BlockSpec(memory_space=pl.ANY),
                  pl.BlockSpec((None, t, d), lambda b, i, dst: (b, i, 0)),
                  pl.BlockSpec((None, 1, d), lambda b, i, dst: (b, 0, 0))],
        out_specs=pl.BlockSpec((None, t, d), lambda b, i, dst: (b, i, 0)),
        scratch_shapes=[pltpu.VMEM((2, t, d // 2), jnp.int32), pltpu.VMEM((2, t, d // 2), jnp.int32),
                        pltpu.SemaphoreType.DMA((2,)), pltpu.SemaphoreType.DMA((2,))])
    return pl.pallas_call(
        functools.partial(_combine_body, tok_off=tok_off, t=t, n_tiles=bx * nt), grid_spec=grid_spec,
        out_shape=jax.ShapeDtypeStruct((bx, lx, d), F32),
        compiler_params=_params(("arbitrary", "arbitrary")), name="moe_combine",
    )(dest, out, x, gate)


def moe_route(h, logits, w13, w2, layer):
    m = h.shape[0]
    g_logit = logits[:, :N_GROUPS]
    e_logit = logits[:, N_GROUPS:N_GROUPS + N_EXPERTS].reshape(m, N_GROUPS, EXPERTS_PER_GROUP)
    g_idx = jnp.argmax(g_logit, axis=-1)
    p_grp = jnp.take_along_axis(jax.nn.softmax(g_logit, axis=-1), g_idx[:, None], axis=1)
    e_sel = jnp.take_along_axis(e_logit, g_idx[:, None, None], axis=1)[:, 0]
    top_v, top_i = lax.top_k(e_sel, TOP_K)
    gate = (p_grp * jax.nn.softmax(top_v, axis=-1)).reshape(-1)
    e_flat = (g_idx[:, None] * EXPERTS_PER_GROUP + top_i).reshape(-1).astype(jnp.int32)

    a = m * TOP_K
    nb = -(-a // MOE_BLOCK) + N_EXPERTS
    order = jnp.argsort(e_flat).astype(jnp.int32)
    rank = jnp.argsort(order).astype(jnp.int32)
    e_s = e_flat[order]
    experts = jnp.arange(N_EXPERTS, dtype=jnp.int32)
    start = jnp.searchsorted(e_s, experts, side='left').astype(jnp.int32)
    counts = jnp.searchsorted(e_s, experts, side='right').astype(jnp.int32) - start
    padded = -(-counts // MOE_BLOCK) * MOE_BLOCK
    pad_end = jnp.cumsum(padded)
    pad_start = pad_end - padded
    blk_ids = jnp.arange(nb, dtype=jnp.int32)
    blk_raw = jnp.searchsorted(pad_end, blk_ids * MOE_BLOCK, side='right')
    blk_exp = jnp.minimum(blk_raw, N_EXPERTS - 1).astype(jnp.int32)
    n_used = (pad_end[-1] // MOE_BLOCK).astype(jnp.int32).reshape(1)
    r = (blk_ids * MOE_BLOCK - pad_start[blk_exp])[:, None] + jnp.arange(MOE_BLOCK, dtype=jnp.int32)[None, :]
    valid = ((r < counts[blk_exp][:, None]) & (blk_raw < N_EXPERTS)[:, None]).reshape(-1)
    src = order[jnp.where(valid, (start[blk_exp][:, None] + r).reshape(-1), 0)]
    row_tok = jnp.where(valid, src // TOP_K, 0).astype(jnp.int32)
    row_gate = jnp.where(valid, gate[src], 0.0)
    dest = (pad_start[e_flat] + rank - start[e_flat]).astype(jnp.int32)

    out = moe_experts(h, row_tok, row_gate[:, None], blk_exp, n_used, w13, w2, layer)
    return out, dest


def _prep_w_in(w):
    sizes = (512, 512, 1024, 1024, 32, 768, 512, 64, 1024, 1024)
    offs = [0]
    for s in sizes:
        offs.append(offs[-1] + s)
    q, k, v, g, al, qdn, kvdn, kr, xb, gb = [w[:, offs[i]:offs[i + 1]] for i in range(10)]
    zpad = lambda t, n: jnp.pad(t, ((0, 0), (0, n - t.shape[1])))
    return jnp.concatenate([qdn, zpad(al, AL_W), zpad(kr, KR_W), v, g, xb, gb, q, k, kvdn],
                           axis=1).astype(BF16)


def _prep_w_uq(w):
    w = w.reshape(MLA_Q_RANK, MLA_HEADS, MLA_NOPE + MLA_ROPE)
    w = jnp.pad(w, ((0, 0), (0, 0), (0, HEAD_W - MLA_NOPE - MLA_ROPE)))
    return w.reshape(MLA_Q_RANK, MLA_HEADS * HEAD_W).astype(BF16)


def _head_gain(g):
    return jnp.pad(g, (0, HEAD_W - g.shape[0])).reshape(1, HEAD_W)


def _rope_tables(rows):
    n = MLA_ROPE // 4
    inv = ROPE_BASE ** (-jnp.arange(n, dtype=F32) / n)
    r = jnp.repeat(jnp.arange(rows, dtype=F32), GRID_W)
    c = jnp.tile(jnp.arange(GRID_W, dtype=F32), rows)
    ar, ac = r[:, None] * inv, c[:, None] * inv
    l = ar.shape[0]
    cos = jnp.concatenate([jnp.cos(ar), jnp.cos(ar), jnp.cos(ac), jnp.cos(ac),
                           jnp.ones((l, LANES - MLA_ROPE), F32)], axis=1)
    sin = jnp.concatenate([-jnp.sin(ar), jnp.sin(ar), -jnp.sin(ac), jnp.sin(ac),
                           jnp.zeros((l, LANES - MLA_ROPE), F32)], axis=1)
    return cos, sin


def _gla_gate_w(w_a2, b_a, d):
    hq = GLA_HEADS * GLA_DK
    w = jnp.zeros((AL_W, hq), F32).at[d * GLA_GATE_RANK:(d + 1) * GLA_GATE_RANK].set(w_a2[d])
    return w.astype(BF16), b_a[d].reshape(1, hq)


def _lru_gate_w(w_gate, d):
    return jnp.concatenate([w_gate[d, 0], w_gate[d, 1]], axis=-1).astype(BF16)


def _layer(x, xc, mods, rope, prm, moe_w13, moe_w2, layer, last):
    (norm1_g, norm2_g, w_in, gla_w_a2, gla_b_a, gla_norm_g,
     mla_q_norm_g, mla_kv_norm_g, mla_w_uq, mla_w_ukv, mla_q_head_g, mla_k_head_g,
     lru_conv_w, lru_conv_b, lru_w_gate, lru_b_gate, lru_lambda, w_out,
     moe_w_grp, moe_b_grp, moe_w_exp, moe_b_exp) = prm
    bsz, l, d = x.shape
    lc = xc.shape[1]
    ctx_out = not last
    lat = lambda k: mods[:bsz, k].reshape(bsz, 1, d)
    ctx = lambda k: jnp.broadcast_to(mods[bsz, k].reshape(1, 1, d), (bsz, 1, d))

    w_in_p = _prep_w_in(w_in)
    h = norm_mod(x, norm1_g, lat(0), lat(1))
    hc = norm_mod(xc, norm1_g, ctx(0), ctx(1))
    p = matmul(h.reshape(bsz * l, d), w_in_p).reshape(bsz, l, D_INP)
    pc = matmul(hc.reshape(bsz * lc, d), w_in_p).reshape(bsz, lc, D_INP)

    waf, baf = _gla_gate_w(gla_w_a2, gla_b_a, 0)
    wab, bab = _gla_gate_w(gla_w_a2, gla_b_a, 1)
    s_zero = jnp.zeros((bsz, GLA_HEADS, GLA_DK, GLA_DV), F32)
    ocf, s_f = gla_pass(pc, waf, baf, s_zero, False)
    of, _ = gla_pass(p, waf, baf, s_f, False)
    if ctx_out:
        yac, s_b = gla_pass(pc, wab, bab, s_zero, True, ocf, gla_norm_g)
    else:
        _, s_b = gla_pass(pc, wab, bab, s_zero, True)
    ya, _ = gla_pass(p, wab, bab, s_b, True, of, gla_norm_g)

    wgf, wgb = _lru_gate_w(lru_w_gate, 0), _lru_gate_w(lru_w_gate, 1)
    h_zero = jnp.zeros((bsz, 1, D_LRU), F32)
    lru = functools.partial(lru_pass, cw=lru_conv_w, cb=lru_conv_b)
    hcf, h0f = lru(pc, wg=wgf, bg=lru_b_gate[0], lam=lru_lambda[0], h0=h_zero, reverse=False)
    hf, _ = lru(p, wg=wgf, bg=lru_b_gate[0], lam=lru_lambda[0], h0=h0f, reverse=False)
    if ctx_out:
        ylc, h0b = lru(pc, wg=wgb, bg=lru_b_gate[1], lam=lru_lambda[1], h0=h_zero, reverse=True, h_fwd=hcf)
    else:
        _, h0b = lru(pc, wg=wgb, bg=lru_b_gate[1], lam=lru_lambda[1], h0=h_zero, reverse=True)
    yl, _ = lru(p, wg=wgb, bg=lru_b_gate[1], lam=lru_lambda[1], h0=h0b, reverse=True, h_fwd=hf)

    cos, sin = rope
    cos_c = jnp.ones((lc, LANES), F32)
    sin_c = jnp.zeros((lc, LANES), F32)
    w_uq = _prep_w_uq(mla_w_uq)
    w_ukv = mla_w_ukv.astype(BF16)
    qg, kg = _head_gain(mla_q_head_g), _head_gain(mla_k_head_g)
    kc, vc = kv_proj(pc, mla_kv_norm_g, w_ukv, kg, cos_c, sin_c)
    kl, vl = kv_proj(p, mla_kv_norm_g, w_ukv, kg, cos, sin)
    ql = q_proj(p, mla_q_norm_g, w_uq, qg, cos, sin)
    yb = attention(ql, kc, vc, kl, vl)

    w_out_b = w_out.astype(BF16)
    x = out_proj(ya, yb, yl, w_out_b, x, lat(2))
    if ctx_out:
        qc = q_proj(pc, mla_q_norm_g, w_uq, qg, cos_c, sin_c)
        ybc = attention(qc, kc, vc)
        xc = out_proj(yac, ybc, ylc, w_out_b, xc, ctx(2))

    nr = LANES
    wr = jnp.pad(jnp.concatenate([moe_w_grp, moe_w_exp], axis=1), ((0, 0), (0, nr - N_GROUPS - N_EXPERTS)))
    br = jnp.pad(jnp.concatenate([moe_b_grp, moe_b_exp]), (0, nr - N_GROUPS - N_EXPERTS)).reshape(1, nr)
    ctx1 = lambda k: mods[bsz, k].reshape(1, 1, d)
    h2, lg = norm_router(x, norm2_g, lat(3), lat(4), wr, br,
                         ctx=(xc, ctx1(3), ctx1(4)) if ctx_out else None)
    out, dest = moe_route(h2, lg, moe_w13, moe_w2, layer)
    if ctx_out:
        xc = moe_combine(xc, ctx(5), out, dest, 0)
        x = moe_combine(x, lat(5), out, dest, bsz * lc)
    else:
        x = moe_combine(x, lat(5), out, dest, 0)
    return x, xc


def kernel(x, c, ctx, c_ctx, w_mod, b_mod, norm1_g, norm2_g, w_in, gla_w_a2, gla_b_a, gla_norm_g,
           mla_q_norm_g, mla_kv_norm_g, mla_w_uq, mla_w_ukv, mla_q_head_g, mla_k_head_g,
           lru_conv_w, lru_conv_b, lru_w_gate, lru_b_gate, lru_lambda, w_out,
           moe_w_grp, moe_b_grp, moe_w_exp, moe_b_exp, moe_w13, moe_w2):
    bsz, l, d = x.shape
    depth = w_mod.shape[0]
    assert bsz + 1 <= SUBLANES and l % GRID_W == 0
    cond = jnp.zeros((SUBLANES, d), F32).at[:bsz].set(c).at[bsz].set(c_ctx)
    mods = adaln_all(cond, w_mod, b_mod).reshape(depth, SUBLANES, 6, d)
    rope = _rope_tables(l // GRID_W)
    per_layer = (norm1_g, norm2_g, w_in, gla_w_a2, gla_b_a, gla_norm_g,
                 mla_q_norm_g, mla_kv_norm_g, mla_w_uq, mla_w_ukv, mla_q_head_g, mla_k_head_g,
                 lru_conv_w, lru_conv_b, lru_w_gate, lru_b_gate, lru_lambda, w_out,
                 moe_w_grp, moe_b_grp, moe_w_exp, moe_b_exp)
    xc = ctx
    for i in range(depth):
        x, xc = _layer(x, xc, mods[i], rope, tuple(t[i] for t in per_layer), moe_w13, moe_w2, i,
                       last=(i == depth - 1))
    return x
```

```python
import functools

import jax
import jax.numpy as jnp
from jax import lax
from jax.experimental import pallas as pl
from jax.experimental.pallas import tpu as pltpu

F32 = jnp.float32
BF16 = jnp.bfloat16
HI = lax.Precision.HIGHEST

D_MODEL = 4096
GRID_W = 64
NORM_EPS = 1e-6
D_GLA = D_MODEL // 4
D_MLA = D_MODEL // 2
D_LRU = D_MODEL // 4
GLA_HEADS = 4
GLA_DK = 128
GLA_DV = 256
GLA_GATE_RANK = 16
GLA_TAU = 16.0
GLA_CHUNK = 64
GLA_QSCALE = GLA_DK ** -0.5
MLA_HEADS = 16
MLA_NOPE = 128
MLA_ROPE = 64
MLA_V = 128
MLA_Q_RANK = 768
MLA_KV_RANK = 512
MLA_SCALE = (MLA_NOPE + MLA_ROPE) ** -0.5
LOG2E = 1.4426950408889634
ROPE_BASE = 10000.0
LRU_BLOCKS = 8
LRU_BW = 128
LRU_CONV = 4
LRU_C = 8.0
N_GROUPS = 8
EXPERTS_PER_GROUP = 8
N_EXPERTS = 64
TOP_K = 2
D_EXPERT = 256

LANES = 128
SUBLANES = 8
VMEM_BYTES = 64 << 20
VMEM_LIMIT = VMEM_BYTES - (8 << 20)
MOE_BLOCK = 256
MOE_SLOTS = 3

QDN_W, AL_W, KR_W = MLA_Q_RANK, LANES, LANES
COL_QDN = 0
COL_AL = 768
COL_KR = 896
COL_V = 1024
COL_G = 2048
COL_XB = 3072
COL_GB = 4096
COL_Q = 5120
COL_K = 5632
COL_KVDN = 6144
D_INP = 6656
HEAD_W = 256
VT_ROWS = MLA_V + 16
HEADS_PER_STEP = 4


def _params(sem, vmem=VMEM_LIMIT):
    return pltpu.CompilerParams(dimension_semantics=sem, vmem_limit_bytes=vmem)


def _tile(n, pref):
    t = min(n, pref)
    while n % t or t % SUBLANES:
        t -= 1
    return t


def _silu(x):
    return x * jax.nn.sigmoid(x)


def _softplus(x):
    return jnp.maximum(x, 0.0) + jnp.log1p(jnp.exp(-jnp.abs(x)))


def _pack_halves(x):
    w = x.shape[1] // 2
    bits = lax.bitcast_convert_type(x.astype(BF16).astype(F32), jnp.int32)
    return bits[:, w:] | lax.shift_right_logical(bits[:, :w], jnp.int32(16))


def _unpack_halves(p):
    lo = lax.bitcast_convert_type(lax.shift_left(p, jnp.int32(16)), F32)
    hi = lax.bitcast_convert_type(p & jnp.int32(-65536), F32)
    return lo, hi


def _adaln_body(c_ref, w_ref, b_ref, o_ref):
    s = _silu(c_ref[...])
    o_ref[...] = jnp.dot(s, w_ref[...], preferred_element_type=F32, precision=HI) + b_ref[...]


def adaln_all(cond, w_mod, b_mod):
    depth, d, n = w_mod.shape
    tn = 1024
    return pl.pallas_call(
        _adaln_body,
        grid=(depth, n // tn),
        in_specs=[pl.BlockSpec((SUBLANES, d), lambda l, j: (0, 0)),
                  pl.BlockSpec((None, d, tn), lambda l, j: (l, 0, j)),
                  pl.BlockSpec((None, 1, tn), lambda l, j: (l, 0, j))],
        out_specs=pl.BlockSpec((None, SUBLANES, tn), lambda l, j: (l, 0, j)),
        out_shape=jax.ShapeDtypeStruct((depth, SUBLANES, n), F32),
        compiler_params=_params(("arbitrary", "arbitrary")),
        name="adaln",
    )(cond, w_mod, b_mod.reshape(depth, 1, n))


def _norm_mod(x, g, shift, scale):
    y = x * lax.rsqrt(jnp.mean(x * x, axis=-1, keepdims=True) + NORM_EPS) * g
    return y * (1.0 + scale) + shift


def _norm_mod_body(x_ref, g_ref, sh_ref, sc_ref, o_ref):
    o_ref[...] = _norm_mod(x_ref[...], g_ref[...], sh_ref[...], sc_ref[...]).astype(o_ref.dtype)


def _norm_router_body(*refs, n_ctx):
    if n_ctx:
        (xc_ref, shc_ref, scc_ref, x_ref, sh_ref, sc_ref, g_ref, wr_ref, br_ref, o_ref, lg_ref) = refs
        is_ctx = pl.program_id(0) < n_ctx
        x = jnp.where(is_ctx, xc_ref[...], x_ref[...])
        sh = jnp.where(is_ctx, shc_ref[...], sh_ref[...])
        sc = jnp.where(is_ctx, scc_ref[...], sc_ref[...])
    else:
        x_ref, sh_ref, sc_ref, g_ref, wr_ref, br_ref, o_ref, lg_ref = refs
        x, sh, sc = x_ref[...], sh_ref[...], sc_ref[...]
    h = _norm_mod(x, g_ref[...], sh, sc)
    o_ref[...] = _pack_halves(h)
    lg_ref[...] = jnp.dot(h, wr_ref[...], preferred_element_type=F32, precision=HI) + br_ref[...]


def norm_mod(x, g, shift, scale):
    bx, lx, d = x.shape
    tm = _tile(lx, 256)
    row = pl.BlockSpec((None, tm, d), lambda b, i: (b, i, 0))
    vec = pl.BlockSpec((1, d), lambda b, i: (0, 0))
    mod = pl.BlockSpec((None, 1, d), lambda b, i: (b, 0, 0))
    return pl.pallas_call(
        _norm_mod_body, grid=(bx, lx // tm),
        in_specs=[row, vec, mod, mod], out_specs=row,
        out_shape=jax.ShapeDtypeStruct((bx, lx, d), BF16),
        compiler_params=_params(("arbitrary", "arbitrary")), name="norm_mod",
    )(x, g.reshape(1, d), shift, scale)


def norm_router(x, g, shift, scale, wr, br, ctx=None):
    bsz, l, d = x.shape
    nr = wr.shape[1]
    lc = ctx[0].shape[1] if ctx is not None else 0
    tm = _tile(l, 256) if ctx is None else _tile(lc, 256)
    assert l % tm == 0
    n_ctx = bsz * lc // tm
    n_lat = bsz * l // tm
    per_b = l // tm
    lat = lambda f: jnp.maximum(f - n_ctx, 0)
    vec = lambda w: pl.BlockSpec((1, w), lambda f: (0, 0))
    in_specs, args = [], []
    if ctx is not None:
        xc, shc, scc = ctx
        one = pl.BlockSpec((None, 1, d), lambda f: (0, 0, 0))
        in_specs += [pl.BlockSpec((tm, d), lambda f: (jnp.minimum(f, n_ctx - 1), 0)), one, one]
        args += [xc.reshape(bsz * lc, d), shc, scc]
    mod = pl.BlockSpec((None, 1, d), lambda f: (lat(f) // per_b, 0, 0))
    in_specs += [pl.BlockSpec((tm, d), lambda f: (lat(f), 0)), mod, mod, vec(d),
                 pl.BlockSpec((d, nr), lambda f: (0, 0)), vec(nr)]
    args += [x.reshape(bsz * l, d), shift, scale, g.reshape(1, d), wr, br]
    total = bsz * (lc + l)
    return pl.pallas_call(
        functools.partial(_norm_router_body, n_ctx=n_ctx), grid=(n_ctx + n_lat,), in_specs=in_specs,
        out_specs=[pl.BlockSpec((tm, d // 2), lambda f: (f, 0)), pl.BlockSpec((tm, nr), lambda f: (f, 0))],
        out_shape=[jax.ShapeDtypeStruct((total, d // 2), jnp.int32), jax.ShapeDtypeStruct((total, nr), F32)],
        compiler_params=_params(("arbitrary",)), name="norm_router",
    )(*args)


def _mm_body(a_ref, w_ref, o_ref):
    o_ref[...] = jnp.dot(a_ref[...], w_ref[...], preferred_element_type=F32).astype(o_ref.dtype)


def matmul(a, w, out_dtype=F32, tm_pref=1024, tn=512):
    m, k = a.shape
    n = w.shape[1]
    tm = _tile(m, tm_pref)
    return pl.pallas_call(
        _mm_body, grid=(m // tm, n // tn),
        in_specs=[pl.BlockSpec((tm, k), lambda i, j: (i, 0)),
                  pl.BlockSpec((k, tn), lambda i, j: (0, j))],
        out_specs=pl.BlockSpec((tm, tn), lambda i, j: (i, j)),
        out_shape=jax.ShapeDtypeStruct((m, n), out_dtype),
        compiler_params=_params(("arbitrary", "arbitrary")), name="matmul",
    )(a, w)


def _wout_body(ya_ref, yb_ref, yl_ref, w_ref, x_ref, g_ref, o_ref):
    acc = jnp.dot(ya_ref[...], w_ref[0:D_GLA, :], preferred_element_type=F32)
    acc += jnp.dot(yb_ref[...], w_ref[D_GLA:D_GLA + D_MLA, :], preferred_element_type=F32)
    acc += jnp.dot(yl_ref[...], w_ref[D_GLA + D_MLA:, :], preferred_element_type=F32)
    o_ref[...] = x_ref[...] + g_ref[...] * acc


def out_proj(ya, yb, yl, w, x, gate):
    bx, lx, d = x.shape
    tm = _tile(lx, 1024)
    tn = 512
    return pl.pallas_call(
        _wout_body, grid=(bx, lx // tm, d // tn),
        in_specs=[pl.BlockSpec((None, tm, D_GLA), lambda b, i, j: (b, i, 0)),
                  pl.BlockSpec((None, tm, D_MLA), lambda b, i, j: (b, i, 0)),
                  pl.BlockSpec((None, tm, D_LRU), lambda b, i, j: (b, i, 0)),
                  pl.BlockSpec((d, tn), lambda b, i, j: (0, j)),
                  pl.BlockSpec((None, tm, tn), lambda b, i, j: (b, i, j)),
                  pl.BlockSpec((None, 1, tn), lambda b, i, j: (b, 0, j))],
        out_specs=pl.BlockSpec((None, tm, tn), lambda b, i, j: (b, i, j)),
        out_shape=jax.ShapeDtypeStruct((bx, lx, d), F32),
        compiler_params=_params(("arbitrary", "arbitrary", "arbitrary")), name="out_proj",
    )(ya, yb, yl, w, x, gate)


def _gla_body(*refs, reverse, final, nchunk):
    if final:
        (q_ref, k_ref, v_ref, al_ref, wa_ref, ba_ref, s0_ref, of_ref, g_ref, ng_ref,
         o_ref, sfin_ref, s_scr) = refs
    else:
        q_ref, k_ref, v_ref, al_ref, wa_ref, ba_ref, s0_ref, o_ref, sfin_ref, s_scr = refs
    C = GLA_CHUNK

    @pl.when(pl.program_id(1) == 0)
    def _():
        s_scr[...] = s0_ref[...]

    r_i = lax.broadcasted_iota(jnp.int32, (C, C), 0)
    c_i = lax.broadcasted_iota(jnp.int32, (C, C), 1)
    keep = (c_i >= r_i) if reverse else (c_i <= r_i)
    tri = jnp.where(keep, 1.0, 0.0).astype(BF16)
    ref_row = C - 1 - C // 2 if reverse else C // 2
    last_row = 0 if reverse else C - 1

    order = range(nchunk - 1, -1, -1) if reverse else range(nchunk)
    for ci in order:
        rows = slice(ci * C, (ci + 1) * C)
        z = jnp.dot(al_ref[rows, :].astype(BF16), wa_ref[...], preferred_element_type=F32) + ba_ref[...]
        logd = (jnp.minimum(z, 0.0) - jnp.log1p(jnp.exp(-jnp.abs(z)))) * (1.0 / GLA_TAU)
        l_hi = logd.astype(BF16)
        rem = logd - l_hi.astype(F32)
        l_mid = rem.astype(BF16)
        l_lo = (rem - l_mid.astype(F32)).astype(BF16)
        b = (jnp.dot(tri, l_hi, preferred_element_type=F32) + jnp.dot(tri, l_mid, preferred_element_type=F32)
             + jnp.dot(tri, l_lo, preferred_element_type=F32))
        b_ref = b[ref_row:ref_row + 1, :]
        b_last = b[last_row:last_row + 1, :]
        q = q_ref[rows, :] * GLA_QSCALE
        k = k_ref[rows, :]
        qe = (q * jnp.exp(b - b_ref)).astype(BF16)
        ke = (k * jnp.exp(b_ref - b)).astype(BF16)
        qin = (q * jnp.exp(b)).astype(BF16)
        kst = k * jnp.exp(b_last - b)
        for h in range(GLA_HEADS):
            hs = slice(h * GLA_DK, (h + 1) * GLA_DK)
            vs = slice(h * GLA_DV, (h + 1) * GLA_DV)
            v = v_ref[rows, vs].astype(BF16)
            att = lax.dot_general(qe[:, hs], ke[:, hs], (((1,), (1,)), ((), ())),
                                  preferred_element_type=F32)
            att = jnp.where(keep, att, 0.0).astype(BF16)
            s = s_scr[h]
            o = (jnp.dot(att, v, preferred_element_type=F32)
                 + jnp.dot(qin[:, hs], s.astype(BF16), preferred_element_type=F32))
            d_col = jnp.exp(jnp.broadcast_to(b_last[:, hs], (GLA_DK, GLA_DK))).T
            s_scr[h] = (jnp.concatenate([d_col, d_col], axis=1) * s
                        + jnp.dot(kst[:, hs].T.astype(BF16), v, preferred_element_type=F32))
            if final:
                o = o + of_ref[rows, vs]
                y = o * lax.rsqrt(jnp.mean(o * o, axis=-1, keepdims=True) + NORM_EPS) * ng_ref[...]
                o_ref[rows, vs] = (y * _silu(g_ref[rows, vs])).astype(o_ref.dtype)
            else:
                o_ref[rows, vs] = o
    sfin_ref[...] = s_scr[...]


def gla_pass(p, wa, ba, s0, reverse, o_fwd=None, norm_g=None):
    bsz, l, _ = p.shape
    t = _tile(l, 256)
    nt = l // t
    final = o_fwd is not None

    def rix(i):
        return nt - 1 - i if reverse else i

    def col(width, off):
        return pl.BlockSpec((None, t, width), lambda b, i: (b, rix(i), off // width))

    hq = GLA_HEADS * GLA_DK
    state = pl.BlockSpec((None, GLA_HEADS, GLA_DK, GLA_DV), lambda b, i: (b, 0, 0, 0))
    in_specs = [col(hq, COL_Q), col(hq, COL_K), col(D_GLA, COL_V), col(AL_W, COL_AL),
                pl.BlockSpec((AL_W, hq), lambda b, i: (0, 0)),
                pl.BlockSpec((1, hq), lambda b, i: (0, 0)), state]
    args = [p, p, p, p, wa, ba, s0]
    orow = pl.BlockSpec((None, t, D_GLA), lambda b, i: (b, rix(i), 0))
    if final:
        in_specs += [orow, col(D_GLA, COL_G), pl.BlockSpec((1, GLA_DV), lambda b, i: (0, 0))]
        args += [o_fwd, p, norm_g.reshape(1, GLA_DV)]
    return pl.pallas_call(
        functools.partial(_gla_body, reverse=reverse, final=final, nchunk=t // GLA_CHUNK),
        grid=(bsz, nt), in_specs=in_specs, out_specs=[orow, state],
        out_shape=[jax.ShapeDtypeStruct((bsz, l, D_GLA), BF16 if final else F32),
                   jax.ShapeDtypeStruct((bsz, GLA_HEADS, GLA_DK, GLA_DV), F32)],
        scratch_shapes=[pltpu.VMEM((GLA_HEADS, GLA_DK, GLA_DV), F32)],
        compiler_params=_params(("arbitrary", "arbitrary")), name="gla",
    )(*args)


def _lru_body(*refs, reverse, final, t, nt):
    if final:
        (x_ref, xp_ref, xn_ref, cw_ref, cb_ref, wg_ref, bg_ref, lam_ref, h0_ref, hf_ref, gb_ref,
         o_ref, hfin_ref, h_scr, hbuf) = refs
    else:
        (x_ref, xp_ref, xn_ref, cw_ref, cb_ref, wg_ref, bg_ref, lam_ref, h0_ref,
         o_ref, hfin_ref, h_scr) = refs
        hbuf = o_ref
    i = pl.program_id(1)
    ti = nt - 1 - i if reverse else i

    @pl.when(i == 0)
    def _():
        h_scr[...] = h0_ref[...]

    ngroup = t // SUBLANES
    grp = lambda v: v.reshape(v.shape[0] // SUBLANES, SUBLANES, v.shape[1])
    sub = lax.broadcasted_iota(jnp.int32, (1, SUBLANES, 1), 1)
    x3 = grp(x_ref[...])
    xp3 = grp(jnp.where(ti == 0, 0.0, xp_ref[...]))
    xn3 = grp(jnp.where(ti == nt - 1, 0.0, xn_ref[...]))

    def row_shift(k):
        rot = pltpu.roll(x3, k % SUBLANES, axis=1)
        if k > 0:
            nb_rot = jnp.concatenate([pltpu.roll(xp3, k % SUBLANES, axis=1), rot[:-1]], axis=0)
            return jnp.where(sub >= k, rot, nb_rot)
        nb_rot = jnp.concatenate([rot[1:], pltpu.roll(xn3, k % SUBLANES, axis=1)], axis=0)
        return jnp.where(sub < SUBLANES + k, rot, nb_rot)

    cw = cw_ref[...]
    xc = (cb_ref[...] + row_shift(2) * cw[0:1, :] + row_shift(1) * cw[1:2, :]
          + x3 * cw[2:3, :] + row_shift(-1) * cw[3:4, :]).reshape(t, D_LRU)

    zs = []
    for n in range(LRU_BLOCKS):
        zs.append(jnp.dot(xc[:, n * LRU_BW:(n + 1) * LRU_BW].astype(BF16), wg_ref[n],
                          preferred_element_type=F32))
    z_r = jnp.concatenate([z[:, :LRU_BW] for z in zs], axis=1) + bg_ref[0:1, :]
    z_i = jnp.concatenate([z[:, LRU_BW:] for z in zs], axis=1) + bg_ref[1:2, :]
    log_a = -LRU_C * jax.nn.sigmoid(z_r) * _softplus(-lam_ref[...])
    a = jnp.exp(log_a)
    th = jnp.tanh(log_a)
    one_minus_a2 = -2.0 * th / (1.0 - th)
    u = jnp.sqrt(one_minus_a2) * jax.nn.sigmoid(z_i) * xc

    a, u = grp(a), grp(u)
    s = 1
    while s < SUBLANES:
        if reverse:
            ok = sub < SUBLANES - s
            a_s = pltpu.roll(a, SUBLANES - s, axis=1)
            u_s = pltpu.roll(u, SUBLANES - s, axis=1)
        else:
            ok = sub >= s
            a_s = pltpu.roll(a, s, axis=1)
            u_s = pltpu.roll(u, s, axis=1)
        u = jnp.where(ok, a * u_s + u, u)
        a = jnp.where(ok, a * a_s, a)
        s *= 2
    edge = 0 if reverse else SUBLANES - 1
    h_prev = h_scr[...]
    for gi in (range(ngroup - 1, -1, -1) if reverse else range(ngroup)):
        rows = slice(gi * SUBLANES, (gi + 1) * SUBLANES)
        hg = a[gi] * h_prev + u[gi]
        h_prev = hg[edge:edge + 1, :]
        hbuf[rows, :] = hg
    h_scr[...] = h_prev
    hfin_ref[...] = h_prev
    if final:
        g = gb_ref[...]
        gelu = 0.5 * g * (1.0 + jnp.tanh(0.7978845608028654 * (g + 0.044715 * g * g * g)))
        o_ref[...] = (gelu * (hf_ref[...] + hbuf[...])).astype(o_ref.dtype)


def lru_pass(p, cw, cb, wg, bg, lam, h0, reverse, h_fwd=None):
    bsz, l, _ = p.shape
    t = _tile(l, 256)
    nt = l // t
    tb = t // SUBLANES
    nb8 = l // SUBLANES
    final = h_fwd is not None
    xoff = COL_XB // D_LRU

    def rix(i):
        return nt - 1 - i if reverse else i

    vec = lambda r: pl.BlockSpec((r, D_LRU), lambda b, i: (0, 0))
    one = pl.BlockSpec((None, 1, D_LRU), lambda b, i: (b, 0, 0))
    orow = pl.BlockSpec((None, t, D_LRU), lambda b, i: (b, rix(i), 0))
    in_specs = [pl.BlockSpec((None, t, D_LRU), lambda b, i: (b, rix(i), xoff)),
                pl.BlockSpec((None, SUBLANES, D_LRU),
                             lambda b, i: (b, jnp.maximum(rix(i) * tb - 1, 0), xoff)),
                pl.BlockSpec((None, SUBLANES, D_LRU),
                             lambda b, i: (b, jnp.minimum((rix(i) + 1) * tb, nb8 - 1), xoff)),
                vec(LRU_CONV), vec(1),
                pl.BlockSpec((LRU_BLOCKS, LRU_BW, 2 * LRU_BW), lambda b, i: (0, 0, 0)),
                vec(2), vec(1), one]
    args = [p, p, p, cw, cb.reshape(1, D_LRU), wg, bg, lam.reshape(1, D_LRU), h0]
    if final:
        in_specs += [orow, pl.BlockSpec((None, t, D_LRU), lambda b, i: (b, rix(i), COL_GB // D_LRU))]
        args += [h_fwd, p]
    return pl.pallas_call(
        functools.partial(_lru_body, reverse=reverse, final=final, t=t, nt=nt),
        grid=(bsz, nt), in_specs=in_specs, out_specs=[orow, one],
        out_shape=[jax.ShapeDtypeStruct((bsz, l, D_LRU), BF16 if final else F32),
                   jax.ShapeDtypeStruct((bsz, 1, D_LRU), F32)],
        scratch_shapes=[pltpu.VMEM((1, D_LRU), F32)] + ([pltpu.VMEM((t, D_LRU), F32)] if final else []),
        compiler_params=_params(("arbitrary", "arbitrary")), name="lru",
    )(*args)


def _rms(x, width):
    return x * lax.rsqrt(jnp.sum(x * x, axis=-1, keepdims=True) * (1.0 / width) + NORM_EPS)


def _rope(r, cos, sin):
    lane = lax.broadcasted_iota(jnp.int32, r.shape, 1)
    first = (lane % 32) < 16
    rot = jnp.where(first, pltpu.roll(r, LANES - 16, axis=1), pltpu.roll(r, 16, axis=1))
    return r * cos + rot * sin


def _qproj_body(x_ref, gn_ref, w_ref, hg_ref, cos_ref, sin_ref, o_ref, xn_scr):
    @pl.when(pl.program_id(2) == 0)
    def _():
        xn_scr[...] = (_rms(x_ref[...], MLA_Q_RANK) * gn_ref[...]).astype(BF16)

    a = jnp.dot(xn_scr[...], w_ref[...], preferred_element_type=F32)
    for hh in range(HEADS_PER_STEP):
        c0 = hh * HEAD_W
        qn = _rms(a[:, c0:c0 + MLA_NOPE], MLA_NOPE) * hg_ref[:, :MLA_NOPE]
        qr = _rms(a[:, c0 + MLA_NOPE:c0 + HEAD_W], MLA_ROPE) * hg_ref[:, MLA_NOPE:]
        qr = _rope(qr, cos_ref[...], sin_ref[...])
        o_ref[hh] = (jnp.concatenate([qn, qr], axis=1) * (MLA_SCALE * LOG2E)).astype(o_ref.dtype)


def q_proj(p, gn, w, hg, cos, sin):
    bsz, l, _ = p.shape
    tm = _tile(l, 512)
    hw = HEADS_PER_STEP * HEAD_W
    return pl.pallas_call(
        _qproj_body, grid=(bsz, l // tm, MLA_HEADS // HEADS_PER_STEP),
        in_specs=[pl.BlockSpec((None, tm, QDN_W), lambda b, i, h: (b, i, COL_QDN // QDN_W)),
                  pl.BlockSpec((1, QDN_W), lambda b, i, h: (0, 0)),
                  pl.BlockSpec((QDN_W, hw), lambda b, i, h: (0, h)),
                  pl.BlockSpec((1, HEAD_W), lambda b, i, h: (0, 0)),
                  pl.BlockSpec((tm, LANES), lambda b, i, h: (i, 0)),
                  pl.BlockSpec((tm, LANES), lambda b, i, h: (i, 0))],
        out_specs=pl.BlockSpec((None, HEADS_PER_STEP, tm, HEAD_W), lambda b, i, h: (b, h, i, 0)),
        out_shape=jax.ShapeDtypeStruct((bsz, MLA_HEADS, l, HEAD_W), BF16),
        scratch_shapes=[pltpu.VMEM((tm, QDN_W), BF16)],
        compiler_params=_params(("arbitrary", "arbitrary", "arbitrary")), name="q_proj",
    )(p, gn.reshape(1, QDN_W), w, hg, cos, sin)


def _kvproj_body(x_ref, kr_ref, gn_ref, w_ref, hg_ref, cos_ref, sin_ref, k_ref, v_ref, xn_scr, kr_scr):
    @pl.when(pl.program_id(2) == 0)
    def _():
        xn_scr[...] = (_rms(x_ref[...], MLA_KV_RANK) * gn_ref[...]).astype(BF16)
        kr = _rms(kr_ref[...], MLA_ROPE) * hg_ref[:, MLA_NOPE:]
        kr_scr[...] = _rope(kr, cos_ref[...], sin_ref[...]).astype(BF16)

    a = jnp.dot(xn_scr[...], w_ref[...], preferred_element_type=F32)
    for hh in range(HEADS_PER_STEP):
        c0 = hh * HEAD_W
        kn = _rms(a[:, c0:c0 + MLA_NOPE], MLA_NOPE) * hg_ref[:, :MLA_NOPE]
        k_ref[hh, :, :MLA_NOPE] = kn.astype(k_ref.dtype)
        k_ref[hh, :, MLA_NOPE:] = kr_scr[...]
        v_ref[hh, :MLA_V, :] = a[:, c0 + MLA_NOPE:c0 + HEAD_W].T.astype(v_ref.dtype)
        v_ref[hh, MLA_V:, :] = jnp.ones((VT_ROWS - MLA_V, a.shape[0]), v_ref.dtype)


def kv_proj(p, gn, w, hg, cos, sin):
    bsz, l, _ = p.shape
    tm = _tile(l, 512)
    hw = HEADS_PER_STEP * HEAD_W
    return pl.pallas_call(
        _kvproj_body, grid=(bsz, l // tm, MLA_HEADS // HEADS_PER_STEP),
        in_specs=[pl.BlockSpec((None, tm, MLA_KV_RANK), lambda b, i, h: (b, i, COL_KVDN // MLA_KV_RANK)),
                  pl.BlockSpec((None, tm, KR_W), lambda b, i, h: (b, i, COL_KR // KR_W)),
                  pl.BlockSpec((1, MLA_KV_RANK), lambda b, i, h: (0, 0)),
                  pl.BlockSpec((MLA_KV_RANK, hw), lambda b, i, h: (0, h)),
                  pl.BlockSpec((1, HEAD_W), lambda b, i, h: (0, 0)),
                  pl.BlockSpec((tm, LANES), lambda b, i, h: (i, 0)),
                  pl.BlockSpec((tm, LANES), lambda b, i, h: (i, 0))],
        out_specs=[pl.BlockSpec((None, HEADS_PER_STEP, tm, HEAD_W), lambda b, i, h: (b, h, i, 0)),
                   pl.BlockSpec((None, HEADS_PER_STEP, VT_ROWS, tm), lambda b, i, h: (b, h, 0, i))],
        out_shape=[jax.ShapeDtypeStruct((bsz, MLA_HEADS, l, HEAD_W), BF16),
                   jax.ShapeDtypeStruct((bsz, MLA_HEADS, VT_ROWS, l), BF16)],
        scratch_shapes=[pltpu.VMEM((tm, MLA_KV_RANK), BF16), pltpu.VMEM((tm, KR_W), BF16)],
        compiler_params=_params(("arbitrary", "arbitrary", "arbitrary")), name="kv_proj",
    )(p, p, gn.reshape(1, MLA_KV_RANK), w, hg, cos, sin)


def _scores_t(k, q):
    return lax.dot_general(k, q, (((1,), (1,)), ((), ())), preferred_element_type=F32)


def _attn_body(*refs, tk, n_lat):
    if n_lat:
        q_ref, kc_ref, vc_ref, k_ref, v_ref, o_ref, m_scr, acc_scr, s_scr = refs
    else:
        q_ref, kc_ref, vc_ref, o_ref = refs

    def qk(slot, off):
        s_scr[slot] = _scores_t(k_ref[pl.ds(off, tk), :], q_ref[...])

    if n_lat:
        qk(0, 0)

    sc = _scores_t(kc_ref[...], q_ref[...])
    m0 = jnp.max(sc, axis=0, keepdims=True)
    acc = jnp.dot(vc_ref[...], jnp.exp2(sc - m0).astype(BF16), preferred_element_type=F32)

    if n_lat:
        m_scr[...] = m0
        acc_scr[...] = acc

        def consume(slot, off):
            s = s_scr[slot]
            m = m_scr[...]
            m_new = jnp.maximum(m, jnp.max(s, axis=0, keepdims=True))
            p = jnp.exp2(s - m_new).astype(BF16)
            m_scr[...] = m_new
            acc_scr[...] = (jnp.exp2(m - m_new) * acc_scr[...]
                            + jnp.dot(v_ref[:, pl.ds(off, tk)], p, preferred_element_type=F32))

        def pair(jj, carry):
            off = pl.multiple_of(jj * (2 * tk), 2 * tk)
            qk(1, off + tk)
            consume(0, off)
            qk(0, off + 2 * tk)
            consume(1, off + tk)
            return carry

        lax.fori_loop(0, n_lat // 2 - 1, pair, 0)
        off = (n_lat - 2) * tk
        qk(1, off + tk)
        consume(0, off)
        consume(1, off + tk)
        acc = acc_scr[...]
    o_ref[...] = (acc[:MLA_V] / acc[MLA_V:MLA_V + 1]).T.astype(o_ref.dtype)


def attention(q, kc, vc, k=None, v=None):
    bsz, nh, lq, _ = q.shape
    lc = kc.shape[2]
    tq = _tile(lq, 1024)
    keys = lambda n: pl.BlockSpec((None, None, n, HEAD_W), lambda b, h, i: (b, h, 0, 0))
    vals = lambda n: pl.BlockSpec((None, None, VT_ROWS, n), lambda b, h, i: (b, h, 0, 0))
    in_specs = [pl.BlockSpec((None, None, tq, HEAD_W), lambda b, h, i: (b, h, i, 0)), keys(lc), vals(lc)]
    args = [q, kc, vc]
    tk = n_lat = 0
    scratch = []
    if k is not None:
        lk = k.shape[2]
        tk = _tile(lk // 2, 1024)
        assert tk % LANES == 0 and lk % (2 * tk) == 0
        n_lat = lk // tk
        in_specs += [keys(lk), vals(lk)]
        args += [k, v]
        scratch = [pltpu.VMEM((1, tq), F32), pltpu.VMEM((VT_ROWS, tq), F32), pltpu.VMEM((2, tk, tq), F32)]
    return pl.pallas_call(
        functools.partial(_attn_body, tk=tk, n_lat=n_lat),
        grid=(bsz, nh, lq // tq), in_specs=in_specs,
        out_specs=pl.BlockSpec((None, tq, MLA_V), lambda b, h, i: (b, i, h)),
        out_shape=jax.ShapeDtypeStruct((bsz, lq, nh * MLA_V), BF16),
        scratch_shapes=scratch,
        compiler_params=_params(("arbitrary", "arbitrary", "arbitrary")), name="attention",
    )(*args)


def _row_copy(src_hbm, row, buf, slot, r, sem):
    return pltpu.make_async_copy(src_hbm.at[pl.ds(row, 1), :], buf.at[slot, pl.ds(r, 1), :], sem.at[slot])


def _moe_body(be_ref, nu_ref, tok_ref, h_hbm, w13_ref, w2_ref, g_ref, o_ref, xbuf, sem):
    j = pl.program_id(0)
    nu = nu_ref[0]
    slot = j % MOE_SLOTS
    ahead = MOE_SLOTS - 1

    def rows(blk, slot, start):
        base = blk * MOE_BLOCK

        def body(r, carry):
            cp = _row_copy(h_hbm, tok_ref[base + r], xbuf, slot, r, sem)
            if start:
                cp.start()
            else:
                cp.wait()
            return carry

        lax.fori_loop(0, MOE_BLOCK, body, 0, unroll=8)

    for first in range(ahead):
        @pl.when((j == 0) & (first < nu))
        def _():
            rows(first, first, True)

    @pl.when(j + ahead < nu)
    def _():
        rows(j + ahead, (j + ahead) % MOE_SLOTS, True)

    @pl.when(j < nu)
    def _():
        rows(j, slot, False)
        x_lo, x_hi = _unpack_halves(xbuf[slot])
        half = w13_ref.shape[0] // 2
        hid = (jnp.dot(x_lo.astype(BF16), w13_ref[:half, :].astype(BF16), preferred_element_type=F32)
               + jnp.dot(x_hi.astype(BF16), w13_ref[half:, :].astype(BF16), preferred_element_type=F32))
        act = (_silu(hid[:, :D_EXPERT]) * hid[:, D_EXPERT:]).astype(BF16)
        out = jnp.dot(act, w2_ref[...].astype(BF16), preferred_element_type=F32)
        o_ref[...] = _pack_halves(out * g_ref[...])

    @pl.when(j >= nu)
    def _():
        o_ref[...] = jnp.zeros_like(o_ref)


def moe_experts(h, row_tok, gate, blk_exp, n_used, w13, w2, layer):
    m, dp = h.shape
    d = 2 * dp
    p = row_tok.shape[0]
    nb = p // MOE_BLOCK
    grid_spec = pltpu.PrefetchScalarGridSpec(
        num_scalar_prefetch=3, grid=(nb,),
        in_specs=[pl.BlockSpec(memory_space=pl.ANY),
                  pl.BlockSpec((None, None, d, 2 * D_EXPERT), lambda j, be, nu, tok: (layer, be[j], 0, 0)),
                  pl.BlockSpec((None, None, D_EXPERT, d), lambda j, be, nu, tok: (layer, be[j], 0, 0)),
                  pl.BlockSpec((MOE_BLOCK, 1), lambda j, be, nu, tok: (j, 0))],
        out_specs=pl.BlockSpec((MOE_BLOCK, dp), lambda j, be, nu, tok: (j, 0)),
        scratch_shapes=[pltpu.VMEM((MOE_SLOTS, MOE_BLOCK, dp), jnp.int32),
                        pltpu.SemaphoreType.DMA((MOE_SLOTS,))])
    return pl.pallas_call(
        _moe_body, grid_spec=grid_spec,
        out_shape=jax.ShapeDtypeStruct((p, dp), jnp.int32),
        compiler_params=_params(("arbitrary",)), name="moe_experts",
    )(blk_exp, n_used, row_tok, h, w13, w2, gate)


def _combine_body(dest_ref, out_hbm, x_ref, g_ref, o_ref, buf_a, buf_b, sem_a, sem_b, *, tok_off, t, n_tiles):
    f = pl.program_id(0) * pl.num_programs(1) + pl.program_id(1)
    slot = f % 2

    def rows(tile, slot, start):
        base = (tok_off + tile * t) * TOP_K

        def body(r, carry):
            ca = _row_copy(out_hbm, dest_ref[base + TOP_K * r], buf_a, slot, r, sem_a)
            cb = _row_copy(out_hbm, dest_ref[base + TOP_K * r + 1], buf_b, slot, r, sem_b)
            if start:
                ca.start()
                cb.start()
            else:
                ca.wait()
                cb.wait()
            return carry

        lax.fori_loop(0, t, body, 0, unroll=8)

    @pl.when(f == 0)
    def _():
        rows(0, 0, True)

    @pl.when(f + 1 < n_tiles)
    def _():
        rows(f + 1, 1 - slot, True)

    rows(f, slot, False)
    a_lo, a_hi = _unpack_halves(buf_a[slot])
    b_lo, b_hi = _unpack_halves(buf_b[slot])
    half = x_ref.shape[1] // 2
    o_ref[:, :half] = x_ref[:, :half] + g_ref[:, :half] * (a_lo + b_lo)
    o_ref[:, half:] = x_ref[:, half:] + g_ref[:, half:] * (a_hi + b_hi)


def moe_combine(x, gate, out, dest, tok_off):
    bx, lx, d = x.shape
    t = _tile(lx, 128)
    nt = lx // t
    grid_spec = pltpu.PrefetchScalarGridSpec(
        num_scalar_prefetch=1, grid=(bx, nt),
        in_specs=[pl.BlockSpec(memory_space=pl.ANY),
                  pl.BlockSpec((None, t, d), lambda b, i, dst: (b, i, 0)),
                  pl.BlockSpec((None, 1, d), lambda b, i, dst: (b, 0, 0))],
        out_specs=pl.BlockSpec((None, t, d), lambda b, i, dst: (b, i, 0)),
        scratch_shapes=[pltpu.VMEM((2, t, d // 2), jnp.int32), pltpu.VMEM((2, t, d // 2), jnp.int32),
                        pltpu.SemaphoreType.DMA((2,)), pltpu.SemaphoreType.DMA((2,))])
    return pl.pallas_call(
        functools.partial(_combine_body, tok_off=tok_off, t=t, n_tiles=bx * nt), grid_spec=grid_spec,
        out_shape=jax.ShapeDtypeStruct((bx, lx, d), F32),
        compiler_params=_params(("arbitrary", "arbitrary")), name="moe_combine",
    )(dest, out, x, gate)


def moe_route(h, logits, w13, w2, layer):
    m = h.shape[0]
    g_logit = logits[:, :N_GROUPS]
    e_logit = logits[:, N_GROUPS:N_GROUPS + N_EXPERTS].reshape(m, N_GROUPS, EXPERTS_PER_GROUP)
    g_idx = jnp.argmax(g_logit, axis=-1)
    p_grp = jnp.take_along_axis(jax.nn.softmax(g_logit, axis=-1), g_idx[:, None], axis=1)
    e_sel = jnp.take_along_axis(e_logit, g_idx[:, None, None], axis=1)[:, 0]
    top_v, top_i = lax.top_k(e_sel, TOP_K)
    gate = (p_grp * jax.nn.softmax(top_v, axis=-1)).reshape(-1)
    e_flat = (g_idx[:, None] * EXPERTS_PER_GROUP + top_i).reshape(-1).astype(jnp.int32)

    a = m * TOP_K
    nb = -(-a // MOE_BLOCK) + N_EXPERTS
    order = jnp.argsort(e_flat).astype(jnp.int32)
    rank = jnp.argsort(order).astype(jnp.int32)
    e_s = e_flat[order]
    experts = jnp.arange(N_EXPERTS, dtype=jnp.int32)
    start = jnp.searchsorted(e_s, experts, side='left').astype(jnp.int32)
    counts = jnp.searchsorted(e_s, experts, side='right').astype(jnp.int32) - start
    padded = -(-counts // MOE_BLOCK) * MOE_BLOCK
    pad_end = jnp.cumsum(padded)
    pad_start = pad_end - padded
    blk_ids = jnp.arange(nb, dtype=jnp.int32)
    blk_raw = jnp.searchsorted(pad_end, blk_ids * MOE_BLOCK, side='right')
    blk_exp = jnp.minimum(blk_raw, N_EXPERTS - 1).astype(jnp.int32)
    n_used = (pad_end[-1] // MOE_BLOCK).astype(jnp.int32).reshape(1)
    r = (blk_ids * MOE_BLOCK - pad_start[blk_exp])[:, None] + jnp.arange(MOE_BLOCK, dtype=jnp.int32)[None, :]
    valid = ((r < counts[blk_exp][:, None]) & (blk_raw < N_EXPERTS)[:, None]).reshape(-1)
    src = order[jnp.where(valid, (start[blk_exp][:, None] + r).reshape(-1), 0)]
    row_tok = jnp.where(valid, src // TOP_K, 0).astype(jnp.int32)
    row_gate = jnp.where(valid, gate[src], 0.0)
    dest = (pad_start[e_flat] + rank - start[e_flat]).astype(jnp.int32)

    out = moe_experts(h, row_tok, row_gate[:, None], blk_exp, n_used, w13, w2, layer)
    return out, dest


def _prep_w_in(w):
    sizes = (512, 512, 1024, 1024, 32, 768, 512, 64, 1024, 1024)
    offs = [0]
    for s in sizes:
        offs.append(offs[-1] + s)
    q, k, v, g, al, qdn, kvdn, kr, xb, gb = [w[:, offs[i]:offs[i + 1]] for i in range(10)]
    zpad = lambda t, n: jnp.pad(t, ((0, 0), (0, n - t.shape[1])))
    return jnp.concatenate([qdn, zpad(al, AL_W), zpad(kr, KR_W), v, g, xb, gb, q, k, kvdn],
                           axis=1).astype(BF16)


def _prep_w_uq(w):
    w = w.reshape(MLA_Q_RANK, MLA_HEADS, MLA_NOPE + MLA_ROPE)
    w = jnp.pad(w, ((0, 0), (0, 0), (0, HEAD_W - MLA_NOPE - MLA_ROPE)))
    return w.reshape(MLA_Q_RANK, MLA_HEADS * HEAD_W).astype(BF16)


def _head_gain(g):
    return jnp.pad(g, (0, HEAD_W - g.shape[0])).reshape(1, HEAD_W)


def _rope_tables(rows):
    n = MLA_ROPE // 4
    inv = ROPE_BASE ** (-jnp.arange(n, dtype=F32) / n)
    r = jnp.repeat(jnp.arange(rows, dtype=F32), GRID_W)
    c = jnp.tile(jnp.arange(GRID_W, dtype=F32), rows)
    ar, ac = r[:, None] * inv, c[:, None] * inv
    l = ar.shape[0]
    cos = jnp.concatenate([jnp.cos(ar), jnp.cos(ar), jnp.cos(ac), jnp.cos(ac),
                           jnp.ones((l, LANES - MLA_ROPE), F32)], axis=1)
    sin = jnp.concatenate([-jnp.sin(ar), jnp.sin(ar), -jnp.sin(ac), jnp.sin(ac),
                           jnp.zeros((l, LANES - MLA_ROPE), F32)], axis=1)
    return cos, sin


def _gla_gate_w(w_a2, b_a, d):
    hq = GLA_HEADS * GLA_DK
    w = jnp.zeros((AL_W, hq), F32).at[d * GLA_GATE_RANK:(d + 1) * GLA_GATE_RANK].set(w_a2[d])
    return w.astype(BF16), b_a[d].reshape(1, hq)


def _lru_gate_w(w_gate, d):
    return jnp.concatenate([w_gate[d, 0], w_gate[d, 1]], axis=-1).astype(BF16)


def _layer(x, xc, mods, rope, prm, moe_w13, moe_w2, layer, last):
    (norm1_g, norm2_g, w_in, gla_w_a2, gla_b_a, gla_norm_g,
     mla_q_norm_g, mla_kv_norm_g, mla_w_uq, mla_w_ukv, mla_q_head_g, mla_k_head_g,
     lru_conv_w, lru_conv_b, lru_w_gate, lru_b_gate, lru_lambda, w_out,
     moe_w_grp, moe_b_grp, moe_w_exp, moe_b_exp) = prm
    bsz, l, d = x.shape
    lc = xc.shape[1]
    ctx_out = not last
    lat = lambda k: mods[:bsz, k].reshape(bsz, 1, d)
    ctx = lambda k: jnp.broadcast_to(mods[bsz, k].reshape(1, 1, d), (bsz, 1, d))

    w_in_p = _prep_w_in(w_in)
    h = norm_mod(x, norm1_g, lat(0), lat(1))
    hc = norm_mod(xc, norm1_g, ctx(0), ctx(1))
    p = matmul(h.reshape(bsz * l, d), w_in_p).reshape(bsz, l, D_INP)
    pc = matmul(hc.reshape(bsz * lc, d), w_in_p).reshape(bsz, lc, D_INP)

    waf, baf = _gla_gate_w(gla_w_a2, gla_b_a, 0)
    wab, bab = _gla_gate_w(gla_w_a2, gla_b_a, 1)
    s_zero = jnp.zeros((bsz, GLA_HEADS, GLA_DK, GLA_DV), F32)
    ocf, s_f = gla_pass(pc, waf, baf, s_zero, False)
    of, _ = gla_pass(p, waf, baf, s_f, False)
    if ctx_out:
        yac, s_b = gla_pass(pc, wab, bab, s_zero, True, ocf, gla_norm_g)
    else:
        _, s_b = gla_pass(pc, wab, bab, s_zero, True)
    ya, _ = gla_pass(p, wab, bab, s_b, True, of, gla_norm_g)

    wgf, wgb = _lru_gate_w(lru_w_gate, 0), _lru_gate_w(lru_w_gate, 1)
    h_zero = jnp.zeros((bsz, 1, D_LRU), F32)
    lru = functools.partial(lru_pass, cw=lru_conv_w, cb=lru_conv_b)
    hcf, h0f = lru(pc, wg=wgf, bg=lru_b_gate[0], lam=lru_lambda[0], h0=h_zero, reverse=False)
    hf, _ = lru(p, wg=wgf, bg=lru_b_gate[0], lam=lru_lambda[0], h0=h0f, reverse=False)
    if ctx_out:
        ylc, h0b = lru(pc, wg=wgb, bg=lru_b_gate[1], lam=lru_lambda[1], h0=h_zero, reverse=True, h_fwd=hcf)
    else:
        _, h0b = lru(pc, wg=wgb, bg=lru_b_gate[1], lam=lru_lambda[1], h0=h_zero, reverse=True)
    yl, _ = lru(p, wg=wgb, bg=lru_b_gate[1], lam=lru_lambda[1], h0=h0b, reverse=True, h_fwd=hf)

    cos, sin = rope
    cos_c = jnp.ones((lc, LANES), F32)
    sin_c = jnp.zeros((lc, LANES), F32)
    w_uq = _prep_w_uq(mla_w_uq)
    w_ukv = mla_w_ukv.astype(BF16)
    qg, kg = _head_gain(mla_q_head_g), _head_gain(mla_k_head_g)
    kc, vc = kv_proj(pc, mla_kv_norm_g, w_ukv, kg, cos_c, sin_c)
    kl, vl = kv_proj(p, mla_kv_norm_g, w_ukv, kg, cos, sin)
    ql = q_proj(p, mla_q_norm_g, w_uq, qg, cos, sin)
    yb = attention(ql, kc, vc, kl, vl)

    w_out_b = w_out.astype(BF16)
    x = out_proj(ya, yb, yl, w_out_b, x, lat(2))
    if ctx_out:
        qc = q_proj(pc, mla_q_norm_g, w_uq, qg, cos_c, sin_c)
        ybc = attention(qc, kc, vc)
        xc = out_proj(yac, ybc, ylc, w_out_b, xc, ctx(2))

    nr = LANES
    wr = jnp.pad(jnp.concatenate([moe_w_grp, moe_w_exp], axis=1), ((0, 0), (0, nr - N_GROUPS - N_EXPERTS)))
    br = jnp.pad(jnp.concatenate([moe_b_grp, moe_b_exp]), (0, nr - N_GROUPS - N_EXPERTS)).reshape(1, nr)
    ctx1 = lambda k: mods[bsz, k].reshape(1, 1, d)
    h2, lg = norm_router(x, norm2_g, lat(3), lat(4), wr, br,
                         ctx=(xc, ctx1(3), ctx1(4)) if ctx_out else None)
    out, dest = moe_route(h2, lg, moe_w13, moe_w2, layer)
    if ctx_out:
        xc = moe_combine(xc, ctx(5), out, dest, 0)
        x = moe_combine(x, lat(5), out, dest, bsz * lc)
    else:
        x = moe_combine(x, lat(5), out, dest, 0)
    return x, xc


def kernel(x, c, ctx, c_ctx, w_mod, b_mod, norm1_g, norm2_g, w_in, gla_w_a2, gla_b_a, gla_norm_g,
           mla_q_norm_g, mla_kv_norm_g, mla_w_uq, mla_w_ukv, mla_q_head_g, mla_k_head_g,
           lru_conv_w, lru_conv_b, lru_w_gate, lru_b_gate, lru_lambda, w_out,
           moe_w_grp, moe_b_grp, moe_w_exp, moe_b_exp, moe_w13, moe_w2):
    bsz, l, d = x.shape
    depth = w_mod.shape[0]
    assert bsz + 1 <= SUBLANES and l % GRID_W == 0
    cond = jnp.zeros((SUBLANES, d), F32).at[:bsz].set(c).at[bsz].set(c_ctx)
    mods = adaln_all(cond, w_mod, b_mod).reshape(depth, SUBLANES, 6, d)
    rope = _rope_tables(l // GRID_W)
    per_layer = (norm1_g, norm2_g, w_in, gla_w_a2, gla_b_a, gla_norm_g,
                 mla_q_norm_g, mla_kv_norm_g, mla_w_uq, mla_w_ukv, mla_q_head_g, mla_k_head_g,
                 lru_conv_w, lru_conv_b, lru_w_gate, lru_b_gate, lru_lambda, w_out,
                 moe_w_grp, moe_b_grp, moe_w_exp, moe_b_exp)
    xc = ctx
    for i in range(depth):
        x, xc = _layer(x, xc, mods[i], rope, tuple(t[i] for t in per_layer), moe_w13, moe_w2, i,
                       last=(i == depth - 1))
    return x
```

```python
import functools

import jax
import jax.numpy as jnp
from jax import lax
from jax.experimental import pallas as pl
from jax.experimental.pallas import tpu as pltpu

F32 = jnp.float32
BF16 = jnp.bfloat16
HI = lax.Precision.HIGHEST

D_MODEL = 4096
GRID_W = 64
NORM_EPS = 1e-6
D_GLA = D_MODEL // 4
D_MLA = D_MODEL // 2
D_LRU = D_MODEL // 4
GLA_HEADS = 4
GLA_DK = 128
GLA_DV = 256
GLA_GATE_RANK = 16
GLA_TAU = 16.0
GLA_CHUNK = 64
GLA_QSCALE = GLA_DK ** -0.5
MLA_HEADS = 16
MLA_NOPE = 128
MLA_ROPE = 64
MLA_V = 128
MLA_Q_RANK = 768
MLA_KV_RANK = 512
MLA_SCALE = (MLA_NOPE + MLA_ROPE) ** -0.5
LOG2E = 1.4426950408889634
ROPE_BASE = 10000.0
LRU_BLOCKS = 8
LRU_BW = 128
LRU_CONV = 4
LRU_C = 8.0
N_GROUPS = 8
EXPERTS_PER_GROUP = 8
N_EXPERTS = 64
TOP_K = 2
D_EXPERT = 256

LANES = 128
SUBLANES = 8
VMEM_BYTES = 64 << 20
VMEM_LIMIT = VMEM_BYTES - (8 << 20)
MOE_BLOCK = 256
MOE_SLOTS = 3

QDN_W, AL_W, KR_W = MLA_Q_RANK, LANES, LANES
COL_QDN = 0
COL_AL = 768
COL_KR = 896
COL_V = 1024
COL_G = 2048
COL_XB = 3072
COL_GB = 4096
COL_Q = 5120
COL_K = 5632
COL_KVDN = 6144
D_INP = 6656
HEAD_W = 256
VT_ROWS = MLA_V + 16
HEADS_PER_STEP = 4


def _params(sem, vmem=VMEM_LIMIT):
    return pltpu.CompilerParams(dimension_semantics=sem, vmem_limit_bytes=vmem)


def _tile(n, pref):
    t = min(n, pref)
    while n % t or t % SUBLANES:
        t -= 1
    return t


def _silu(x):
    return x * jax.nn.sigmoid(x)


def _softplus(x):
    return jnp.maximum(x, 0.0) + jnp.log1p(jnp.exp(-jnp.abs(x)))


def _pack_halves(x):
    w = x.shape[1] // 2
    bits = lax.bitcast_convert_type(x.astype(BF16).astype(F32), jnp.int32)
    return bits[:, w:] | lax.shift_right_logical(bits[:, :w], jnp.int32(16))


def _unpack_halves(p):
    lo = lax.bitcast_convert_type(lax.shift_left(p, jnp.int32(16)), F32)
    hi = lax.bitcast_convert_type(p & jnp.int32(-65536), F32)
    return lo, hi


def _adaln_body(c_ref, w_ref, b_ref, o_ref):
    s = _silu(c_ref[...])
    o_ref[...] = jnp.dot(s, w_ref[...], preferred_element_type=F32, precision=HI) + b_ref[...]


def adaln_all(cond, w_mod, b_mod):
    depth, d, n = w_mod.shape
    tn = 1024
    return pl.pallas_call(
        _adaln_body,
        grid=(depth, n // tn),
        in_specs=[pl.BlockSpec((SUBLANES, d), lambda l, j: (0, 0)),
                  pl.BlockSpec((None, d, tn), lambda l, j: (l, 0, j)),
                  pl.BlockSpec((None, 1, tn), lambda l, j: (l, 0, j))],
        out_specs=pl.BlockSpec((None, SUBLANES, tn), lambda l, j: (l, 0, j)),
        out_shape=jax.ShapeDtypeStruct((depth, SUBLANES, n), F32),
        compiler_params=_params(("arbitrary", "arbitrary")),
        name="adaln",
    )(cond, w_mod, b_mod.reshape(depth, 1, n))


def _norm_mod(x, g, shift, scale):
    y = x * lax.rsqrt(jnp.mean(x * x, axis=-1, keepdims=True) + NORM_EPS) * g
    return y * (1.0 + scale) + shift


def _norm_mod_body(x_ref, g_ref, sh_ref, sc_ref, o_ref):
    o_ref[...] = _norm_mod(x_ref[...], g_ref[...], sh_ref[...], sc_ref[...]).astype(o_ref.dtype)


def _norm_router_body(*refs, n_ctx):
    if n_ctx:
        (xc_ref, shc_ref, scc_ref, x_ref, sh_ref, sc_ref, g_ref, wr_ref, br_ref, o_ref, lg_ref) = refs
        is_ctx = pl.program_id(0) < n_ctx
        x = jnp.where(is_ctx, xc_ref[...], x_ref[...])
        sh = jnp.where(is_ctx, shc_ref[...], sh_ref[...])
        sc = jnp.where(is_ctx, scc_ref[...], sc_ref[...])
    else:
        x_ref, sh_ref, sc_ref, g_ref, wr_ref, br_ref, o_ref, lg_ref = refs
        x, sh, sc = x_ref[...], sh_ref[...], sc_ref[...]
    h = _norm_mod(x, g_ref[...], sh, sc)
    o_ref[...] = _pack_halves(h)
    lg_ref[...] = jnp.dot(h, wr_ref[...], preferred_element_type=F32, precision=HI) + br_ref[...]


def norm_mod(x, g, shift, scale):
    bx, lx, d = x.shape
    tm = _tile(lx, 256)
    row = pl.BlockSpec((None, tm, d), lambda b, i: (b, i, 0))
    vec = pl.BlockSpec((1, d), lambda b, i: (0, 0))
    mod = pl.BlockSpec((None, 1, d), lambda b, i: (b, 0, 0))
    return pl.pallas_call(
        _norm_mod_body, grid=(bx, lx // tm),
        in_specs=[row, vec, mod, mod], out_specs=row,
        out_shape=jax.ShapeDtypeStruct((bx, lx, d), BF16),
        compiler_params=_params(("arbitrary", "arbitrary")), name="norm_mod",
    )(x, g.reshape(1, d), shift, scale)


def norm_router(x, g, shift, scale, wr, br, ctx=None):
    bsz, l, d = x.shape
    nr = wr.shape[1]
    lc = ctx[0].shape[1] if ctx is not None else 0
    tm = _tile(l, 256) if ctx is None else _tile(lc, 256)
    assert l % tm == 0
    n_ctx = bsz * lc // tm
    n_lat = bsz * l // tm
    per_b = l // tm
    lat = lambda f: jnp.maximum(f - n_ctx, 0)
    vec = lambda w: pl.BlockSpec((1, w), lambda f: (0, 0))
    in_specs, args = [], []
    if ctx is not None:
        xc, shc, scc = ctx
        one = pl.BlockSpec((None, 1, d), lambda f: (0, 0, 0))
        in_specs += [pl.BlockSpec((tm, d), lambda f: (jnp.minimum(f, n_ctx - 1), 0)), one, one]
        args += [xc.reshape(bsz * lc, d), shc, scc]
    mod = pl.BlockSpec((None, 1, d), lambda f: (lat(f) // per_b, 0, 0))
    in_specs += [pl.BlockSpec((tm, d), lambda f: (lat(f), 0)), mod, mod, vec(d),
                 pl.BlockSpec((d, nr), lambda f: (0, 0)), vec(nr)]
    args += [x.reshape(bsz * l, d), shift, scale, g.reshape(1, d), wr, br]
    total = bsz * (lc + l)
    return pl.pallas_call(
        functools.partial(_norm_router_body, n_ctx=n_ctx), grid=(n_ctx + n_lat,), in_specs=in_specs,
        out_specs=[pl.BlockSpec((tm, d // 2), lambda f: (f, 0)), pl.BlockSpec((tm, nr), lambda f: (f, 0))],
        out_shape=[jax.ShapeDtypeStruct((total, d // 2), jnp.int32), jax.ShapeDtypeStruct((total, nr), F32)],
        compiler_params=_params(("arbitrary",)), name="norm_router",
    )(*args)


def _mm_body(a_ref, w_ref, o_ref):
    o_ref[...] = jnp.dot(a_ref[...], w_ref[...], preferred_element_type=F32).astype(o_ref.dtype)


def matmul(a, w, out_dtype=F32, tm_pref=1024, tn=512):
    m, k = a.shape
    n = w.shape[1]
    tm = _tile(m, tm_pref)
    return pl.pallas_call(
        _mm_body, grid=(m // tm, n // tn),
        in_specs=[pl.BlockSpec((tm, k), lambda i, j: (i, 0)),
                  pl.BlockSpec((k, tn), lambda i, j: (0, j))],
        out_specs=pl.BlockSpec((tm, tn), lambda i, j: (i, j)),
        out_shape=jax.ShapeDtypeStruct((m, n), out_dtype),
        compiler_params=_params(("arbitrary", "arbitrary")), name="matmul",
    )(a, w)


def _wout_body(ya_ref, yb_ref, yl_ref, w_ref, x_ref, g_ref, o_ref):
    acc = jnp.dot(ya_ref[...], w_ref[0:D_GLA, :], preferred_element_type=F32)
    acc += jnp.dot(yb_ref[...], w_ref[D_GLA:D_GLA + D_MLA, :], preferred_element_type=F32)
    acc += jnp.dot(yl_ref[...], w_ref[D_GLA + D_MLA:, :], preferred_element_type=F32)
    o_ref[...] = x_ref[...] + g_ref[...] * acc


def out_proj(ya, yb, yl, w, x, gate):
    bx, lx, d = x.shape
    tm = _tile(lx, 1024)
    tn = 512
    return pl.pallas_call(
        _wout_body, grid=(bx, lx // tm, d // tn),
        in_specs=[pl.BlockSpec((None, tm, D_GLA), lambda b, i, j: (b, i, 0)),
                  pl.BlockSpec((None, tm, D_MLA), lambda b, i, j: (b, i, 0)),
                  pl.BlockSpec((None, tm, D_LRU), lambda b, i, j: (b, i, 0)),
                  pl.BlockSpec((d, tn), lambda b, i, j: (0, j)),
                  pl.BlockSpec((None, tm, tn), lambda b, i, j: (b, i, j)),
                  pl.BlockSpec((None, 1, tn), lambda b, i, j: (b, 0, j))],
        out_specs=pl.BlockSpec((None, tm, tn), lambda b, i, j: (b, i, j)),
        out_shape=jax.ShapeDtypeStruct((bx, lx, d), F32),
        compiler_params=_params(("arbitrary", "arbitrary", "arbitrary")), name="out_proj",
    )(ya, yb, yl, w, x, gate)


def _gla_body(*refs, reverse, final, nchunk):
    if final:
        (q_ref, k_ref, v_ref, al_ref, wa_ref, ba_ref, s0_ref, of_ref, g_ref, ng_ref,
         o_ref, sfin_ref, s_scr) = refs
    else:
        q_ref, k_ref, v_ref, al_ref, wa_ref, ba_ref, s0_ref, o_ref, sfin_ref, s_scr = refs
    C = GLA_CHUNK

    @pl.when(pl.program_id(1) == 0)
    def _():
        s_scr[...] = s0_ref[...]

    r_i = lax.broadcasted_iota(jnp.int32, (C, C), 0)
    c_i = lax.broadcasted_iota(jnp.int32, (C, C), 1)
    keep = (c_i >= r_i) if reverse else (c_i <= r_i)
    tri = jnp.where(keep, 1.0, 0.0).astype(BF16)
    ref_row = C - 1 - C // 2 if reverse else C // 2
    last_row = 0 if reverse else C - 1

    order = range(nchunk - 1, -1, -1) if reverse else range(nchunk)
    for ci in order:
        rows = slice(ci * C, (ci + 1) * C)
        z = jnp.dot(al_ref[rows, :].astype(BF16), wa_ref[...], preferred_element_type=F32) + ba_ref[...]
        logd = (jnp.minimum(z, 0.0) - jnp.log1p(jnp.exp(-jnp.abs(z)))) * (1.0 / GLA_TAU)
        l_hi = logd.astype(BF16)
        rem = logd - l_hi.astype(F32)
        l_mid = rem.astype(BF16)
        l_lo = (rem - l_mid.astype(F32)).astype(BF16)
        b = (jnp.dot(tri, l_hi, preferred_element_type=F32) + jnp.dot(tri, l_mid, preferred_element_type=F32)
             + jnp.dot(tri, l_lo, preferred_element_type=F32))
        b_ref = b[ref_row:ref_row + 1, :]
        b_last = b[last_row:last_row + 1, :]
        q = q_ref[rows, :] * GLA_QSCALE
        k = k_ref[rows, :]
        qe = (q * jnp.exp(b - b_ref)).astype(BF16)
        ke = (k * jnp.exp(b_ref - b)).astype(BF16)
        qin = (q * jnp.exp(b)).astype(BF16)
        kst = k * jnp.exp(b_last - b)
        for h in range(GLA_HEADS):
            hs = slice(h * GLA_DK, (h + 1) * GLA_DK)
            vs = slice(h * GLA_DV, (h + 1) * GLA_DV)
            v = v_ref[rows, vs].astype(BF16)
            att = lax.dot_general(qe[:, hs], ke[:, hs], (((1,), (1,)), ((), ())),
                                  preferred_element_type=F32)
            att = jnp.where(keep, att, 0.0).astype(BF16)
            s = s_scr[h]
            o = (jnp.dot(att, v, preferred_element_type=F32)
                 + jnp.dot(qin[:, hs], s.astype(BF16), preferred_element_type=F32))
            d_col = jnp.exp(jnp.broadcast_to(b_last[:, hs], (GLA_DK, GLA_DK))).T
            s_scr[h] = (jnp.concatenate([d_col, d_col], axis=1) * s
                        + jnp.dot(kst[:, hs].T.astype(BF16), v, preferred_element_type=F32))
            if final:
                o = o + of_ref[rows, vs]
                y = o * lax.rsqrt(jnp.mean(o * o, axis=-1, keepdims=True) + NORM_EPS) * ng_ref[...]
                o_ref[rows, vs] = (y * _silu(g_ref[rows, vs])).astype(o_ref.dtype)
            else:
                o_ref[rows, vs] = o
    sfin_ref[...] = s_scr[...]


def gla_pass(p, wa, ba, s0, reverse, o_fwd=None, norm_g=None):
    bsz, l, _ = p.shape
    t = _tile(l, 256)
    nt = l // t
    final = o_fwd is not None

    def rix(i):
        return nt - 1 - i if reverse else i

    def col(width, off):
        return pl.BlockSpec((None, t, width), lambda b, i: (b, rix(i), off // width))

    hq = GLA_HEADS * GLA_DK
    state = pl.BlockSpec((None, GLA_HEADS, GLA_DK, GLA_DV), lambda b, i: (b, 0, 0, 0))
    in_specs = [col(hq, COL_Q), col(hq, COL_K), col(D_GLA, COL_V), col(AL_W, COL_AL),
                pl.BlockSpec((AL_W, hq), lambda b, i: (0, 0)),
                pl.BlockSpec((1, hq), lambda b, i: (0, 0)), state]
    args = [p, p, p, p, wa, ba, s0]
    orow = pl.BlockSpec((None, t, D_GLA), lambda b, i: (b, rix(i), 0))
    if final:
        in_specs += [orow, col(D_GLA, COL_G), pl.BlockSpec((1, GLA_DV), lambda b, i: (0, 0))]
        args += [o_fwd, p, norm_g.reshape(1, GLA_DV)]
    return pl.pallas_call(
        functools.partial(_gla_body, reverse=reverse, final=final, nchunk=t // GLA_CHUNK),
        grid=(bsz, nt), in_specs=in_specs, out_specs=[orow, state],
        out_shape=[jax.ShapeDtypeStruct((bsz, l, D_GLA), BF16 if final else F32),
                   jax.ShapeDtypeStruct((bsz, GLA_HEADS, GLA_DK, GLA_DV), F32)],
        scratch_shapes=[pltpu.VMEM((GLA_HEADS, GLA_DK, GLA_DV), F32)],
        compiler_params=_params(("arbitrary", "arbitrary")), name="gla",
    )(*args)


def _lru_body(*refs, reverse, final, t, nt):
    if final:
        (x_ref, xp_ref, xn_ref, cw_ref, cb_ref, wg_ref, bg_ref, lam_ref, h0_ref, hf_ref, gb_ref,
         o_ref, hfin_ref, h_scr, hbuf) = refs
    else:
        (x_ref, xp_ref, xn_ref, cw_ref, cb_ref, wg_ref, bg_ref, lam_ref, h0_ref,
         o_ref, hfin_ref, h_scr) = refs
        hbuf = o_ref
    i = pl.program_id(1)
    ti = nt - 1 - i if reverse else i

    @pl.when(i == 0)
    def _():
        h_scr[...] = h0_ref[...]

    ngroup = t // SUBLANES
    grp = lambda v: v.reshape(v.shape[0] // SUBLANES, SUBLANES, v.shape[1])
    sub = lax.broadcasted_iota(jnp.int32, (1, SUBLANES, 1), 1)
    x3 = grp(x_ref[...])
    xp3 = grp(jnp.where(ti == 0, 0.0, xp_ref[...]))
    xn3 = grp(jnp.where(ti == nt - 1, 0.0, xn_ref[...]))

    def row_shift(k):
        rot = pltpu.roll(x3, k % SUBLANES, axis=1)
        if k > 0:
            nb_rot = jnp.concatenate([pltpu.roll(xp3, k % SUBLANES, axis=1), rot[:-1]], axis=0)
            return jnp.where(sub >= k, rot, nb_rot)
        nb_rot = jnp.concatenate([rot[1:], pltpu.roll(xn3, k % SUBLANES, axis=1)], axis=0)
        return jnp.where(sub < SUBLANES + k, rot, nb_rot)

    cw = cw_ref[...]
    xc = (cb_ref[...] + row_shift(2) * cw[0:1, :] + row_shift(1) * cw[1:2, :]
          + x3 * cw[2:3, :] + row_shift(-1) * cw[3:4, :]).reshape(t, D_LRU)

    zs = []
    for n in range(LRU_BLOCKS):
        zs.append(jnp.dot(xc[:, n * LRU_BW:(n + 1) * LRU_BW].astype(BF16), wg_ref[n],
                          preferred_element_type=F32))
    z_r = jnp.concatenate([z[:, :LRU_BW] for z in zs], axis=1) + bg_ref[0:1, :]
    z_i = jnp.concatenate([z[:, LRU_BW:] for z in zs], axis=1) + bg_ref[1:2, :]
    log_a = -LRU_C * jax.nn.sigmoid(z_r) * _softplus(-lam_ref[...])
    a = jnp.exp(log_a)
    th = jnp.tanh(log_a)
    one_minus_a2 = -2.0 * th / (1.0 - th)
    u = jnp.sqrt(one_minus_a2) * jax.nn.sigmoid(z_i) * xc

    a, u = grp(a), grp(u)
    s = 1
    while s < SUBLANES:
        if reverse:
            ok = sub < SUBLANES - s
            a_s = pltpu.roll(a, SUBLANES - s, axis=1)
            u_s = pltpu.roll(u, SUBLANES - s, axis=1)
        else:
            ok = sub >= s
            a_s = pltpu.roll(a, s, axis=1)
            u_s = pltpu.roll(u, s, axis=1)
        u = jnp.where(ok, a * u_s + u, u)
        a = jnp.where(ok, a * a_s, a)
        s *= 2
    edge = 0 if reverse else SUBLANES - 1
    h_prev = h_scr[...]
    for gi in (range(ngroup - 1, -1, -1) if reverse else range(ngroup)):
        rows = slice(gi * SUBLANES, (gi + 1) * SUBLANES)
        hg = a[gi] * h_prev + u[gi]
        h_prev = hg[edge:edge + 1, :]
        hbuf[rows, :] = hg
    h_scr[...] = h_prev
    hfin_ref[...] = h_prev
    if final:
        g = gb_ref[...]
        gelu = 0.5 * g * (1.0 + jnp.tanh(0.7978845608028654 * (g + 0.044715 * g * g * g)))
        o_ref[...] = (gelu * (hf_ref[...] + hbuf[...])).astype(o_ref.dtype)


def lru_pass(p, cw, cb, wg, bg, lam, h0, reverse, h_fwd=None):
    bsz, l, _ = p.shape
    t = _tile(l, 256)
    nt = l // t
    tb = t // SUBLANES
    nb8 = l // SUBLANES
    final = h_fwd is not None
    xoff = COL_XB // D_LRU

    def rix(i):
        return nt - 1 - i if reverse else i

    vec = lambda r: pl.BlockSpec((r, D_LRU), lambda b, i: (0, 0))
    one = pl.BlockSpec((None, 1, D_LRU), lambda b, i: (b, 0, 0))
    orow = pl.BlockSpec((None, t, D_LRU), lambda b, i: (b, rix(i), 0))
    in_specs = [pl.BlockSpec((None, t, D_LRU), lambda b, i: (b, rix(i), xoff)),
                pl.BlockSpec((None, SUBLANES, D_LRU),
                             lambda b, i: (b, jnp.maximum(rix(i) * tb - 1, 0), xoff)),
                pl.BlockSpec((None, SUBLANES, D_LRU),
                             lambda b, i: (b, jnp.minimum((rix(i) + 1) * tb, nb8 - 1), xoff)),
                vec(LRU_CONV), vec(1),
                pl.BlockSpec((LRU_BLOCKS, LRU_BW, 2 * LRU_BW), lambda b, i: (0, 0, 0)),
                vec(2), vec(1), one]
    args = [p, p, p, cw, cb.reshape(1, D_LRU), wg, bg, lam.reshape(1, D_LRU), h0]
    if final:
        in_specs += [orow, pl.BlockSpec((None, t, D_LRU), lambda b, i: (b, rix(i), COL_GB // D_LRU))]
        args += [h_fwd, p]
    return pl.pallas_call(
        functools.partial(_lru_body, reverse=reverse, final=final, t=t, nt=nt),
        grid=(bsz, nt), in_specs=in_specs, out_specs=[orow, one],
        out_shape=[jax.ShapeDtypeStruct((bsz, l, D_LRU), BF16 if final else F32),
                   jax.ShapeDtypeStruct((bsz, 1, D_LRU), F32)],
        scratch_shapes=[pltpu.VMEM((1, D_LRU), F32)] + ([pltpu.VMEM((t, D_LRU), F32)] if final else []),
        compiler_params=_params(("arbitrary", "arbitrary")), name="lru",
    )(*args)


def _rms(x, width):
    return x * lax.rsqrt(jnp.sum(x * x, axis=-1, keepdims=True) * (1.0 / width) + NORM_EPS)


def _rope(r, cos, sin):
    lane = lax.broadcasted_iota(jnp.int32, r.shape, 1)
    first = (lane % 32) < 16
    rot = jnp.where(first, pltpu.roll(r, LANES - 16, axis=1), pltpu.roll(r, 16, axis=1))
    return r * cos + rot * sin


def _qproj_body(x_ref, gn_ref, w_ref, hg_ref, cos_ref, sin_ref, o_ref, xn_scr):
    @pl.when(pl.program_id(2) == 0)
    def _():
        xn_scr[...] = (_rms(x_ref[...], MLA_Q_RANK) * gn_ref[...]).astype(BF16)

    a = jnp.dot(xn_scr[...], w_ref[...], preferred_element_type=F32)
    for hh in range(HEADS_PER_STEP):
        c0 = hh * HEAD_W
        qn = _rms(a[:, c0:c0 + MLA_NOPE], MLA_NOPE) * hg_ref[:, :MLA_NOPE]
        qr = _rms(a[:, c0 + MLA_NOPE:c0 + HEAD_W], MLA_ROPE) * hg_ref[:, MLA_NOPE:]
        qr = _rope(qr, cos_ref[...], sin_ref[...])
        o_ref[hh] = (jnp.concatenate([qn, qr], axis=1) * (MLA_SCALE * LOG2E)).astype(o_ref.dtype)


def q_proj(p, gn, w, hg, cos, sin):
    bsz, l, _ = p.shape
    tm = _tile(l, 512)
    hw = HEADS_PER_STEP * HEAD_W
    return pl.pallas_call(
        _qproj_body, grid=(bsz, l // tm, MLA_HEADS // HEADS_PER_STEP),
        in_specs=[pl.BlockSpec((None, tm, QDN_W), lambda b, i, h: (b, i, COL_QDN // QDN_W)),
                  pl.BlockSpec((1, QDN_W), lambda b, i, h: (0, 0)),
                  pl.BlockSpec((QDN_W, hw), lambda b, i, h: (0, h)),
                  pl.BlockSpec((1, HEAD_W), lambda b, i, h: (0, 0)),
                  pl.BlockSpec((tm, LANES), lambda b, i, h: (i, 0)),
                  pl.BlockSpec((tm, LANES), lambda b, i, h: (i, 0))],
        out_specs=pl.BlockSpec((None, HEADS_PER_STEP, tm, HEAD_W), lambda b, i, h: (b, h, i, 0)),
        out_shape=jax.ShapeDtypeStruct((bsz, MLA_HEADS, l, HEAD_W), BF16),
        scratch_shapes=[pltpu.VMEM((tm, QDN_W), BF16)],
        compiler_params=_params(("arbitrary", "arbitrary", "arbitrary")), name="q_proj",
    )(p, gn.reshape(1, QDN_W), w, hg, cos, sin)


def _kvproj_body(x_ref, kr_ref, gn_ref, w_ref, hg_ref, cos_ref, sin_ref, k_ref, v_ref, xn_scr, kr_scr):
    @pl.when(pl.program_id(2) == 0)
    def _():
        xn_scr[...] = (_rms(x_ref[...], MLA_KV_RANK) * gn_ref[...]).astype(BF16)
        kr = _rms(kr_ref[...], MLA_ROPE) * hg_ref[:, MLA_NOPE:]
        kr_scr[...] = _rope(kr, cos_ref[...], sin_ref[...]).astype(BF16)

    a = jnp.dot(xn_scr[...], w_ref[...], preferred_element_type=F32)
    for hh in range(HEADS_PER_STEP):
        c0 = hh * HEAD_W
        kn = _rms(a[:, c0:c0 + MLA_NOPE], MLA_NOPE) * hg_ref[:, :MLA_NOPE]
        k_ref[hh, :, :MLA_NOPE] = kn.astype(k_ref.dtype)
        k_ref[hh, :, MLA_NOPE:] = kr_scr[...]
        v_ref[hh, :MLA_V, :] = a[:, c0 + MLA_NOPE:c0 + HEAD_W].T.astype(v_ref.dtype)
        v_ref[hh, MLA_V:, :] = jnp.ones((VT_ROWS - MLA_V, a.shape[0]), v_ref.dtype)


def kv_proj(p, gn, w, hg, cos, sin):
    bsz, l, _ = p.shape
    tm = _tile(l, 512)
    hw = HEADS_PER_STEP * HEAD_W
    return pl.pallas_call(
        _kvproj_body, grid=(bsz, l // tm, MLA_HEADS // HEADS_PER_STEP),
        in_specs=[pl.BlockSpec((None, tm, MLA_KV_RANK), lambda b, i, h: (b, i, COL_KVDN // MLA_KV_RANK)),
                  pl.BlockSpec((None, tm, KR_W), lambda b, i, h: (b, i, COL_KR // KR_W)),
                  pl.BlockSpec((1, MLA_KV_RANK), lambda b, i, h: (0, 0)),
                  pl.BlockSpec((MLA_KV_RANK, hw), lambda b, i, h: (0, h)),
                  pl.BlockSpec((1, HEAD_W), lambda b, i, h: (0, 0)),
                  pl.BlockSpec((tm, LANES), lambda b, i, h: (i, 0)),
                  pl.BlockSpec((tm, LANES), lambda b, i, h: (i, 0))],
        out_specs=[pl.BlockSpec((None, HEADS_PER_STEP, tm, HEAD_W), lambda b, i, h: (b, h, i, 0)),
                   pl.BlockSpec((None, HEADS_PER_STEP, VT_ROWS, tm), lambda b, i, h: (b, h, 0, i))],
        out_shape=[jax.ShapeDtypeStruct((bsz, MLA_HEADS, l, HEAD_W), BF16),
                   jax.ShapeDtypeStruct((bsz, MLA_HEADS, VT_ROWS, l), BF16)],
        scratch_shapes=[pltpu.VMEM((tm, MLA_KV_RANK), BF16), pltpu.VMEM((tm, KR_W), BF16)],
        compiler_params=_params(("arbitrary", "arbitrary", "arbitrary")), name="kv_proj",
    )(p, p, gn.reshape(1, MLA_KV_RANK), w, hg, cos, sin)


def _scores_t(k, q):
    return lax.dot_general(k, q, (((1,), (1,)), ((), ())), preferred_element_type=F32)


def _attn_body(*refs, tk, n_lat):
    if n_lat:
        q_ref, kc_ref, vc_ref, k_ref, v_ref, o_ref, m_scr, acc_scr, s_scr = refs
    else:
        q_ref, kc_ref, vc_ref, o_ref = refs

    def qk(slot, off):
        s_scr[slot] = _scores_t(k_ref[pl.ds(off, tk), :], q_ref[...])

    if n_lat:
        qk(0, 0)

    sc = _scores_t(kc_ref[...], q_ref[...])
    m0 = jnp.max(sc, axis=0, keepdims=True)
    acc = jnp.dot(vc_ref[...], jnp.exp2(sc - m0).astype(BF16), preferred_element_type=F32)

    if n_lat:
        m_scr[...] = m0
        acc_scr[...] = acc

        def consume(slot, off):
            s = s_scr[slot]
            m = m_scr[...]
            m_new = jnp.maximum(m, jnp.max(s, axis=0, keepdims=True))
            p = jnp.exp2(s - m_new).astype(BF16)
            m_scr[...] = m_new
            acc_scr[...] = (jnp.exp2(m - m_new) * acc_scr[...]
                            + jnp.dot(v_ref[:, pl.ds(off, tk)], p, preferred_element_type=F32))

        def pair(jj, carry):
            off = pl.multiple_of(jj * (2 * tk), 2 * tk)
            qk(1, off + tk)
            consume(0, off)
            qk(0, off + 2 * tk)
            consume(1, off + tk)
            return carry

        lax.fori_loop(0, n_lat // 2 - 1, pair, 0)
        off = (n_lat - 2) * tk
        qk(1, off + tk)
        consume(0, off)
        consume(1, off + tk)
        acc = acc_scr[...]
    o_ref[...] = (acc[:MLA_V] / acc[MLA_V:MLA_V + 1]).T.astype(o_ref.dtype)


def attention(q, kc, vc, k=None, v=None):
    bsz, nh, lq, _ = q.shape
    lc = kc.shape[2]
    tq = _tile(lq, 1024)
    keys = lambda n: pl.BlockSpec((None, None, n, HEAD_W), lambda b, h, i: (b, h, 0, 0))
    vals = lambda n: pl.BlockSpec((None, None, VT_ROWS, n), lambda b, h, i: (b, h, 0, 0))
    in_specs = [pl.BlockSpec((None, None, tq, HEAD_W), lambda b, h, i: (b, h, i, 0)), keys(lc), vals(lc)]
    args = [q, kc, vc]
    tk = n_lat = 0
    scratch = []
    if k is not None:
        lk = k.shape[2]
        tk = _tile(lk // 2, 1024)
        assert tk % LANES == 0 and lk % (2 * tk) == 0
        n_lat = lk // tk
        in_specs += [keys(lk), vals(lk)]
        args += [k, v]
        scratch = [pltpu.VMEM((1, tq), F32), pltpu.VMEM((VT_ROWS, tq), F32), pltpu.VMEM((2, tk, tq), F32)]
    return pl.pallas_call(
        functools.partial(_attn_body, tk=tk, n_lat=n_lat),
        grid=(bsz, nh, lq // tq), in_specs=in_specs,
        out_specs=pl.BlockSpec((None, tq, MLA_V), lambda b, h, i: (b, i, h)),
        out_shape=jax.ShapeDtypeStruct((bsz, lq, nh * MLA_V), BF16),
        scratch_shapes=scratch,
        compiler_params=_params(("arbitrary", "arbitrary", "arbitrary")), name="attention",
    )(*args)


def _row_copy(src_hbm, row, buf, slot, r, sem):
    return pltpu.make_async_copy(src_hbm.at[pl.ds(row, 1), :], buf.at[slot, pl.ds(r, 1), :], sem.at[slot])


def _moe_body(be_ref, nu_ref, first_ref, tok_ref, h_hbm, w13_ref, w2_ref, o_ref, xbuf, sem, *, n_assign):
    j = pl.program_id(0)
    nu = nu_ref[0]
    slot = j % MOE_SLOTS
    ahead = MOE_SLOTS - 1

    def rows(blk, slot, start):
        base = first_ref[blk]

        def body(r, carry):
            tok = tok_ref[jnp.minimum(base + r, n_assign - 1)]
            cp = _row_copy(h_hbm, tok, xbuf, slot, r, sem)
            if start:
                cp.start()
            else:
                cp.wait()
            return carry

        lax.fori_loop(0, MOE_BLOCK, body, 0, unroll=8)

    for first in range(ahead):
        @pl.when((j == 0) & (first < nu))
        def _():
            rows(first, first, True)

    @pl.when(j + ahead < nu)
    def _():
        rows(j + ahead, (j + ahead) % MOE_SLOTS, True)

    @pl.when(j < nu)
    def _():
        rows(j, slot, False)
        x_lo, x_hi = _unpack_halves(xbuf[slot])
        half = w13_ref.shape[0] // 2
        hid = (jnp.dot(x_lo.astype(BF16), w13_ref[:half, :].astype(BF16), preferred_element_type=F32)
               + jnp.dot(x_hi.astype(BF16), w13_ref[half:, :].astype(BF16), preferred_element_type=F32))
        act = (_silu(hid[:, :D_EXPERT]) * hid[:, D_EXPERT:]).astype(BF16)
        o_ref[...] = _pack_halves(jnp.dot(act, w2_ref[...].astype(BF16), preferred_element_type=F32))

    @pl.when(j >= nu)
    def _():
        o_ref[...] = jnp.zeros_like(o_ref)


def moe_experts(h, tok_sorted, blk_first, blk_exp, n_used, w13, w2, layer):
    m, dp = h.shape
    d = 2 * dp
    nb = blk_first.shape[0]
    grid_spec = pltpu.PrefetchScalarGridSpec(
        num_scalar_prefetch=4, grid=(nb,),
        in_specs=[pl.BlockSpec(memory_space=pl.ANY),
                  pl.BlockSpec((None, None, d, 2 * D_EXPERT), lambda j, be, nu, fi, tok: (layer, be[j], 0, 0)),
                  pl.BlockSpec((None, None, D_EXPERT, d), lambda j, be, nu, fi, tok: (layer, be[j], 0, 0))],
        out_specs=pl.BlockSpec((MOE_BLOCK, dp), lambda j, be, nu, fi, tok: (j, 0)),
        scratch_shapes=[pltpu.VMEM((MOE_SLOTS, MOE_BLOCK, dp), jnp.int32),
                        pltpu.SemaphoreType.DMA((MOE_SLOTS,))])
    return pl.pallas_call(
        functools.partial(_moe_body, n_assign=tok_sorted.shape[0]), grid_spec=grid_spec,
        out_shape=jax.ShapeDtypeStruct((nb * MOE_BLOCK, dp), jnp.int32),
        compiler_params=_params(("arbitrary",)), name="moe_experts",
    )(blk_exp, n_used, blk_first, tok_sorted, h, w13, w2)


def _combine_body(dest_ref, out_hbm, x_ref, g_ref, tg_ref, *rest, tok_off, t, n_tiles, with_norm):
    if with_norm:
        ng_ref, sh_ref, sc_ref, o_ref, h_ref, buf_a, buf_b, sem_a, sem_b = rest
    else:
        o_ref, buf_a, buf_b, sem_a, sem_b = rest
    f = pl.program_id(0) * pl.num_programs(1) + pl.program_id(1)
    slot = f % 2

    def rows(tile, slot, start):
        base = (tok_off + tile * t) * TOP_K

        def body(r, carry):
            ca = _row_copy(out_hbm, dest_ref[base + TOP_K * r], buf_a, slot, r, sem_a)
            cb = _row_copy(out_hbm, dest_ref[base + TOP_K * r + 1], buf_b, slot, r, sem_b)
            if start:
                ca.start()
                cb.start()
            else:
                ca.wait()
                cb.wait()
            return carry

        lax.fori_loop(0, t, body, 0, unroll=8)

    @pl.when(f == 0)
    def _():
        rows(0, 0, True)

    @pl.when(f + 1 < n_tiles)
    def _():
        rows(f + 1, 1 - slot, True)

    rows(f, slot, False)
    d = x_ref.shape[1]
    half = d // 2
    tg0, tg1 = tg_ref[:, :LANES], tg_ref[:, LANES:]
    ss = jnp.zeros((t, LANES), F32)
    for c in range(half // LANES):
        wcols = slice(c * LANES, (c + 1) * LANES)
        a_lo, a_hi = _unpack_halves(buf_a[slot, :, wcols])
        b_lo, b_hi = _unpack_halves(buf_b[slot, :, wcols])
        for cols, ya, yb in ((wcols, a_lo, b_lo),
                             (slice(half + c * LANES, half + (c + 1) * LANES), a_hi, b_hi)):
            y = x_ref[:, cols] + g_ref[:, cols] * (tg0 * ya + tg1 * yb)
            o_ref[:, cols] = y
            ss = ss + y * y
    if with_norm:
        rs = lax.rsqrt(jnp.sum(ss, axis=-1, keepdims=True) * (1.0 / d) + NORM_EPS)
        rs = jnp.broadcast_to(rs, (t, LANES))
        for c in range(d // LANES):
            cols = slice(c * LANES, (c + 1) * LANES)
            h_ref[:, cols] = (o_ref[:, cols] * rs * ng_ref[:, cols] * (1.0 + sc_ref[:, cols])
                              + sh_ref[:, cols]).astype(h_ref.dtype)


def moe_combine(x, gate, out, dest, tok_gate, tok_off, norm=None):
    bx, lx, d = x.shape
    t = _tile(lx, 128)
    nt = lx // t
    tg = jnp.repeat(tok_gate[tok_off:tok_off + bx * lx], LANES, axis=1).reshape(bx, lx, TOP_K * LANES)
    row = pl.BlockSpec((None, t, d), lambda b, i, dst: (b, i, 0))
    mod = pl.BlockSpec((None, 1, d), lambda b, i, dst: (b, 0, 0))
    in_specs = [pl.BlockSpec(memory_space=pl.ANY), row, mod,
                pl.BlockSpec((None, t, TOP_K * LANES), lambda b, i, dst: (b, i, 0))]
    args = [dest, out, x, gate, tg]
    out_specs, out_shape = row, jax.ShapeDtypeStruct((bx, lx, d), F32)
    if norm is not None:
        in_specs += [pl.BlockSpec((1, d), lambda b, i, dst: (0, 0)), mod, mod]
        args += [norm[0].reshape(1, d), norm[1], norm[2]]
        out_specs, out_shape = [row, row], [out_shape, jax.ShapeDtypeStruct((bx, lx, d), BF16)]
    grid_spec = pltpu.PrefetchScalarGridSpec(
        num_scalar_prefetch=1, grid=(bx, nt), in_specs=in_specs, out_specs=out_specs,
        scratch_shapes=[pltpu.VMEM((2, t, d // 2), jnp.int32), pltpu.VMEM((2, t, d // 2), jnp.int32),
                        pltpu.SemaphoreType.DMA((2,)), pltpu.SemaphoreType.DMA((2,))])
    return pl.pallas_call(
        functools.partial(_combine_body, tok_off=tok_off, t=t, n_tiles=bx * nt, with_norm=norm is not None),
        grid_spec=grid_spec, out_shape=out_shape,
        compiler_params=_params(("arbitrary", "arbitrary")), name="moe_combine",
    )(*args)


def moe_route(h, logits, w13, w2, layer):
    m = h.shape[0]
    g_logit = logits[:, :N_GROUPS]
    e_logit = logits[:, N_GROUPS:N_GROUPS + N_EXPERTS].reshape(m, N_GROUPS, EXPERTS_PER_GROUP)
    g_idx = jnp.argmax(g_logit, axis=-1)
    p_grp = jnp.take_along_axis(jax.nn.softmax(g_logit, axis=-1), g_idx[:, None], axis=1)
    e_sel = jnp.take_along_axis(e_logit, g_idx[:, None, None], axis=1)[:, 0]
    top_v, top_i = lax.top_k(e_sel, TOP_K)
    gate = (p_grp * jax.nn.softmax(top_v, axis=-1)).reshape(-1)
    e_flat = (g_idx[:, None] * EXPERTS_PER_GROUP + top_i).reshape(-1).astype(jnp.int32)

    a = m * TOP_K
    nb = -(-a // MOE_BLOCK) + N_EXPERTS
    order = jnp.argsort(e_flat).astype(jnp.int32)
    rank = jnp.argsort(order).astype(jnp.int32)
    e_s = e_flat[order]
    experts = jnp.arange(N_EXPERTS, dtype=jnp.int32)
    start = jnp.searchsorted(e_s, experts, side='left').astype(jnp.int32)
    counts = jnp.searchsorted(e_s, experts, side='right').astype(jnp.int32) - start
    padded = -(-counts // MOE_BLOCK) * MOE_BLOCK
    pad_end = jnp.cumsum(padded)
    pad_start = pad_end - padded
    blk_ids = jnp.arange(nb, dtype=jnp.int32)
    blk_raw = jnp.searchsorted(pad_end, blk_ids * MOE_BLOCK, side='right')
    blk_exp = jnp.minimum(blk_raw, N_EXPERTS - 1).astype(jnp.int32)
    n_used = (pad_end[-1] // MOE_BLOCK).astype(jnp.int32).reshape(1)
    blk_first = jnp.where(blk_raw < N_EXPERTS,
                          start[blk_exp] + blk_ids * MOE_BLOCK - pad_start[blk_exp], 0).astype(jnp.int32)
    tok_sorted = order // TOP_K
    dest = (pad_start[e_flat] + rank - start[e_flat]).astype(jnp.int32)

    out = moe_experts(h, tok_sorted, blk_first, blk_exp, n_used, w13, w2, layer)
    return out, dest, gate.reshape(m, TOP_K)


def _prep_w_in(w):
    sizes = (512, 512, 1024, 1024, 32, 768, 512, 64, 1024, 1024)
    offs = [0]
    for s in sizes:
        offs.append(offs[-1] + s)
    q, k, v, g, al, qdn, kvdn, kr, xb, gb = [w[:, offs[i]:offs[i + 1]] for i in range(10)]
    zpad = lambda t, n: jnp.pad(t, ((0, 0), (0, n - t.shape[1])))
    return jnp.concatenate([qdn, zpad(al, AL_W), zpad(kr, KR_W), v, g, xb, gb, q, k, kvdn],
                           axis=1).astype(BF16)


def _prep_w_uq(w):
    w = w.reshape(MLA_Q_RANK, MLA_HEADS, MLA_NOPE + MLA_ROPE)
    w = jnp.pad(w, ((0, 0), (0, 0), (0, HEAD_W - MLA_NOPE - MLA_ROPE)))
    return w.reshape(MLA_Q_RANK, MLA_HEADS * HEAD_W).astype(BF16)


def _head_gain(g):
    return jnp.pad(g, (0, HEAD_W - g.shape[0])).reshape(1, HEAD_W)


def _rope_tables(rows):
    n = MLA_ROPE // 4
    inv = ROPE_BASE ** (-jnp.arange(n, dtype=F32) / n)
    r = jnp.repeat(jnp.arange(rows, dtype=F32), GRID_W)
    c = jnp.tile(jnp.arange(GRID_W, dtype=F32), rows)
    ar, ac = r[:, None] * inv, c[:, None] * inv
    l = ar.shape[0]
    cos = jnp.concatenate([jnp.cos(ar), jnp.cos(ar), jnp.cos(ac), jnp.cos(ac),
                           jnp.ones((l, LANES - MLA_ROPE), F32)], axis=1)
    sin = jnp.concatenate([-jnp.sin(ar), jnp.sin(ar), -jnp.sin(ac), jnp.sin(ac),
                           jnp.zeros((l, LANES - MLA_ROPE), F32)], axis=1)
    return cos, sin


def _gla_gate_w(w_a2, b_a, d):
    hq = GLA_HEADS * GLA_DK
    w = jnp.zeros((AL_W, hq), F32).at[d * GLA_GATE_RANK:(d + 1) * GLA_GATE_RANK].set(w_a2[d])
    return w.astype(BF16), b_a[d].reshape(1, hq)


def _lru_gate_w(w_gate, d):
    return jnp.concatenate([w_gate[d, 0], w_gate[d, 1]], axis=-1).astype(BF16)


def _layer(x, xc, mods, rope, prm, moe_w13, moe_w2, layer, last, h_pre=None, nxt=None):
    (norm1_g, norm2_g, w_in, gla_w_a2, gla_b_a, gla_norm_g,
     mla_q_norm_g, mla_kv_norm_g, mla_w_uq, mla_w_ukv, mla_q_head_g, mla_k_head_g,
     lru_conv_w, lru_conv_b, lru_w_gate, lru_b_gate, lru_lambda, w_out,
     moe_w_grp, moe_b_grp, moe_w_exp, moe_b_exp) = prm
    bsz, l, d = x.shape
    lc = xc.shape[1]
    ctx_out = not last
    lat = lambda k: mods[:bsz, k].reshape(bsz, 1, d)
    ctx = lambda k: jnp.broadcast_to(mods[bsz, k].reshape(1, 1, d), (bsz, 1, d))

    w_in_p = _prep_w_in(w_in)
    if h_pre is None:
        h = norm_mod(x, norm1_g, lat(0), lat(1))
        hc = norm_mod(xc, norm1_g, ctx(0), ctx(1))
    else:
        h, hc = h_pre
    p = matmul(h.reshape(bsz * l, d), w_in_p).reshape(bsz, l, D_INP)
    pc = matmul(hc.reshape(bsz * lc, d), w_in_p).reshape(bsz, lc, D_INP)

    waf, baf = _gla_gate_w(gla_w_a2, gla_b_a, 0)
    wab, bab = _gla_gate_w(gla_w_a2, gla_b_a, 1)
    s_zero = jnp.zeros((bsz, GLA_HEADS, GLA_DK, GLA_DV), F32)
    ocf, s_f = gla_pass(pc, waf, baf, s_zero, False)
    of, _ = gla_pass(p, waf, baf, s_f, False)
    if ctx_out:
        yac, s_b = gla_pass(pc, wab, bab, s_zero, True, ocf, gla_norm_g)
    else:
        _, s_b = gla_pass(pc, wab, bab, s_zero, True)
    ya, _ = gla_pass(p, wab, bab, s_b, True, of, gla_norm_g)

    wgf, wgb = _lru_gate_w(lru_w_gate, 0), _lru_gate_w(lru_w_gate, 1)
    h_zero = jnp.zeros((bsz, 1, D_LRU), F32)
    lru = functools.partial(lru_pass, cw=lru_conv_w, cb=lru_conv_b)
    hcf, h0f = lru(pc, wg=wgf, bg=lru_b_gate[0], lam=lru_lambda[0], h0=h_zero, reverse=False)
    hf, _ = lru(p, wg=wgf, bg=lru_b_gate[0], lam=lru_lambda[0], h0=h0f, reverse=False)
    if ctx_out:
        ylc, h0b = lru(pc, wg=wgb, bg=lru_b_gate[1], lam=lru_lambda[1], h0=h_zero, reverse=True, h_fwd=hcf)
    else:
        _, h0b = lru(pc, wg=wgb, bg=lru_b_gate[1], lam=lru_lambda[1], h0=h_zero, reverse=True)
    yl, _ = lru(p, wg=wgb, bg=lru_b_gate[1], lam=lru_lambda[1], h0=h0b, reverse=True, h_fwd=hf)

    cos, sin = rope
    cos_c = jnp.ones((lc, LANES), F32)
    sin_c = jnp.zeros((lc, LANES), F32)
    w_uq = _prep_w_uq(mla_w_uq)
    w_ukv = mla_w_ukv.astype(BF16)
    qg, kg = _head_gain(mla_q_head_g), _head_gain(mla_k_head_g)
    kc, vc = kv_proj(pc, mla_kv_norm_g, w_ukv, kg, cos_c, sin_c)
    kl, vl = kv_proj(p, mla_kv_norm_g, w_ukv, kg, cos, sin)
    ql = q_proj(p, mla_q_norm_g, w_uq, qg, cos, sin)
    yb = attention(ql, kc, vc, kl, vl)

    w_out_b = w_out.astype(BF16)
    x = out_proj(ya, yb, yl, w_out_b, x, lat(2))
    if ctx_out:
        qc = q_proj(pc, mla_q_norm_g, w_uq, qg, cos_c, sin_c)
        ybc = attention(qc, kc, vc)
        xc = out_proj(yac, ybc, ylc, w_out_b, xc, ctx(2))

    nr = LANES
    wr = jnp.pad(jnp.concatenate([moe_w_grp, moe_w_exp], axis=1), ((0, 0), (0, nr - N_GROUPS - N_EXPERTS)))
    br = jnp.pad(jnp.concatenate([moe_b_grp, moe_b_exp]), (0, nr - N_GROUPS - N_EXPERTS)).reshape(1, nr)
    ctx1 = lambda k: mods[bsz, k].reshape(1, 1, d)
    h2, lg = norm_router(x, norm2_g, lat(3), lat(4), wr, br,
                         ctx=(xc, ctx1(3), ctx1(4)) if ctx_out else None)
    out, dest, tok_gate = moe_route(h2, lg, moe_w13, moe_w2, layer)
    h_next = None
    if ctx_out:
        g_next, m_next = nxt
        nlat = lambda k: m_next[:bsz, k].reshape(bsz, 1, d)
        nctx = lambda k: jnp.broadcast_to(m_next[bsz, k].reshape(1, 1, d), (bsz, 1, d))
        xc, hc_next = moe_combine(xc, ctx(5), out, dest, tok_gate, 0, norm=(g_next, nctx(0), nctx(1)))
        x, hl_next = moe_combine(x, lat(5), out, dest, tok_gate, bsz * lc, norm=(g_next, nlat(0), nlat(1)))
        h_next = (hl_next, hc_next)
    else:
        x = moe_combine(x, lat(5), out, dest, tok_gate, 0)
    return x, xc, h_next


def kernel(x, c, ctx, c_ctx, w_mod, b_mod, norm1_g, norm2_g, w_in, gla_w_a2, gla_b_a, gla_norm_g,
           mla_q_norm_g, mla_kv_norm_g, mla_w_uq, mla_w_ukv, mla_q_head_g, mla_k_head_g,
           lru_conv_w, lru_conv_b, lru_w_gate, lru_b_gate, lru_lambda, w_out,
           moe_w_grp, moe_b_grp, moe_w_exp, moe_b_exp, moe_w13, moe_w2):
    bsz, l, d = x.shape
    depth = w_mod.shape[0]
    assert bsz + 1 <= SUBLANES and l % GRID_W == 0
    cond = jnp.zeros((SUBLANES, d), F32).at[:bsz].set(c).at[bsz].set(c_ctx)
    mods = adaln_all(cond, w_mod, b_mod).reshape(depth, SUBLANES, 6, d)
    rope = _rope_tables(l // GRID_W)
    per_layer = (norm1_g, norm2_g, w_in, gla_w_a2, gla_b_a, gla_norm_g,
                 mla_q_norm_g, mla_kv_norm_g, mla_w_uq, mla_w_ukv, mla_q_head_g, mla_k_head_g,
                 lru_conv_w, lru_conv_b, lru_w_gate, lru_b_gate, lru_lambda, w_out,
                 moe_w_grp, moe_b_grp, moe_w_exp, moe_b_exp)
    xc = ctx
    h_pre = None
    for i in range(depth):
        last = i == depth - 1
        x, xc, h_pre = _layer(x, xc, mods[i], rope, tuple(t[i] for t in per_layer), moe_w13, moe_w2, i,
                              last=last, h_pre=h_pre, nxt=None if last else (norm1_g[i + 1], mods[i + 1]))
    return x
```

```python
import functools

import jax
import jax.numpy as jnp
from jax import lax
from jax.experimental import pallas as pl
from jax.experimental.pallas import tpu as pltpu

F32 = jnp.float32
BF16 = jnp.bfloat16
HI = lax.Precision.HIGHEST

D_MODEL = 4096
GRID_W = 64
NORM_EPS = 1e-6
D_GLA = D_MODEL // 4
D_MLA = D_MODEL // 2
D_LRU = D_MODEL // 4
GLA_HEADS = 4
GLA_DK = 128
GLA_DV = 256
GLA_GATE_RANK = 16
GLA_TAU = 16.0
GLA_CHUNK = 64
GLA_QSCALE = GLA_DK ** -0.5
MLA_HEADS = 16
MLA_NOPE = 128
MLA_ROPE = 64
MLA_V = 128
MLA_Q_RANK = 768
MLA_KV_RANK = 512
MLA_SCALE = (MLA_NOPE + MLA_ROPE) ** -0.5
LOG2E = 1.4426950408889634
ROPE_BASE = 10000.0
LRU_BLOCKS = 8
LRU_BW = 128
LRU_CONV = 4
LRU_C = 8.0
N_GROUPS = 8
EXPERTS_PER_GROUP = 8
N_EXPERTS = 64
TOP_K = 2
D_EXPERT = 256

LANES = 128
SUBLANES = 8
VMEM_BYTES = 64 << 20
VMEM_LIMIT = VMEM_BYTES - (8 << 20)
MOE_BLOCK = 256
MOE_SLOTS = 3

QDN_W, AL_W, KR_W = MLA_Q_RANK, LANES, LANES
COL_QDN = 0
COL_AL = 768
COL_KR = 896
COL_V = 1024
COL_G = 2048
COL_XB = 3072
COL_GB = 4096
COL_Q = 5120
COL_K = 5632
COL_KVDN = 6144
D_INP = 6656
HEAD_W = 256
VT_ROWS = MLA_V + 16
HEADS_PER_STEP = 4


def _params(sem, vmem=VMEM_LIMIT):
    return pltpu.CompilerParams(dimension_semantics=sem, vmem_limit_bytes=vmem)


def _tile(n, pref):
    t = min(n, pref)
    while n % t or t % SUBLANES:
        t -= 1
    return t


def _silu(x):
    return x * jax.nn.sigmoid(x)


def _softplus(x):
    return jnp.maximum(x, 0.0) + jnp.log1p(jnp.exp(-jnp.abs(x)))


def _pack_halves(x):
    w = x.shape[1] // 2
    bits = lax.bitcast_convert_type(x.astype(BF16).astype(F32), jnp.int32)
    return bits[:, w:] | lax.shift_right_logical(bits[:, :w], jnp.int32(16))


def _unpack_halves(p):
    lo = lax.bitcast_convert_type(lax.shift_left(p, jnp.int32(16)), F32)
    hi = lax.bitcast_convert_type(p & jnp.int32(-65536), F32)
    return lo, hi


def _adaln_body(c_ref, w_ref, b_ref, o_ref):
    s = _silu(c_ref[...]).astype(BF16)
    o_ref[...] = jnp.dot(s, w_ref[...].astype(BF16), preferred_element_type=F32) + b_ref[...]


def adaln_all(cond, w_mod, b_mod):
    depth, d, n = w_mod.shape
    tn = 1024
    return pl.pallas_call(
        _adaln_body,
        grid=(depth, n // tn),
        in_specs=[pl.BlockSpec((SUBLANES, d), lambda l, j: (0, 0)),
                  pl.BlockSpec((None, d, tn), lambda l, j: (l, 0, j)),
                  pl.BlockSpec((None, 1, tn), lambda l, j: (l, 0, j))],
        out_specs=pl.BlockSpec((None, SUBLANES, tn), lambda l, j: (l, 0, j)),
        out_shape=jax.ShapeDtypeStruct((depth, SUBLANES, n), F32),
        compiler_params=_params(("arbitrary", "arbitrary")),
        name="adaln",
    )(cond, w_mod, b_mod.reshape(depth, 1, n))


def _norm_mod(x, g, shift, scale):
    y = x * lax.rsqrt(jnp.mean(x * x, axis=-1, keepdims=True) + NORM_EPS) * g
    return y * (1.0 + scale) + shift


def _norm_mod_body(x_ref, g_ref, sh_ref, sc_ref, o_ref):
    o_ref[...] = _norm_mod(x_ref[...], g_ref[...], sh_ref[...], sc_ref[...]).astype(o_ref.dtype)


def _norm_router_body(*refs, n_ctx):
    if n_ctx:
        (xc_ref, shc_ref, scc_ref, x_ref, sh_ref, sc_ref, g_ref, wr_ref, br_ref, o_ref, lg_ref) = refs
        is_ctx = pl.program_id(0) < n_ctx
        x = jnp.where(is_ctx, xc_ref[...], x_ref[...])
        sh = jnp.where(is_ctx, shc_ref[...], sh_ref[...])
        sc = jnp.where(is_ctx, scc_ref[...], sc_ref[...])
    else:
        x_ref, sh_ref, sc_ref, g_ref, wr_ref, br_ref, o_ref, lg_ref = refs
        x, sh, sc = x_ref[...], sh_ref[...], sc_ref[...]
    h = _norm_mod(x, g_ref[...], sh, sc)
    o_ref[...] = _pack_halves(h)
    lg_ref[...] = jnp.dot(h, wr_ref[...], preferred_element_type=F32, precision=HI) + br_ref[...]


def norm_mod(x, g, shift, scale):
    bx, lx, d = x.shape
    tm = _tile(lx, 256)
    row = pl.BlockSpec((None, tm, d), lambda b, i: (b, i, 0))
    vec = pl.BlockSpec((1, d), lambda b, i: (0, 0))
    mod = pl.BlockSpec((None, 1, d), lambda b, i: (b, 0, 0))
    return pl.pallas_call(
        _norm_mod_body, grid=(bx, lx // tm),
        in_specs=[row, vec, mod, mod], out_specs=row,
        out_shape=jax.ShapeDtypeStruct((bx, lx, d), BF16),
        compiler_params=_params(("arbitrary", "arbitrary")), name="norm_mod",
    )(x, g.reshape(1, d), shift, scale)


def norm_router(x, g, shift, scale, wr, br, ctx=None):
    bsz, l, d = x.shape
    nr = wr.shape[1]
    lc = ctx[0].shape[1] if ctx is not None else 0
    tm = _tile(l, 256) if ctx is None else _tile(lc, 256)
    assert l % tm == 0
    n_ctx = bsz * lc // tm
    n_lat = bsz * l // tm
    per_b = l // tm
    lat = lambda f: jnp.maximum(f - n_ctx, 0)
    vec = lambda w: pl.BlockSpec((1, w), lambda f: (0, 0))
    in_specs, args = [], []
    if ctx is not None:
        xc, shc, scc = ctx
        one = pl.BlockSpec((None, 1, d), lambda f: (0, 0, 0))
        in_specs += [pl.BlockSpec((tm, d), lambda f: (jnp.minimum(f, n_ctx - 1), 0)), one, one]
        args += [xc.reshape(bsz * lc, d), shc, scc]
    mod = pl.BlockSpec((None, 1, d), lambda f: (lat(f) // per_b, 0, 0))
    in_specs += [pl.BlockSpec((tm, d), lambda f: (lat(f), 0)), mod, mod, vec(d),
                 pl.BlockSpec((d, nr), lambda f: (0, 0)), vec(nr)]
    args += [x.reshape(bsz * l, d), shift, scale, g.reshape(1, d), wr, br]
    total = bsz * (lc + l)
    return pl.pallas_call(
        functools.partial(_norm_router_body, n_ctx=n_ctx), grid=(n_ctx + n_lat,), in_specs=in_specs,
        out_specs=[pl.BlockSpec((tm, d // 2), lambda f: (f, 0)), pl.BlockSpec((tm, nr), lambda f: (f, 0))],
        out_shape=[jax.ShapeDtypeStruct((total, d // 2), jnp.int32), jax.ShapeDtypeStruct((total, nr), F32)],
        compiler_params=_params(("arbitrary",)), name="norm_router",
    )(*args)


def _mm_body(a_ref, w_ref, o_ref):
    o_ref[...] = jnp.dot(a_ref[...], w_ref[...], preferred_element_type=F32).astype(o_ref.dtype)


def matmul(a, w, out_dtype=F32, tm_pref=1024, tn=512):
    m, k = a.shape
    n = w.shape[1]
    tm = _tile(m, tm_pref)
    return pl.pallas_call(
        _mm_body, grid=(m // tm, n // tn),
        in_specs=[pl.BlockSpec((tm, k), lambda i, j: (i, 0)),
                  pl.BlockSpec((k, tn), lambda i, j: (0, j))],
        out_specs=pl.BlockSpec((tm, tn), lambda i, j: (i, j)),
        out_shape=jax.ShapeDtypeStruct((m, n), out_dtype),
        compiler_params=_params(("arbitrary", "arbitrary")), name="matmul",
    )(a, w)


def _wout_body(ya_ref, yb_ref, yl_ref, w_ref, x_ref, g_ref, o_ref):
    acc = jnp.dot(ya_ref[...], w_ref[0:D_GLA, :], preferred_element_type=F32)
    acc += jnp.dot(yb_ref[...], w_ref[D_GLA:D_GLA + D_MLA, :], preferred_element_type=F32)
    acc += jnp.dot(yl_ref[...], w_ref[D_GLA + D_MLA:, :], preferred_element_type=F32)
    o_ref[...] = x_ref[...] + g_ref[...] * acc


def out_proj(ya, yb, yl, w, x, gate):
    bx, lx, d = x.shape
    tm = _tile(lx, 1024)
    tn = 512
    return pl.pallas_call(
        _wout_body, grid=(bx, lx // tm, d // tn),
        in_specs=[pl.BlockSpec((None, tm, D_GLA), lambda b, i, j: (b, i, 0)),
                  pl.BlockSpec((None, tm, D_MLA), lambda b, i, j: (b, i, 0)),
                  pl.BlockSpec((None, tm, D_LRU), lambda b, i, j: (b, i, 0)),
                  pl.BlockSpec((d, tn), lambda b, i, j: (0, j)),
                  pl.BlockSpec((None, tm, tn), lambda b, i, j: (b, i, j)),
                  pl.BlockSpec((None, 1, tn), lambda b, i, j: (b, 0, j))],
        out_specs=pl.BlockSpec((None, tm, tn), lambda b, i, j: (b, i, j)),
        out_shape=jax.ShapeDtypeStruct((bx, lx, d), F32),
        compiler_params=_params(("arbitrary", "arbitrary", "arbitrary")), name="out_proj",
    )(ya, yb, yl, w, x, gate)


def _gla_body(*refs, reverse, final, nchunk):
    if final:
        (q_ref, k_ref, v_ref, al_ref, wa_ref, ba_ref, s0_ref, of_ref, g_ref, ng_ref,
         o_ref, sfin_ref, s_scr) = refs
    else:
        q_ref, k_ref, v_ref, al_ref, wa_ref, ba_ref, s0_ref, o_ref, sfin_ref, s_scr = refs
    C = GLA_CHUNK

    @pl.when(pl.program_id(1) == 0)
    def _():
        s_scr[...] = s0_ref[...]

    r_i = lax.broadcasted_iota(jnp.int32, (C, C), 0)
    c_i = lax.broadcasted_iota(jnp.int32, (C, C), 1)
    keep = (c_i >= r_i) if reverse else (c_i <= r_i)
    tri = jnp.where(keep, 1.0, 0.0).astype(BF16)
    ref_row = C - 1 - C // 2 if reverse else C // 2
    last_row = 0 if reverse else C - 1

    order = range(nchunk - 1, -1, -1) if reverse else range(nchunk)
    for ci in order:
        rows = slice(ci * C, (ci + 1) * C)
        z = jnp.dot(al_ref[rows, :].astype(BF16), wa_ref[...], preferred_element_type=F32) + ba_ref[...]
        logd = (jnp.minimum(z, 0.0) - jnp.log1p(jnp.exp(-jnp.abs(z)))) * (1.0 / GLA_TAU)
        l_hi = logd.astype(BF16)
        rem = logd - l_hi.astype(F32)
        l_mid = rem.astype(BF16)
        l_lo = (rem - l_mid.astype(F32)).astype(BF16)
        b = (jnp.dot(tri, l_hi, preferred_element_type=F32) + jnp.dot(tri, l_mid, preferred_element_type=F32)
             + jnp.dot(tri, l_lo, preferred_element_type=F32))
        b_ref = b[ref_row:ref_row + 1, :]
        b_last = b[last_row:last_row + 1, :]
        q = q_ref[rows, :] * GLA_QSCALE
        k = k_ref[rows, :]
        qe = (q * jnp.exp(b - b_ref)).astype(BF16)
        ke = (k * jnp.exp(b_ref - b)).astype(BF16)
        qin = (q * jnp.exp(b)).astype(BF16)
        kst = k * jnp.exp(b_last - b)
        for h in range(GLA_HEADS):
            hs = slice(h * GLA_DK, (h + 1) * GLA_DK)
            vs = slice(h * GLA_DV, (h + 1) * GLA_DV)
            v = v_ref[rows, vs].astype(BF16)
            att = lax.dot_general(qe[:, hs], ke[:, hs], (((1,), (1,)), ((), ())),
                                  preferred_element_type=F32)
            att = jnp.where(keep, att, 0.0).astype(BF16)
            s = s_scr[h]
            o = (jnp.dot(att, v, preferred_element_type=F32)
                 + jnp.dot(qin[:, hs], s.astype(BF16), preferred_element_type=F32))
            d_col = jnp.exp(jnp.broadcast_to(b_last[:, hs], (GLA_DK, GLA_DK))).T
            s_scr[h] = (jnp.concatenate([d_col, d_col], axis=1) * s
                        + jnp.dot(kst[:, hs].T.astype(BF16), v, preferred_element_type=F32))
            if final:
                o = o + of_ref[rows, vs]
                y = o * lax.rsqrt(jnp.mean(o * o, axis=-1, keepdims=True) + NORM_EPS) * ng_ref[...]
                o_ref[rows, vs] = (y * _silu(g_ref[rows, vs])).astype(o_ref.dtype)
            else:
                o_ref[rows, vs] = o
    sfin_ref[...] = s_scr[...]


def gla_pass(p, wa, ba, s0, reverse, o_fwd=None, norm_g=None):
    bsz, l, _ = p.shape
    t = _tile(l, 256)
    nt = l // t
    final = o_fwd is not None

    def rix(i):
        return nt - 1 - i if reverse else i

    def col(width, off):
        return pl.BlockSpec((None, t, width), lambda b, i: (b, rix(i), off // width))

    hq = GLA_HEADS * GLA_DK
    state = pl.BlockSpec((None, GLA_HEADS, GLA_DK, GLA_DV), lambda b, i: (b, 0, 0, 0))
    in_specs = [col(hq, COL_Q), col(hq, COL_K), col(D_GLA, COL_V), col(AL_W, COL_AL),
                pl.BlockSpec((AL_W, hq), lambda b, i: (0, 0)),
                pl.BlockSpec((1, hq), lambda b, i: (0, 0)), state]
    args = [p, p, p, p, wa, ba, s0]
    orow = pl.BlockSpec((None, t, D_GLA), lambda b, i: (b, rix(i), 0))
    if final:
        in_specs += [orow, col(D_GLA, COL_G), pl.BlockSpec((1, GLA_DV), lambda b, i: (0, 0))]
        args += [o_fwd, p, norm_g.reshape(1, GLA_DV)]
    return pl.pallas_call(
        functools.partial(_gla_body, reverse=reverse, final=final, nchunk=t // GLA_CHUNK),
        grid=(bsz, nt), in_specs=in_specs, out_specs=[orow, state],
        out_shape=[jax.ShapeDtypeStruct((bsz, l, D_GLA), BF16 if final else F32),
                   jax.ShapeDtypeStruct((bsz, GLA_HEADS, GLA_DK, GLA_DV), F32)],
        scratch_shapes=[pltpu.VMEM((GLA_HEADS, GLA_DK, GLA_DV), F32)],
        compiler_params=_params(("arbitrary", "arbitrary")), name="gla",
    )(*args)


def _lru_body(*refs, reverse, final, t, nt):
    if final:
        (x_ref, xp_ref, xn_ref, cw_ref, cb_ref, wg_ref, bg_ref, lam_ref, h0_ref, hf_ref, gb_ref,
         o_ref, hfin_ref, h_scr, hbuf) = refs
    else:
        (x_ref, xp_ref, xn_ref, cw_ref, cb_ref, wg_ref, bg_ref, lam_ref, h0_ref,
         o_ref, hfin_ref, h_scr) = refs
        hbuf = o_ref
    i = pl.program_id(1)
    ti = nt - 1 - i if reverse else i

    @pl.when(i == 0)
    def _():
        h_scr[...] = h0_ref[...]

    ngroup = t // SUBLANES
    grp = lambda v: v.reshape(v.shape[0] // SUBLANES, SUBLANES, v.shape[1])
    sub = lax.broadcasted_iota(jnp.int32, (1, SUBLANES, 1), 1)
    x3 = grp(x_ref[...])
    xp3 = grp(jnp.where(ti == 0, 0.0, xp_ref[...]))
    xn3 = grp(jnp.where(ti == nt - 1, 0.0, xn_ref[...]))

    def row_shift(k):
        rot = pltpu.roll(x3, k % SUBLANES, axis=1)
        if k > 0:
            nb_rot = jnp.concatenate([pltpu.roll(xp3, k % SUBLANES, axis=1), rot[:-1]], axis=0)
            return jnp.where(sub >= k, rot, nb_rot)
        nb_rot = jnp.concatenate([rot[1:], pltpu.roll(xn3, k % SUBLANES, axis=1)], axis=0)
        return jnp.where(sub < SUBLANES + k, rot, nb_rot)

    cw = cw_ref[...]
    xc = (cb_ref[...] + row_shift(2) * cw[0:1, :] + row_shift(1) * cw[1:2, :]
          + x3 * cw[2:3, :] + row_shift(-1) * cw[3:4, :]).reshape(t, D_LRU)

    zs = []
    for n in range(LRU_BLOCKS):
        zs.append(jnp.dot(xc[:, n * LRU_BW:(n + 1) * LRU_BW].astype(BF16), wg_ref[n],
                          preferred_element_type=F32))
    z_r = jnp.concatenate([z[:, :LRU_BW] for z in zs], axis=1) + bg_ref[0:1, :]
    z_i = jnp.concatenate([z[:, LRU_BW:] for z in zs], axis=1) + bg_ref[1:2, :]
    log_a = -LRU_C * jax.nn.sigmoid(z_r) * _softplus(-lam_ref[...])
    a = jnp.exp(log_a)
    th = jnp.tanh(log_a)
    one_minus_a2 = -2.0 * th / (1.0 - th)
    u = jnp.sqrt(one_minus_a2) * jax.nn.sigmoid(z_i) * xc

    a, u = grp(a), grp(u)
    s = 1
    while s < SUBLANES:
        if reverse:
            ok = sub < SUBLANES - s
            a_s = pltpu.roll(a, SUBLANES - s, axis=1)
            u_s = pltpu.roll(u, SUBLANES - s, axis=1)
        else:
            ok = sub >= s
            a_s = pltpu.roll(a, s, axis=1)
            u_s = pltpu.roll(u, s, axis=1)
        u = jnp.where(ok, a * u_s + u, u)
        a = jnp.where(ok, a * a_s, a)
        s *= 2
    edge = 0 if reverse else SUBLANES - 1
    h_prev = h_scr[...]
    for gi in (range(ngroup - 1, -1, -1) if reverse else range(ngroup)):
        rows = slice(gi * SUBLANES, (gi + 1) * SUBLANES)
        hg = a[gi] * h_prev + u[gi]
        h_prev = hg[edge:edge + 1, :]
        hbuf[rows, :] = hg
    h_scr[...] = h_prev
    hfin_ref[...] = h_prev
    if final:
        g = gb_ref[...]
        gelu = 0.5 * g * (1.0 + jnp.tanh(0.7978845608028654 * (g + 0.044715 * g * g * g)))
        o_ref[...] = (gelu * (hf_ref[...] + hbuf[...])).astype(o_ref.dtype)


def lru_pass(p, cw, cb, wg, bg, lam, h0, reverse, h_fwd=None):
    bsz, l, _ = p.shape
    t = _tile(l, 256)
    nt = l // t
    tb = t // SUBLANES
    nb8 = l // SUBLANES
    final = h_fwd is not None
    xoff = COL_XB // D_LRU

    def rix(i):
        return nt - 1 - i if reverse else i

    vec = lambda r: pl.BlockSpec((r, D_LRU), lambda b, i: (0, 0))
    one = pl.BlockSpec((None, 1, D_LRU), lambda b, i: (b, 0, 0))
    orow = pl.BlockSpec((None, t, D_LRU), lambda b, i: (b, rix(i), 0))
    in_specs = [pl.BlockSpec((None, t, D_LRU), lambda b, i: (b, rix(i), xoff)),
                pl.BlockSpec((None, SUBLANES, D_LRU),
                             lambda b, i: (b, jnp.maximum(rix(i) * tb - 1, 0), xoff)),
                pl.BlockSpec((None, SUBLANES, D_LRU),
                             lambda b, i: (b, jnp.minimum((rix(i) + 1) * tb, nb8 - 1), xoff)),
                vec(LRU_CONV), vec(1),
                pl.BlockSpec((LRU_BLOCKS, LRU_BW, 2 * LRU_BW), lambda b, i: (0, 0, 0)),
                vec(2), vec(1), one]
    args = [p, p, p, cw, cb.reshape(1, D_LRU), wg, bg, lam.reshape(1, D_LRU), h0]
    if final:
        in_specs += [orow, pl.BlockSpec((None, t, D_LRU), lambda b, i: (b, rix(i), COL_GB // D_LRU))]
        args += [h_fwd, p]
    return pl.pallas_call(
        functools.partial(_lru_body, reverse=reverse, final=final, t=t, nt=nt),
        grid=(bsz, nt), in_specs=in_specs, out_specs=[orow, one],
        out_shape=[jax.ShapeDtypeStruct((bsz, l, D_LRU), BF16 if final else F32),
                   jax.ShapeDtypeStruct((bsz, 1, D_LRU), F32)],
        scratch_shapes=[pltpu.VMEM((1, D_LRU), F32)] + ([pltpu.VMEM((t, D_LRU), F32)] if final else []),
        compiler_params=_params(("arbitrary", "arbitrary")), name="lru",
    )(*args)


def _rms(x, width):
    return x * lax.rsqrt(jnp.sum(x * x, axis=-1, keepdims=True) * (1.0 / width) + NORM_EPS)


def _rope(r, cos, sin):
    lane = lax.broadcasted_iota(jnp.int32, r.shape, 1)
    first = (lane % 32) < 16
    rot = jnp.where(first, pltpu.roll(r, LANES - 16, axis=1), pltpu.roll(r, 16, axis=1))
    return r * cos + rot * sin


def _qproj_body(x_ref, gn_ref, w_ref, hg_ref, cos_ref, sin_ref, o_ref, xn_scr):
    @pl.when(pl.program_id(2) == 0)
    def _():
        xn_scr[...] = (_rms(x_ref[...], MLA_Q_RANK) * gn_ref[...]).astype(BF16)

    a = jnp.dot(xn_scr[...], w_ref[...], preferred_element_type=F32)
    for hh in range(HEADS_PER_STEP):
        c0 = hh * HEAD_W
        qn = _rms(a[:, c0:c0 + MLA_NOPE], MLA_NOPE) * hg_ref[:, :MLA_NOPE]
        qr = _rms(a[:, c0 + MLA_NOPE:c0 + HEAD_W], MLA_ROPE) * hg_ref[:, MLA_NOPE:]
        qr = _rope(qr, cos_ref[...], sin_ref[...])
        o_ref[hh] = (jnp.concatenate([qn, qr], axis=1) * (MLA_SCALE * LOG2E)).astype(o_ref.dtype)


def q_proj(p, gn, w, hg, cos, sin):
    bsz, l, _ = p.shape
    tm = _tile(l, 512)
    hw = HEADS_PER_STEP * HEAD_W
    return pl.pallas_call(
        _qproj_body, grid=(bsz, l // tm, MLA_HEADS // HEADS_PER_STEP),
        in_specs=[pl.BlockSpec((None, tm, QDN_W), lambda b, i, h: (b, i, COL_QDN // QDN_W)),
                  pl.BlockSpec((1, QDN_W), lambda b, i, h: (0, 0)),
                  pl.BlockSpec((QDN_W, hw), lambda b, i, h: (0, h)),
                  pl.BlockSpec((1, HEAD_W), lambda b, i, h: (0, 0)),
                  pl.BlockSpec((tm, LANES), lambda b, i, h: (i, 0)),
                  pl.BlockSpec((tm, LANES), lambda b, i, h: (i, 0))],
        out_specs=pl.BlockSpec((None, HEADS_PER_STEP, tm, HEAD_W), lambda b, i, h: (b, h, i, 0)),
        out_shape=jax.ShapeDtypeStruct((bsz, MLA_HEADS, l, HEAD_W), BF16),
        scratch_shapes=[pltpu.VMEM((tm, QDN_W), BF16)],
        compiler_params=_params(("arbitrary", "arbitrary", "arbitrary")), name="q_proj",
    )(p, gn.reshape(1, QDN_W), w, hg, cos, sin)


def _kvproj_body(x_ref, kr_ref, gn_ref, w_ref, hg_ref, cos_ref, sin_ref, k_ref, v_ref, xn_scr, kr_scr):
    @pl.when(pl.program_id(2) == 0)
    def _():
        xn_scr[...] = (_rms(x_ref[...], MLA_KV_RANK) * gn_ref[...]).astype(BF16)
        kr = _rms(kr_ref[...], MLA_ROPE) * hg_ref[:, MLA_NOPE:]
        kr_scr[...] = _rope(kr, cos_ref[...], sin_ref[...]).astype(BF16)

    a = jnp.dot(xn_scr[...], w_ref[...], preferred_element_type=F32)
    for hh in range(HEADS_PER_STEP):
        c0 = hh * HEAD_W
        kn = _rms(a[:, c0:c0 + MLA_NOPE], MLA_NOPE) * hg_ref[:, :MLA_NOPE]
        k_ref[hh, :, :MLA_NOPE] = kn.astype(k_ref.dtype)
        k_ref[hh, :, MLA_NOPE:] = kr_scr[...]
        v_ref[hh, :MLA_V, :] = a[:, c0 + MLA_NOPE:c0 + HEAD_W].T.astype(v_ref.dtype)
        v_ref[hh, MLA_V:, :] = jnp.ones((VT_ROWS - MLA_V, a.shape[0]), v_ref.dtype)


def kv_proj(p, gn, w, hg, cos, sin):
    bsz, l, _ = p.shape
    tm = _tile(l, 512)
    hw = HEADS_PER_STEP * HEAD_W
    return pl.pallas_call(
        _kvproj_body, grid=(bsz, l // tm, MLA_HEADS // HEADS_PER_STEP),
        in_specs=[pl.BlockSpec((None, tm, MLA_KV_RANK), lambda b, i, h: (b, i, COL_KVDN // MLA_KV_RANK)),
                  pl.BlockSpec((None, tm, KR_W), lambda b, i, h: (b, i, COL_KR // KR_W)),
                  pl.BlockSpec((1, MLA_KV_RANK), lambda b, i, h: (0, 0)),
                  pl.BlockSpec((MLA_KV_RANK, hw), lambda b, i, h: (0, h)),
                  pl.BlockSpec((1, HEAD_W), lambda b, i, h: (0, 0)),
                  pl.BlockSpec((tm, LANES), lambda b, i, h: (i, 0)),
                  pl.BlockSpec((tm, LANES), lambda b, i, h: (i, 0))],
        out_specs=[pl.BlockSpec((None, HEADS_PER_STEP, tm, HEAD_W), lambda b, i, h: (b, h, i, 0)),
                   pl.BlockSpec((None, HEADS_PER_STEP, VT_ROWS, tm), lambda b, i, h: (b, h, 0, i))],
        out_shape=[jax.ShapeDtypeStruct((bsz, MLA_HEADS, l, HEAD_W), BF16),
                   jax.ShapeDtypeStruct((bsz, MLA_HEADS, VT_ROWS, l), BF16)],
        scratch_shapes=[pltpu.VMEM((tm, MLA_KV_RANK), BF16), pltpu.VMEM((tm, KR_W), BF16)],
        compiler_params=_params(("arbitrary", "arbitrary", "arbitrary")), name="kv_proj",
    )(p, p, gn.reshape(1, MLA_KV_RANK), w, hg, cos, sin)


def _scores_t(k, q):
    return lax.dot_general(k, q, (((1,), (1,)), ((), ())), preferred_element_type=F32)


def _attn_body(*refs, tk, n_lat):
    if n_lat:
        q_ref, kc_ref, vc_ref, k_ref, v_ref, o_ref, m_scr, acc_scr, s_scr = refs
    else:
        q_ref, kc_ref, vc_ref, o_ref = refs

    def qk(slot, off):
        s_scr[slot] = _scores_t(k_ref[pl.ds(off, tk), :], q_ref[...])

    if n_lat:
        qk(0, 0)

    sc = _scores_t(kc_ref[...], q_ref[...])
    m0 = jnp.max(sc, axis=0, keepdims=True)
    acc = jnp.dot(vc_ref[...], jnp.exp2(sc - m0).astype(BF16), preferred_element_type=F32)

    if n_lat:
        m_scr[...] = m0
        acc_scr[...] = acc

        def consume(slot, off):
            s = s_scr[slot]
            m = m_scr[...]
            m_new = jnp.maximum(m, jnp.max(s, axis=0, keepdims=True))
            p = jnp.exp2(s - m_new).astype(BF16)
            m_scr[...] = m_new
            acc_scr[...] = (jnp.exp2(m - m_new) * acc_scr[...]
                            + jnp.dot(v_ref[:, pl.ds(off, tk)], p, preferred_element_type=F32))

        def pair(jj, carry):
            off = pl.multiple_of(jj * (2 * tk), 2 * tk)
            qk(1, off + tk)
            consume(0, off)
            qk(0, off + 2 * tk)
            consume(1, off + tk)
            return carry

        lax.fori_loop(0, n_lat // 2 - 1, pair, 0)
        off = (n_lat - 2) * tk
        qk(1, off + tk)
        consume(0, off)
        consume(1, off + tk)
        acc = acc_scr[...]
    o_ref[...] = (acc[:MLA_V] / acc[MLA_V:MLA_V + 1]).T.astype(o_ref.dtype)


def attention(q, kc, vc, k=None, v=None):
    bsz, nh, lq, _ = q.shape
    lc = kc.shape[2]
    tq = _tile(lq, 1024)
    keys = lambda n: pl.BlockSpec((None, None, n, HEAD_W), lambda b, h, i: (b, h, 0, 0))
    vals = lambda n: pl.BlockSpec((None, None, VT_ROWS, n), lambda b, h, i: (b, h, 0, 0))
    in_specs = [pl.BlockSpec((None, None, tq, HEAD_W), lambda b, h, i: (b, h, i, 0)), keys(lc), vals(lc)]
    args = [q, kc, vc]
    tk = n_lat = 0
    scratch = []
    if k is not None:
        lk = k.shape[2]
        tk = _tile(lk // 2, 1024)
        assert tk % LANES == 0 and lk % (2 * tk) == 0
        n_lat = lk // tk
        in_specs += [keys(lk), vals(lk)]
        args += [k, v]
        scratch = [pltpu.VMEM((1, tq), F32), pltpu.VMEM((VT_ROWS, tq), F32), pltpu.VMEM((2, tk, tq), F32)]
    return pl.pallas_call(
        functools.partial(_attn_body, tk=tk, n_lat=n_lat),
        grid=(bsz, nh, lq // tq), in_specs=in_specs,
        out_specs=pl.BlockSpec((None, tq, MLA_V), lambda b, h, i: (b, i, h)),
        out_shape=jax.ShapeDtypeStruct((bsz, lq, nh * MLA_V), BF16),
        scratch_shapes=scratch,
        compiler_params=_params(("arbitrary", "arbitrary", "arbitrary")), name="attention",
    )(*args)


def _row_copy(src_hbm, row, buf, slot, r, sem):
    return pltpu.make_async_copy(src_hbm.at[pl.ds(row, 1), :], buf.at[slot, pl.ds(r, 1), :], sem.at[slot])


def _moe_body(be_ref, nu_ref, first_ref, tok_ref, h_hbm, w13_ref, w2_ref, o_ref, xbuf, sem, *, n_assign):
    j = pl.program_id(0)
    nu = nu_ref[0]
    slot = j % MOE_SLOTS
    ahead = MOE_SLOTS - 1

    def rows(blk, slot, start):
        base = first_ref[blk]

        def body(r, carry):
            tok = tok_ref[jnp.minimum(base + r, n_assign - 1)]
            cp = _row_copy(h_hbm, tok, xbuf, slot, r, sem)
            if start:
                cp.start()
            else:
                cp.wait()
            return carry

        lax.fori_loop(0, MOE_BLOCK, body, 0, unroll=8)

    for first in range(ahead):
        @pl.when((j == 0) & (first < nu))
        def _():
            rows(first, first, True)

    @pl.when(j + ahead < nu)
    def _():
        rows(j + ahead, (j + ahead) % MOE_SLOTS, True)

    @pl.when(j < nu)
    def _():
        rows(j, slot, False)
        x_lo, x_hi = _unpack_halves(xbuf[slot])
        half = w13_ref.shape[0] // 2
        hid = (jnp.dot(x_lo.astype(BF16), w13_ref[:half, :].astype(BF16), preferred_element_type=F32)
               + jnp.dot(x_hi.astype(BF16), w13_ref[half:, :].astype(BF16), preferred_element_type=F32))
        act = (_silu(hid[:, :D_EXPERT]) * hid[:, D_EXPERT:]).astype(BF16)
        o_ref[...] = _pack_halves(jnp.dot(act, w2_ref[...].astype(BF16), preferred_element_type=F32))

    @pl.when(j >= nu)
    def _():
        o_ref[...] = jnp.zeros_like(o_ref)


def moe_experts(h, tok_sorted, blk_first, blk_exp, n_used, w13, w2, layer):
    m, dp = h.shape
    d = 2 * dp
    nb = blk_first.shape[0]
    grid_spec = pltpu.PrefetchScalarGridSpec(
        num_scalar_prefetch=4, grid=(nb,),
        in_specs=[pl.BlockSpec(memory_space=pl.ANY),
                  pl.BlockSpec((None, None, d, 2 * D_EXPERT), lambda j, be, nu, fi, tok: (layer, be[j], 0, 0)),
                  pl.BlockSpec((None, None, D_EXPERT, d), lambda j, be, nu, fi, tok: (layer, be[j], 0, 0))],
        out_specs=pl.BlockSpec((MOE_BLOCK, dp), lambda j, be, nu, fi, tok: (j, 0)),
        scratch_shapes=[pltpu.VMEM((MOE_SLOTS, MOE_BLOCK, dp), jnp.int32),
                        pltpu.SemaphoreType.DMA((MOE_SLOTS,))])
    return pl.pallas_call(
        functools.partial(_moe_body, n_assign=tok_sorted.shape[0]), grid_spec=grid_spec,
        out_shape=jax.ShapeDtypeStruct((nb * MOE_BLOCK, dp), jnp.int32),
        compiler_params=_params(("arbitrary",)), name="moe_experts",
    )(blk_exp, n_used, blk_first, tok_sorted, h, w13, w2)


def _combine_body(dest_ref, out_hbm, x_ref, g_ref, tg_ref, *rest, tok_off, t, n_tiles, with_norm):
    if with_norm:
        ng_ref, sh_ref, sc_ref, o_ref, h_ref, buf_a, buf_b, sem_a, sem_b = rest
    else:
        o_ref, buf_a, buf_b, sem_a, sem_b = rest
    f = pl.program_id(0) * pl.num_programs(1) + pl.program_id(1)
    slot = f % 2

    def rows(tile, slot, start):
        base = (tok_off + tile * t) * TOP_K

        def body(r, carry):
            ca = _row_copy(out_hbm, dest_ref[base + TOP_K * r], buf_a, slot, r, sem_a)
            cb = _row_copy(out_hbm, dest_ref[base + TOP_K * r + 1], buf_b, slot, r, sem_b)
            if start:
                ca.start()
                cb.start()
            else:
                ca.wait()
                cb.wait()
            return carry

        lax.fori_loop(0, t, body, 0, unroll=8)

    @pl.when(f == 0)
    def _():
        rows(0, 0, True)

    @pl.when(f + 1 < n_tiles)
    def _():
        rows(f + 1, 1 - slot, True)

    rows(f, slot, False)
    d = x_ref.shape[1]
    half = d // 2
    tg0, tg1 = tg_ref[:, :LANES], tg_ref[:, LANES:]
    ss = jnp.zeros((t, LANES), F32)
    for c in range(half // LANES):
        wcols = slice(c * LANES, (c + 1) * LANES)
        a_lo, a_hi = _unpack_halves(buf_a[slot, :, wcols])
        b_lo, b_hi = _unpack_halves(buf_b[slot, :, wcols])
        for cols, ya, yb in ((wcols, a_lo, b_lo),
                             (slice(half + c * LANES, half + (c + 1) * LANES), a_hi, b_hi)):
            y = x_ref[:, cols] + g_ref[:, cols] * (tg0 * ya + tg1 * yb)
            o_ref[:, cols] = y
            ss = ss + y * y
    if with_norm:
        rs = lax.rsqrt(jnp.sum(ss, axis=-1, keepdims=True) * (1.0 / d) + NORM_EPS)
        rs = jnp.broadcast_to(rs, (t, LANES))
        for c in range(d // LANES):
            cols = slice(c * LANES, (c + 1) * LANES)
            h_ref[:, cols] = (o_ref[:, cols] * rs * ng_ref[:, cols] * (1.0 + sc_ref[:, cols])
                              + sh_ref[:, cols]).astype(h_ref.dtype)


def moe_combine(x, gate, out, dest, tok_gate, tok_off, norm=None):
    bx, lx, d = x.shape
    t = _tile(lx, 128)
    nt = lx // t
    tg = jnp.repeat(tok_gate[tok_off:tok_off + bx * lx], LANES, axis=1).reshape(bx, lx, TOP_K * LANES)
    row = pl.BlockSpec((None, t, d), lambda b, i, dst: (b, i, 0))
    mod = pl.BlockSpec((None, 1, d), lambda b, i, dst: (b, 0, 0))
    in_specs = [pl.BlockSpec(memory_space=pl.ANY), row, mod,
                pl.BlockSpec((None, t, TOP_K * LANES), lambda b, i, dst: (b, i, 0))]
    args = [dest, out, x, gate, tg]
    out_specs, out_shape = row, jax.ShapeDtypeStruct((bx, lx, d), F32)
    if norm is not None:
        in_specs += [pl.BlockSpec((1, d), lambda b, i, dst: (0, 0)), mod, mod]
        args += [norm[0].reshape(1, d), norm[1], norm[2]]
        out_specs, out_shape = [row, row], [out_shape, jax.ShapeDtypeStruct((bx, lx, d), BF16)]
    grid_spec = pltpu.PrefetchScalarGridSpec(
        num_scalar_prefetch=1, grid=(bx, nt), in_specs=in_specs, out_specs=out_specs,
        scratch_shapes=[pltpu.VMEM((2, t, d // 2), jnp.int32), pltpu.VMEM((2, t, d // 2), jnp.int32),
                        pltpu.SemaphoreType.DMA((2,)), pltpu.SemaphoreType.DMA((2,))])
    return pl.pallas_call(
        functools.partial(_combine_body, tok_off=tok_off, t=t, n_tiles=bx * nt, with_norm=norm is not None),
        grid_spec=grid_spec, out_shape=out_shape,
        compiler_params=_params(("arbitrary", "arbitrary")), name="moe_combine",
    )(*args)


def moe_route(h, logits, w13, w2, layer):
    m = h.shape[0]
    g_logit = logits[:, :N_GROUPS]
    e_logit = logits[:, N_GROUPS:N_GROUPS + N_EXPERTS].reshape(m, N_GROUPS, EXPERTS_PER_GROUP)
    g_idx = jnp.argmax(g_logit, axis=-1)
    p_grp = jnp.take_along_axis(jax.nn.softmax(g_logit, axis=-1), g_idx[:, None], axis=1)
    e_sel = jnp.take_along_axis(e_logit, g_idx[:, None, None], axis=1)[:, 0]
    top_v, top_i = lax.top_k(e_sel, TOP_K)
    gate = (p_grp * jax.nn.softmax(top_v, axis=-1)).reshape(-1)
    e_flat = (g_idx[:, None] * EXPERTS_PER_GROUP + top_i).reshape(-1).astype(jnp.int32)

    a = m * TOP_K
    nb = -(-a // MOE_BLOCK) + N_EXPERTS
    order = jnp.argsort(e_flat).astype(jnp.int32)
    rank = jnp.argsort(order).astype(jnp.int32)
    e_s = e_flat[order]
    experts = jnp.arange(N_EXPERTS, dtype=jnp.int32)
    start = jnp.searchsorted(e_s, experts, side='left').astype(jnp.int32)
    counts = jnp.searchsorted(e_s, experts, side='right').astype(jnp.int32) - start
    padded = -(-counts // MOE_BLOCK) * MOE_BLOCK
    pad_end = jnp.cumsum(padded)
    pad_start = pad_end - padded
    blk_ids = jnp.arange(nb, dtype=jnp.int32)
    blk_raw = jnp.searchsorted(pad_end, blk_ids * MOE_BLOCK, side='right')
    blk_exp = jnp.minimum(blk_raw, N_EXPERTS - 1).astype(jnp.int32)
    n_used = (pad_end[-1] // MOE_BLOCK).astype(jnp.int32).reshape(1)
    blk_first = jnp.where(blk_raw < N_EXPERTS,
                          start[blk_exp] + blk_ids * MOE_BLOCK - pad_start[blk_exp], 0).astype(jnp.int32)
    tok_sorted = order // TOP_K
    dest = (pad_start[e_flat] + rank - start[e_flat]).astype(jnp.int32)

    out = moe_experts(h, tok_sorted, blk_first, blk_exp, n_used, w13, w2, layer)
    return out, dest, gate.reshape(m, TOP_K)


def _prep_w_in(w):
    sizes = (512, 512, 1024, 1024, 32, 768, 512, 64, 1024, 1024)
    offs = [0]
    for s in sizes:
        offs.append(offs[-1] + s)
    q, k, v, g, al, qdn, kvdn, kr, xb, gb = [w[:, offs[i]:offs[i + 1]] for i in range(10)]
    zpad = lambda t, n: jnp.pad(t, ((0, 0), (0, n - t.shape[1])))
    return jnp.concatenate([qdn, zpad(al, AL_W), zpad(kr, KR_W), v, g, xb, gb, q, k, kvdn],
                           axis=1).astype(BF16)


def _prep_w_uq(w):
    w = w.reshape(MLA_Q_RANK, MLA_HEADS, MLA_NOPE + MLA_ROPE)
    w = jnp.pad(w, ((0, 0), (0, 0), (0, HEAD_W - MLA_NOPE - MLA_ROPE)))
    return w.reshape(MLA_Q_RANK, MLA_HEADS * HEAD_W).astype(BF16)


def _head_gain(g):
    return jnp.pad(g, (0, HEAD_W - g.shape[0])).reshape(1, HEAD_W)


def _rope_tables(rows):
    n = MLA_ROPE // 4
    inv = ROPE_BASE ** (-jnp.arange(n, dtype=F32) / n)
    r = jnp.repeat(jnp.arange(rows, dtype=F32), GRID_W)
    c = jnp.tile(jnp.arange(GRID_W, dtype=F32), rows)
    ar, ac = r[:, None] * inv, c[:, None] * inv
    l = ar.shape[0]
    cos = jnp.concatenate([jnp.cos(ar), jnp.cos(ar), jnp.cos(ac), jnp.cos(ac),
                           jnp.ones((l, LANES - MLA_ROPE), F32)], axis=1)
    sin = jnp.concatenate([-jnp.sin(ar), jnp.sin(ar), -jnp.sin(ac), jnp.sin(ac),
                           jnp.zeros((l, LANES - MLA_ROPE), F32)], axis=1)
    return cos, sin


def _gla_gate_w(w_a2, b_a, d):
    hq = GLA_HEADS * GLA_DK
    w = jnp.zeros((AL_W, hq), F32).at[d * GLA_GATE_RANK:(d + 1) * GLA_GATE_RANK].set(w_a2[d])
    return w.astype(BF16), b_a[d].reshape(1, hq)


def _lru_gate_w(w_gate, d):
    return jnp.concatenate([w_gate[d, 0], w_gate[d, 1]], axis=-1).astype(BF16)


def _layer(x, xc, mods, rope, prm, moe_w13, moe_w2, layer, last, h_pre=None, nxt=None):
    (norm1_g, norm2_g, w_in, gla_w_a2, gla_b_a, gla_norm_g,
     mla_q_norm_g, mla_kv_norm_g, mla_w_uq, mla_w_ukv, mla_q_head_g, mla_k_head_g,
     lru_conv_w, lru_conv_b, lru_w_gate, lru_b_gate, lru_lambda, w_out,
     moe_w_grp, moe_b_grp, moe_w_exp, moe_b_exp) = prm
    bsz, l, d = x.shape
    lc = xc.shape[1]
    ctx_out = not last
    lat = lambda k: mods[:bsz, k].reshape(bsz, 1, d)
    ctx = lambda k: jnp.broadcast_to(mods[bsz, k].reshape(1, 1, d), (bsz, 1, d))

    w_in_p = _prep_w_in(w_in)
    if h_pre is None:
        h = norm_mod(x, norm1_g, lat(0), lat(1))
        hc = norm_mod(xc, norm1_g, ctx(0), ctx(1))
    else:
        h, hc = h_pre
    p = matmul(h.reshape(bsz * l, d), w_in_p).reshape(bsz, l, D_INP)
    pc = matmul(hc.reshape(bsz * lc, d), w_in_p).reshape(bsz, lc, D_INP)

    waf, baf = _gla_gate_w(gla_w_a2, gla_b_a, 0)
    wab, bab = _gla_gate_w(gla_w_a2, gla_b_a, 1)
    s_zero = jnp.zeros((bsz, GLA_HEADS, GLA_DK, GLA_DV), F32)
    ocf, s_f = gla_pass(pc, waf, baf, s_zero, False)
    of, _ = gla_pass(p, waf, baf, s_f, False)
    if ctx_out:
        yac, s_b = gla_pass(pc, wab, bab, s_zero, True, ocf, gla_norm_g)
    else:
        _, s_b = gla_pass(pc, wab, bab, s_zero, True)
    ya, _ = gla_pass(p, wab, bab, s_b, True, of, gla_norm_g)

    wgf, wgb = _lru_gate_w(lru_w_gate, 0), _lru_gate_w(lru_w_gate, 1)
    h_zero = jnp.zeros((bsz, 1, D_LRU), F32)
    lru = functools.partial(lru_pass, cw=lru_conv_w, cb=lru_conv_b)
    hcf, h0f = lru(pc, wg=wgf, bg=lru_b_gate[0], lam=lru_lambda[0], h0=h_zero, reverse=False)
    hf, _ = lru(p, wg=wgf, bg=lru_b_gate[0], lam=lru_lambda[0], h0=h0f, reverse=False)
    if ctx_out:
        ylc, h0b = lru(pc, wg=wgb, bg=lru_b_gate[1], lam=lru_lambda[1], h0=h_zero, reverse=True, h_fwd=hcf)
    else:
        _, h0b = lru(pc, wg=wgb, bg=lru_b_gate[1], lam=lru_lambda[1], h0=h_zero, reverse=True)
    yl, _ = lru(p, wg=wgb, bg=lru_b_gate[1], lam=lru_lambda[1], h0=h0b, reverse=True, h_fwd=hf)

    cos, sin = rope
    cos_c = jnp.ones((lc, LANES), F32)
    sin_c = jnp.zeros((lc, LANES), F32)
    w_uq = _prep_w_uq(mla_w_uq)
    w_ukv = mla_w_ukv.astype(BF16)
    qg, kg = _head_gain(mla_q_head_g), _head_gain(mla_k_head_g)
    kc, vc = kv_proj(pc, mla_kv_norm_g, w_ukv, kg, cos_c, sin_c)
    kl, vl = kv_proj(p, mla_kv_norm_g, w_ukv, kg, cos, sin)
    ql = q_proj(p, mla_q_norm_g, w_uq, qg, cos, sin)
    yb = attention(ql, kc, vc, kl, vl)

    w_out_b = w_out.astype(BF16)
    x = out_proj(ya, yb, yl, w_out_b, x, lat(2))
    if ctx_out:
        qc = q_proj(pc, mla_q_norm_g, w_uq, qg, cos_c, sin_c)
        ybc = attention(qc, kc, vc)
        xc = out_proj(yac, ybc, ylc, w_out_b, xc, ctx(2))

    nr = LANES
    wr = jnp.pad(jnp.concatenate([moe_w_grp, moe_w_exp], axis=1), ((0, 0), (0, nr - N_GROUPS - N_EXPERTS)))
    br = jnp.pad(jnp.concatenate([moe_b_grp, moe_b_exp]), (0, nr - N_GROUPS - N_EXPERTS)).reshape(1, nr)
    ctx1 = lambda k: mods[bsz, k].reshape(1, 1, d)
    h2, lg = norm_router(x, norm2_g, lat(3), lat(4), wr, br,
                         ctx=(xc, ctx1(3), ctx1(4)) if ctx_out else None)
    out, dest, tok_gate = moe_route(h2, lg, moe_w13, moe_w2, layer)
    h_next = None
    if ctx_out:
        g_next, m_next = nxt
        nlat = lambda k: m_next[:bsz, k].reshape(bsz, 1, d)
        nctx = lambda k: jnp.broadcast_to(m_next[bsz, k].reshape(1, 1, d), (bsz, 1, d))
        xc, hc_next = moe_combine(xc, ctx(5), out, dest, tok_gate, 0, norm=(g_next, nctx(0), nctx(1)))
        x, hl_next = moe_combine(x, lat(5), out, dest, tok_gate, bsz * lc, norm=(g_next, nlat(0), nlat(1)))
        h_next = (hl_next, hc_next)
    else:
        x = moe_combine(x, lat(5), out, dest, tok_gate, 0)
    return x, xc, h_next


def kernel(x, c, ctx, c_ctx, w_mod, b_mod, norm1_g, norm2_g, w_in, gla_w_a2, gla_b_a, gla_norm_g,
           mla_q_norm_g, mla_kv_norm_g, mla_w_uq, mla_w_ukv, mla_q_head_g, mla_k_head_g,
           lru_conv_w, lru_conv_b, lru_w_gate, lru_b_gate, lru_lambda, w_out,
           moe_w_grp, moe_b_grp, moe_w_exp, moe_b_exp, moe_w13, moe_w2):
    bsz, l, d = x.shape
    depth = w_mod.shape[0]
    assert bsz + 1 <= SUBLANES and l % GRID_W == 0
    cond = jnp.zeros((SUBLANES, d), F32).at[:bsz].set(c).at[bsz].set(c_ctx)
    mods = adaln_all(cond, w_mod, b_mod).reshape(depth, SUBLANES, 6, d)
    rope = _rope_tables(l // GRID_W)
    per_layer = (norm1_g, norm2_g, w_in, gla_w_a2, gla_b_a, gla_norm_g,
                 mla_q_norm_g, mla_kv_norm_g, mla_w_uq, mla_w_ukv, mla_q_head_g, mla_k_head_g,
                 lru_conv_w, lru_conv_b, lru_w_gate, lru_b_gate, lru_lambda, w_out,
                 moe_w_grp, moe_b_grp, moe_w_exp, moe_b_exp)
    xc = ctx
    h_pre = None
    for i in range(depth):
        last = i == depth - 1
        x, xc, h_pre = _layer(x, xc, mods[i], rope, tuple(t[i] for t in per_layer), moe_w13, moe_w2, i,
                              last=last, h_pre=h_pre, nxt=None if last else (norm1_g[i + 1], mods[i + 1]))
    return x
```

```python
import functools

import jax
import jax.numpy as jnp
from jax import lax
from jax.experimental import pallas as pl
from jax.experimental.pallas import tpu as pltpu

F32 = jnp.float32
BF16 = jnp.bfloat16
HI = lax.Precision.HIGHEST

D_MODEL = 4096
GRID_W = 64
NORM_EPS = 1e-6
D_GLA = D_MODEL // 4
D_MLA = D_MODEL // 2
D_LRU = D_MODEL // 4
GLA_HEADS = 4
GLA_DK = 128
GLA_DV = 256
GLA_GATE_RANK = 16
GLA_TAU = 16.0
GLA_CHUNK = 64
GLA_QSCALE = GLA_DK ** -0.5
MLA_HEADS = 16
MLA_NOPE = 128
MLA_ROPE = 64
MLA_V = 128
MLA_Q_RANK = 768
MLA_KV_RANK = 512
MLA_SCALE = (MLA_NOPE + MLA_ROPE) ** -0.5
LOG2E = 1.4426950408889634
ROPE_BASE = 10000.0
LRU_BLOCKS = 8
LRU_BW = 128
LRU_CONV = 4
LRU_C = 8.0
N_GROUPS = 8
EXPERTS_PER_GROUP = 8
N_EXPERTS = 64
TOP_K = 2
D_EXPERT = 256

LANES = 128
SUBLANES = 8
VMEM_BYTES = 64 << 20
VMEM_LIMIT = VMEM_BYTES - (8 << 20)
MOE_BLOCK = 256
MOE_SLOTS = 3

QDN_W, AL_W, KR_W = MLA_Q_RANK, LANES, LANES
COL_QDN = 0
COL_AL = 768
COL_KR = 896
COL_V = 1024
COL_G = 2048
COL_XB = 3072
COL_GB = 4096
COL_Q = 5120
COL_K = 5632
COL_KVDN = 6144
D_INP = 6656
HEAD_W = 256
VT_ROWS = MLA_V + 16
HEADS_PER_STEP = 4


def _params(sem, vmem=VMEM_LIMIT):
    return pltpu.CompilerParams(dimension_semantics=sem, vmem_limit_bytes=vmem)


def _tile(n, pref):
    t = min(n, pref)
    while n % t or t % SUBLANES:
        t -= 1
    return t


def _silu(x):
    return x * jax.nn.sigmoid(x)


def _softplus(x):
    return jnp.maximum(x, 0.0) + jnp.log1p(jnp.exp(-jnp.abs(x)))


def _pack_halves(x):
    w = x.shape[1] // 2
    bits = lax.bitcast_convert_type(x.astype(BF16).astype(F32), jnp.int32)
    return bits[:, w:] | lax.shift_right_logical(bits[:, :w], jnp.int32(16))


def _unpack_halves(p):
    lo = lax.bitcast_convert_type(lax.shift_left(p, jnp.int32(16)), F32)
    hi = lax.bitcast_convert_type(p & jnp.int32(-65536), F32)
    return lo, hi


def _adaln_body(c_ref, w_ref, b_ref, o_ref):
    s = _silu(c_ref[...]).astype(BF16)
    o_ref[...] = jnp.dot(s, w_ref[...].astype(BF16), preferred_element_type=F32) + b_ref[...]


def adaln_all(cond, w_mod, b_mod):
    depth, d, n = w_mod.shape
    tn = 1024
    return pl.pallas_call(
        _adaln_body,
        grid=(depth, n // tn),
        in_specs=[pl.BlockSpec((SUBLANES, d), lambda l, j: (0, 0)),
                  pl.BlockSpec((None, d, tn), lambda l, j: (l, 0, j)),
                  pl.BlockSpec((None, 1, tn), lambda l, j: (l, 0, j))],
        out_specs=pl.BlockSpec((None, SUBLANES, tn), lambda l, j: (l, 0, j)),
        out_shape=jax.ShapeDtypeStruct((depth, SUBLANES, n), F32),
        compiler_params=_params(("arbitrary", "arbitrary")),
        name="adaln",
    )(cond, w_mod, b_mod.reshape(depth, 1, n))


def _norm_mod(x, g, shift, scale):
    y = x * lax.rsqrt(jnp.mean(x * x, axis=-1, keepdims=True) + NORM_EPS) * g
    return y * (1.0 + scale) + shift


def _norm_mod_body(x_ref, g_ref, sh_ref, sc_ref, o_ref):
    o_ref[...] = _norm_mod(x_ref[...], g_ref[...], sh_ref[...], sc_ref[...]).astype(o_ref.dtype)


def _norm_router_body(*refs, n_ctx):
    if n_ctx:
        (xc_ref, shc_ref, scc_ref, x_ref, sh_ref, sc_ref, g_ref, wr_ref, br_ref, o_ref, lg_ref) = refs
        is_ctx = pl.program_id(0) < n_ctx
        x = jnp.where(is_ctx, xc_ref[...], x_ref[...])
        sh = jnp.where(is_ctx, shc_ref[...], sh_ref[...])
        sc = jnp.where(is_ctx, scc_ref[...], sc_ref[...])
    else:
        x_ref, sh_ref, sc_ref, g_ref, wr_ref, br_ref, o_ref, lg_ref = refs
        x, sh, sc = x_ref[...], sh_ref[...], sc_ref[...]
    h = _norm_mod(x, g_ref[...], sh, sc)
    o_ref[...] = _pack_halves(h)
    lg_ref[...] = jnp.dot(h.astype(BF16), wr_ref[...], preferred_element_type=F32) + br_ref[...]


def norm_mod(x, g, shift, scale):
    bx, lx, d = x.shape
    tm = _tile(lx, 256)
    row = pl.BlockSpec((None, tm, d), lambda b, i: (b, i, 0))
    vec = pl.BlockSpec((1, d), lambda b, i: (0, 0))
    mod = pl.BlockSpec((None, 1, d), lambda b, i: (b, 0, 0))
    return pl.pallas_call(
        _norm_mod_body, grid=(bx, lx // tm),
        in_specs=[row, vec, mod, mod], out_specs=row,
        out_shape=jax.ShapeDtypeStruct((bx, lx, d), BF16),
        compiler_params=_params(("arbitrary", "arbitrary")), name="norm_mod",
    )(x, g.reshape(1, d), shift, scale)


def norm_router(x, g, shift, scale, wr, br, ctx=None):
    bsz, l, d = x.shape
    nr = wr.shape[1]
    lc = ctx[0].shape[1] if ctx is not None else 0
    tm = _tile(l, 256) if ctx is None else _tile(lc, 256)
    assert l % tm == 0
    n_ctx = bsz * lc // tm
    n_lat = bsz * l // tm
    per_b = l // tm
    lat = lambda f: jnp.maximum(f - n_ctx, 0)
    vec = lambda w: pl.BlockSpec((1, w), lambda f: (0, 0))
    in_specs, args = [], []
    if ctx is not None:
        xc, shc, scc = ctx
        one = pl.BlockSpec((None, 1, d), lambda f: (0, 0, 0))
        in_specs += [pl.BlockSpec((tm, d), lambda f: (jnp.minimum(f, n_ctx - 1), 0)), one, one]
        args += [xc.reshape(bsz * lc, d), shc, scc]
    mod = pl.BlockSpec((None, 1, d), lambda f: (lat(f) // per_b, 0, 0))
    in_specs += [pl.BlockSpec((tm, d), lambda f: (lat(f), 0)), mod, mod, vec(d),
                 pl.BlockSpec((d, nr), lambda f: (0, 0)), vec(nr)]
    args += [x.reshape(bsz * l, d), shift, scale, g.reshape(1, d), wr.astype(BF16), br]
    total = bsz * (lc + l)
    return pl.pallas_call(
        functools.partial(_norm_router_body, n_ctx=n_ctx), grid=(n_ctx + n_lat,), in_specs=in_specs,
        out_specs=[pl.BlockSpec((tm, d // 2), lambda f: (f, 0)), pl.BlockSpec((tm, nr), lambda f: (f, 0))],
        out_shape=[jax.ShapeDtypeStruct((total, d // 2), jnp.int32), jax.ShapeDtypeStruct((total, nr), F32)],
        compiler_params=_params(("arbitrary",)), name="norm_router",
    )(*args)


def _mm_body(a_ref, w_ref, o_ref):
    o_ref[...] = jnp.dot(a_ref[...], w_ref[...], preferred_element_type=F32).astype(o_ref.dtype)


def matmul(a, w, out_dtype=F32, tm_pref=1024, tn=512):
    m, k = a.shape
    n = w.shape[1]
    tm = _tile(m, tm_pref)
    return pl.pallas_call(
        _mm_body, grid=(m // tm, n // tn),
        in_specs=[pl.BlockSpec((tm, k), lambda i, j: (i, 0)),
                  pl.BlockSpec((k, tn), lambda i, j: (0, j))],
        out_specs=pl.BlockSpec((tm, tn), lambda i, j: (i, j)),
        out_shape=jax.ShapeDtypeStruct((m, n), out_dtype),
        compiler_params=_params(("arbitrary", "arbitrary")), name="matmul",
    )(a, w)


def _wout_body(ya_ref, yb_ref, yl_ref, w_ref, x_ref, g_ref, o_ref):
    acc = jnp.dot(ya_ref[...], w_ref[0:D_GLA, :], preferred_element_type=F32)
    acc += jnp.dot(yb_ref[...], w_ref[D_GLA:D_GLA + D_MLA, :], preferred_element_type=F32)
    acc += jnp.dot(yl_ref[...], w_ref[D_GLA + D_MLA:, :], preferred_element_type=F32)
    o_ref[...] = x_ref[...] + g_ref[...] * acc


def out_proj(ya, yb, yl, w, x, gate):
    bx, lx, d = x.shape
    tm = _tile(lx, 1024)
    tn = 512
    return pl.pallas_call(
        _wout_body, grid=(bx, lx // tm, d // tn),
        in_specs=[pl.BlockSpec((None, tm, D_GLA), lambda b, i, j: (b, i, 0)),
                  pl.BlockSpec((None, tm, D_MLA), lambda b, i, j: (b, i, 0)),
                  pl.BlockSpec((None, tm, D_LRU), lambda b, i, j: (b, i, 0)),
                  pl.BlockSpec((d, tn), lambda b, i, j: (0, j)),
                  pl.BlockSpec((None, tm, tn), lambda b, i, j: (b, i, j)),
                  pl.BlockSpec((None, 1, tn), lambda b, i, j: (b, 0, j))],
        out_specs=pl.BlockSpec((None, tm, tn), lambda b, i, j: (b, i, j)),
        out_shape=jax.ShapeDtypeStruct((bx, lx, d), F32),
        compiler_params=_params(("arbitrary", "arbitrary", "arbitrary")), name="out_proj",
    )(ya, yb, yl, w, x, gate)


def _gla_body(*refs, reverse, final, nchunk):
    if final:
        (q_ref, k_ref, v_ref, al_ref, wa_ref, ba_ref, s0_ref, of_ref, g_ref, ng_ref,
         o_ref, sfin_ref, s_scr) = refs
    else:
        q_ref, k_ref, v_ref, al_ref, wa_ref, ba_ref, s0_ref, o_ref, sfin_ref, s_scr = refs
    C = GLA_CHUNK

    @pl.when(pl.program_id(1) == 0)
    def _():
        s_scr[...] = s0_ref[...]

    r_i = lax.broadcasted_iota(jnp.int32, (C, C), 0)
    c_i = lax.broadcasted_iota(jnp.int32, (C, C), 1)
    keep = (c_i >= r_i) if reverse else (c_i <= r_i)
    tri = jnp.where(keep, 1.0, 0.0).astype(BF16)
    ref_row = C - 1 - C // 2 if reverse else C // 2
    last_row = 0 if reverse else C - 1

    order = range(nchunk - 1, -1, -1) if reverse else range(nchunk)
    for ci in order:
        rows = slice(ci * C, (ci + 1) * C)
        z = jnp.dot(al_ref[rows, :].astype(BF16), wa_ref[...], preferred_element_type=F32) + ba_ref[...]
        logd = (jnp.minimum(z, 0.0) - jnp.log1p(jnp.exp(-jnp.abs(z)))) * (1.0 / GLA_TAU)
        l_hi = logd.astype(BF16)
        rem = logd - l_hi.astype(F32)
        l_mid = rem.astype(BF16)
        l_lo = (rem - l_mid.astype(F32)).astype(BF16)
        b = (jnp.dot(tri, l_hi, preferred_element_type=F32) + jnp.dot(tri, l_mid, preferred_element_type=F32)
             + jnp.dot(tri, l_lo, preferred_element_type=F32))
        b_ref = b[ref_row:ref_row + 1, :]
        b_last = b[last_row:last_row + 1, :]
        q = q_ref[rows, :] * GLA_QSCALE
        k = k_ref[rows, :]
        qe = (q * jnp.exp(b - b_ref)).astype(BF16)
        ke = (k * jnp.exp(b_ref - b)).astype(BF16)
        qin = (q * jnp.exp(b)).astype(BF16)
        kst = k * jnp.exp(b_last - b)
        for h in range(GLA_HEADS):
            hs = slice(h * GLA_DK, (h + 1) * GLA_DK)
            vs = slice(h * GLA_DV, (h + 1) * GLA_DV)
            v = v_ref[rows, vs].astype(BF16)
            att = lax.dot_general(qe[:, hs], ke[:, hs], (((1,), (1,)), ((), ())),
                                  preferred_element_type=F32)
            att = jnp.where(keep, att, 0.0).astype(BF16)
            s = s_scr[h]
            o = (jnp.dot(att, v, preferred_element_type=F32)
                 + jnp.dot(qin[:, hs], s.astype(BF16), preferred_element_type=F32))
            d_col = jnp.exp(jnp.broadcast_to(b_last[:, hs], (GLA_DK, GLA_DK))).T
            s_scr[h] = (jnp.concatenate([d_col, d_col], axis=1) * s
                        + jnp.dot(kst[:, hs].T.astype(BF16), v, preferred_element_type=F32))
            if final:
                o = o + of_ref[rows, vs]
                y = o * lax.rsqrt(jnp.mean(o * o, axis=-1, keepdims=True) + NORM_EPS) * ng_ref[...]
                o_ref[rows, vs] = (y * _silu(g_ref[rows, vs])).astype(o_ref.dtype)
            else:
                o_ref[rows, vs] = o
    sfin_ref[...] = s_scr[...]


def gla_pass(p, wa, ba, s0, reverse, o_fwd=None, norm_g=None):
    bsz, l, _ = p.shape
    t = _tile(l, 256)
    nt = l // t
    final = o_fwd is not None

    def rix(i):
        return nt - 1 - i if reverse else i

    def col(width, off):
        return pl.BlockSpec((None, t, width), lambda b, i: (b, rix(i), off // width))

    hq = GLA_HEADS * GLA_DK
    state = pl.BlockSpec((None, GLA_HEADS, GLA_DK, GLA_DV), lambda b, i: (b, 0, 0, 0))
    in_specs = [col(hq, COL_Q), col(hq, COL_K), col(D_GLA, COL_V), col(AL_W, COL_AL),
                pl.BlockSpec((AL_W, hq), lambda b, i: (0, 0)),
                pl.BlockSpec((1, hq), lambda b, i: (0, 0)), state]
    args = [p, p, p, p, wa, ba, s0]
    orow = pl.BlockSpec((None, t, D_GLA), lambda b, i: (b, rix(i), 0))
    if final:
        in_specs += [orow, col(D_GLA, COL_G), pl.BlockSpec((1, GLA_DV), lambda b, i: (0, 0))]
        args += [o_fwd, p, norm_g.reshape(1, GLA_DV)]
    return pl.pallas_call(
        functools.partial(_gla_body, reverse=reverse, final=final, nchunk=t // GLA_CHUNK),
        grid=(bsz, nt), in_specs=in_specs, out_specs=[orow, state],
        out_shape=[jax.ShapeDtypeStruct((bsz, l, D_GLA), BF16 if final else F32),
                   jax.ShapeDtypeStruct((bsz, GLA_HEADS, GLA_DK, GLA_DV), F32)],
        scratch_shapes=[pltpu.VMEM((GLA_HEADS, GLA_DK, GLA_DV), F32)],
        compiler_params=_params(("arbitrary", "arbitrary")), name="gla",
    )(*args)


def _lru_body(*refs, reverse, final, t, nt):
    if final:
        (x_ref, xp_ref, xn_ref, cw_ref, cb_ref, wg_ref, bg_ref, lam_ref, h0_ref, hf_ref, gb_ref,
         o_ref, hfin_ref, h_scr, hbuf) = refs
    else:
        (x_ref, xp_ref, xn_ref, cw_ref, cb_ref, wg_ref, bg_ref, lam_ref, h0_ref,
         o_ref, hfin_ref, h_scr) = refs
        hbuf = o_ref
    i = pl.program_id(1)
    ti = nt - 1 - i if reverse else i

    @pl.when(i == 0)
    def _():
        h_scr[...] = h0_ref[...]

    ngroup = t // SUBLANES
    grp = lambda v: v.reshape(v.shape[0] // SUBLANES, SUBLANES, v.shape[1])
    sub = lax.broadcasted_iota(jnp.int32, (1, SUBLANES, 1), 1)
    x3 = grp(x_ref[...])
    xp3 = grp(jnp.where(ti == 0, 0.0, xp_ref[...]))
    xn3 = grp(jnp.where(ti == nt - 1, 0.0, xn_ref[...]))

    def row_shift(k):
        rot = pltpu.roll(x3, k % SUBLANES, axis=1)
        if k > 0:
            nb_rot = jnp.concatenate([pltpu.roll(xp3, k % SUBLANES, axis=1), rot[:-1]], axis=0)
            return jnp.where(sub >= k, rot, nb_rot)
        nb_rot = jnp.concatenate([rot[1:], pltpu.roll(xn3, k % SUBLANES, axis=1)], axis=0)
        return jnp.where(sub < SUBLANES + k, rot, nb_rot)

    cw = cw_ref[...]
    xc = (cb_ref[...] + row_shift(2) * cw[0:1, :] + row_shift(1) * cw[1:2, :]
          + x3 * cw[2:3, :] + row_shift(-1) * cw[3:4, :]).reshape(t, D_LRU)

    zs = []
    for n in range(LRU_BLOCKS):
        zs.append(jnp.dot(xc[:, n * LRU_BW:(n + 1) * LRU_BW].astype(BF16), wg_ref[n],
                          preferred_element_type=F32))
    z_r = jnp.concatenate([z[:, :LRU_BW] for z in zs], axis=1) + bg_ref[0:1, :]
    z_i = jnp.concatenate([z[:, LRU_BW:] for z in zs], axis=1) + bg_ref[1:2, :]
    log_a = -LRU_C * jax.nn.sigmoid(z_r) * _softplus(-lam_ref[...])
    a = jnp.exp(log_a)
    th = jnp.tanh(log_a)
    one_minus_a2 = -2.0 * th / (1.0 - th)
    u = jnp.sqrt(one_minus_a2) * jax.nn.sigmoid(z_i) * xc

    a, u = grp(a), grp(u)
    s = 1
    while s < SUBLANES:
        if reverse:
            ok = sub < SUBLANES - s
            a_s = pltpu.roll(a, SUBLANES - s, axis=1)
            u_s = pltpu.roll(u, SUBLANES - s, axis=1)
        else:
            ok = sub >= s
            a_s = pltpu.roll(a, s, axis=1)
            u_s = pltpu.roll(u, s, axis=1)
        u = jnp.where(ok, a * u_s + u, u)
        a = jnp.where(ok, a * a_s, a)
        s *= 2
    edge = 0 if reverse else SUBLANES - 1
    h_prev = h_scr[...]
    for gi in (range(ngroup - 1, -1, -1) if reverse else range(ngroup)):
        rows = slice(gi * SUBLANES, (gi + 1) * SUBLANES)
        hg = a[gi] * h_prev + u[gi]
        h_prev = hg[edge:edge + 1, :]
        hbuf[rows, :] = hg
    h_scr[...] = h_prev
    hfin_ref[...] = h_prev
    if final:
        g = gb_ref[...]
        gelu = 0.5 * g * (1.0 + jnp.tanh(0.7978845608028654 * (g + 0.044715 * g * g * g)))
        o_ref[...] = (gelu * (hf_ref[...] + hbuf[...])).astype(o_ref.dtype)


def lru_pass(p, cw, cb, wg, bg, lam, h0, reverse, h_fwd=None):
    bsz, l, _ = p.shape
    t = _tile(l, 256)
    nt = l // t
    tb = t // SUBLANES
    nb8 = l // SUBLANES
    final = h_fwd is not None
    xoff = COL_XB // D_LRU

    def rix(i):
        return nt - 1 - i if reverse else i

    vec = lambda r: pl.BlockSpec((r, D_LRU), lambda b, i: (0, 0))
    one = pl.BlockSpec((None, 1, D_LRU), lambda b, i: (b, 0, 0))
    orow = pl.BlockSpec((None, t, D_LRU), lambda b, i: (b, rix(i), 0))
    in_specs = [pl.BlockSpec((None, t, D_LRU), lambda b, i: (b, rix(i), xoff)),
                pl.BlockSpec((None, SUBLANES, D_LRU),
                             lambda b, i: (b, jnp.maximum(rix(i) * tb - 1, 0), xoff)),
                pl.BlockSpec((None, SUBLANES, D_LRU),
                             lambda b, i: (b, jnp.minimum((rix(i) + 1) * tb, nb8 - 1), xoff)),
                vec(LRU_CONV), vec(1),
                pl.BlockSpec((LRU_BLOCKS, LRU_BW, 2 * LRU_BW), lambda b, i: (0, 0, 0)),
                vec(2), vec(1), one]
    args = [p, p, p, cw, cb.reshape(1, D_LRU), wg, bg, lam.reshape(1, D_LRU), h0]
    if final:
        in_specs += [orow, pl.BlockSpec((None, t, D_LRU), lambda b, i: (b, rix(i), COL_GB // D_LRU))]
        args += [h_fwd, p]
    return pl.pallas_call(
        functools.partial(_lru_body, reverse=reverse, final=final, t=t, nt=nt),
        grid=(bsz, nt), in_specs=in_specs, out_specs=[orow, one],
        out_shape=[jax.ShapeDtypeStruct((bsz, l, D_LRU), BF16 if final else F32),
                   jax.ShapeDtypeStruct((bsz, 1, D_LRU), F32)],
        scratch_shapes=[pltpu.VMEM((1, D_LRU), F32)] + ([pltpu.VMEM((t, D_LRU), F32)] if final else []),
        compiler_params=_params(("arbitrary", "arbitrary")), name="lru",
    )(*args)


def _rms(x, width):
    return x * lax.rsqrt(jnp.sum(x * x, axis=-1, keepdims=True) * (1.0 / width) + NORM_EPS)


def _rope(r, cos, sin):
    lane = lax.broadcasted_iota(jnp.int32, r.shape, 1)
    first = (lane % 32) < 16
    rot = jnp.where(first, pltpu.roll(r, LANES - 16, axis=1), pltpu.roll(r, 16, axis=1))
    return r * cos + rot * sin


def _qproj_body(x_ref, gn_ref, w_ref, hg_ref, cos_ref, sin_ref, o_ref, xn_scr):
    @pl.when(pl.program_id(2) == 0)
    def _():
        xn_scr[...] = (_rms(x_ref[...], MLA_Q_RANK) * gn_ref[...]).astype(BF16)

    a = jnp.dot(xn_scr[...], w_ref[...], preferred_element_type=F32)
    for hh in range(HEADS_PER_STEP):
        c0 = hh * HEAD_W
        qn = _rms(a[:, c0:c0 + MLA_NOPE], MLA_NOPE) * hg_ref[:, :MLA_NOPE]
        qr = _rms(a[:, c0 + MLA_NOPE:c0 + HEAD_W], MLA_ROPE) * hg_ref[:, MLA_NOPE:]
        qr = _rope(qr, cos_ref[...], sin_ref[...])
        o_ref[hh] = (jnp.concatenate([qn, qr], axis=1) * (MLA_SCALE * LOG2E)).astype(o_ref.dtype)


def q_proj(p, gn, w, hg, cos, sin):
    bsz, l, _ = p.shape
    tm = _tile(l, 512)
    hw = HEADS_PER_STEP * HEAD_W
    return pl.pallas_call(
        _qproj_body, grid=(bsz, l // tm, MLA_HEADS // HEADS_PER_STEP),
        in_specs=[pl.BlockSpec((None, tm, QDN_W), lambda b, i, h: (b, i, COL_QDN // QDN_W)),
                  pl.BlockSpec((1, QDN_W), lambda b, i, h: (0, 0)),
                  pl.BlockSpec((QDN_W, hw), lambda b, i, h: (0, h)),
                  pl.BlockSpec((1, HEAD_W), lambda b, i, h: (0, 0)),
                  pl.BlockSpec((tm, LANES), lambda b, i, h: (i, 0)),
                  pl.BlockSpec((tm, LANES), lambda b, i, h: (i, 0))],
        out_specs=pl.BlockSpec((None, HEADS_PER_STEP, tm, HEAD_W), lambda b, i, h: (b, h, i, 0)),
        out_shape=jax.ShapeDtypeStruct((bsz, MLA_HEADS, l, HEAD_W), BF16),
        scratch_shapes=[pltpu.VMEM((tm, QDN_W), BF16)],
        compiler_params=_params(("arbitrary", "arbitrary", "arbitrary")), name="q_proj",
    )(p, gn.reshape(1, QDN_W), w, hg, cos, sin)


def _kvproj_body(x_ref, kr_ref, gn_ref, w_ref, hg_ref, cos_ref, sin_ref, k_ref, v_ref, xn_scr, kr_scr):
    @pl.when(pl.program_id(2) == 0)
    def _():
        xn_scr[...] = (_rms(x_ref[...], MLA_KV_RANK) * gn_ref[...]).astype(BF16)
        kr = _rms(kr_ref[...], MLA_ROPE) * hg_ref[:, MLA_NOPE:]
        kr_scr[...] = _rope(kr, cos_ref[...], sin_ref[...]).astype(BF16)

    a = jnp.dot(xn_scr[...], w_ref[...], preferred_element_type=F32)
    for hh in range(HEADS_PER_STEP):
        c0 = hh * HEAD_W
        kn = _rms(a[:, c0:c0 + MLA_NOPE], MLA_NOPE) * hg_ref[:, :MLA_NOPE]
        k_ref[hh, :, :MLA_NOPE] = kn.astype(k_ref.dtype)
        k_ref[hh, :, MLA_NOPE:] = kr_scr[...]
        v_ref[hh, :MLA_V, :] = a[:, c0 + MLA_NOPE:c0 + HEAD_W].T.astype(v_ref.dtype)
        v_ref[hh, MLA_V:, :] = jnp.ones((VT_ROWS - MLA_V, a.shape[0]), v_ref.dtype)


def kv_proj(p, gn, w, hg, cos, sin):
    bsz, l, _ = p.shape
    tm = _tile(l, 512)
    hw = HEADS_PER_STEP * HEAD_W
    return pl.pallas_call(
        _kvproj_body, grid=(bsz, l // tm, MLA_HEADS // HEADS_PER_STEP),
        in_specs=[pl.BlockSpec((None, tm, MLA_KV_RANK), lambda b, i, h: (b, i, COL_KVDN // MLA_KV_RANK)),
                  pl.BlockSpec((None, tm, KR_W), lambda b, i, h: (b, i, COL_KR // KR_W)),
                  pl.BlockSpec((1, MLA_KV_RANK), lambda b, i, h: (0, 0)),
                  pl.BlockSpec((MLA_KV_RANK, hw), lambda b, i, h: (0, h)),
                  pl.BlockSpec((1, HEAD_W), lambda b, i, h: (0, 0)),
                  pl.BlockSpec((tm, LANES), lambda b, i, h: (i, 0)),
                  pl.BlockSpec((tm, LANES), lambda b, i, h: (i, 0))],
        out_specs=[pl.BlockSpec((None, HEADS_PER_STEP, tm, HEAD_W), lambda b, i, h: (b, h, i, 0)),
                   pl.BlockSpec((None, HEADS_PER_STEP, VT_ROWS, tm), lambda b, i, h: (b, h, 0, i))],
        out_shape=[jax.ShapeDtypeStruct((bsz, MLA_HEADS, l, HEAD_W), BF16),
                   jax.ShapeDtypeStruct((bsz, MLA_HEADS, VT_ROWS, l), BF16)],
        scratch_shapes=[pltpu.VMEM((tm, MLA_KV_RANK), BF16), pltpu.VMEM((tm, KR_W), BF16)],
        compiler_params=_params(("arbitrary", "arbitrary", "arbitrary")), name="kv_proj",
    )(p, p, gn.reshape(1, MLA_KV_RANK), w, hg, cos, sin)


def _scores_t(k, q):
    return lax.dot_general(k, q, (((1,), (1,)), ((), ())), preferred_element_type=F32)


def _attn_body(*refs, tk, n_lat):
    if n_lat:
        q_ref, kc_ref, vc_ref, k_ref, v_ref, o_ref, m_scr, acc_scr, s_scr = refs
    else:
        q_ref, kc_ref, vc_ref, o_ref = refs

    def qk(slot, off):
        s_scr[slot] = _scores_t(k_ref[pl.ds(off, tk), :], q_ref[...])

    if n_lat:
        qk(0, 0)

    sc = _scores_t(kc_ref[...], q_ref[...])
    m0 = jnp.max(sc, axis=0, keepdims=True)
    acc = jnp.dot(vc_ref[...], jnp.exp2(sc - m0).astype(BF16), preferred_element_type=F32)

    if n_lat:
        m_scr[...] = m0
        acc_scr[...] = acc

        def consume(slot, off):
            s = s_scr[slot]
            m = m_scr[...]
            m_new = jnp.maximum(m, jnp.max(s, axis=0, keepdims=True))
            p = jnp.exp2(s - m_new).astype(BF16)
            m_scr[...] = m_new
            acc_scr[...] = (jnp.exp2(m - m_new) * acc_scr[...]
                            + jnp.dot(v_ref[:, pl.ds(off, tk)], p, preferred_element_type=F32))

        def pair(jj, carry):
            off = pl.multiple_of(jj * (2 * tk), 2 * tk)
            qk(1, off + tk)
            consume(0, off)
            qk(0, off + 2 * tk)
            consume(1, off + tk)
            return carry

        lax.fori_loop(0, n_lat // 2 - 1, pair, 0)
        off = (n_lat - 2) * tk
        qk(1, off + tk)
        consume(0, off)
        consume(1, off + tk)
        acc = acc_scr[...]
    o_ref[...] = (acc[:MLA_V] / acc[MLA_V:MLA_V + 1]).T.astype(o_ref.dtype)


def attention(q, kc, vc, k=None, v=None):
    bsz, nh, lq, _ = q.shape
    lc = kc.shape[2]
    tq = _tile(lq, 1024)
    keys = lambda n: pl.BlockSpec((None, None, n, HEAD_W), lambda b, h, i: (b, h, 0, 0))
    vals = lambda n: pl.BlockSpec((None, None, VT_ROWS, n), lambda b, h, i: (b, h, 0, 0))
    in_specs = [pl.BlockSpec((None, None, tq, HEAD_W), lambda b, h, i: (b, h, i, 0)), keys(lc), vals(lc)]
    args = [q, kc, vc]
    tk = n_lat = 0
    scratch = []
    if k is not None:
        lk = k.shape[2]
        tk = _tile(lk // 2, 1024)
        assert tk % LANES == 0 and lk % (2 * tk) == 0
        n_lat = lk // tk
        in_specs += [keys(lk), vals(lk)]
        args += [k, v]
        scratch = [pltpu.VMEM((1, tq), F32), pltpu.VMEM((VT_ROWS, tq), F32), pltpu.VMEM((2, tk, tq), F32)]
    return pl.pallas_call(
        functools.partial(_attn_body, tk=tk, n_lat=n_lat),
        grid=(bsz, nh, lq // tq), in_specs=in_specs,
        out_specs=pl.BlockSpec((None, tq, MLA_V), lambda b, h, i: (b, i, h)),
        out_shape=jax.ShapeDtypeStruct((bsz, lq, nh * MLA_V), BF16),
        scratch_shapes=scratch,
        compiler_params=_params(("arbitrary", "arbitrary", "arbitrary")), name="attention",
    )(*args)


def _row_copy(src_hbm, row, buf, slot, r, sem):
    return pltpu.make_async_copy(src_hbm.at[pl.ds(row, 1), :], buf.at[slot, pl.ds(r, 1), :], sem.at[slot])


def _moe_body(be_ref, nu_ref, first_ref, tok_ref, h_hbm, w13_ref, w2_ref, o_ref, xbuf, sem, *, n_assign):
    j = pl.program_id(0)
    nu = nu_ref[0]
    slot = j % MOE_SLOTS
    ahead = MOE_SLOTS - 1

    def rows(blk, slot, start):
        base = first_ref[blk]

        def body(r, carry):
            tok = tok_ref[jnp.minimum(base + r, n_assign - 1)]
            cp = _row_copy(h_hbm, tok, xbuf, slot, r, sem)
            if start:
                cp.start()
            else:
                cp.wait()
            return carry

        lax.fori_loop(0, MOE_BLOCK, body, 0, unroll=8)

    for first in range(ahead):
        @pl.when((j == 0) & (first < nu))
        def _():
            rows(first, first, True)

    @pl.when(j + ahead < nu)
    def _():
        rows(j + ahead, (j + ahead) % MOE_SLOTS, True)

    @pl.when(j < nu)
    def _():
        rows(j, slot, False)
        x_lo, x_hi = _unpack_halves(xbuf[slot])
        half = w13_ref.shape[0] // 2
        hid = (jnp.dot(x_lo.astype(BF16), w13_ref[:half, :].astype(BF16), preferred_element_type=F32)
               + jnp.dot(x_hi.astype(BF16), w13_ref[half:, :].astype(BF16), preferred_element_type=F32))
        act = (_silu(hid[:, :D_EXPERT]) * hid[:, D_EXPERT:]).astype(BF16)
        o_ref[...] = _pack_halves(jnp.dot(act, w2_ref[...].astype(BF16), preferred_element_type=F32))

    @pl.when(j >= nu)
    def _():
        o_ref[...] = jnp.zeros_like(o_ref)


def moe_experts(h, tok_sorted, blk_first, blk_exp, n_used, w13, w2, layer):
    m, dp = h.shape
    d = 2 * dp
    nb = blk_first.shape[0]
    grid_spec = pltpu.PrefetchScalarGridSpec(
        num_scalar_prefetch=4, grid=(nb,),
        in_specs=[pl.BlockSpec(memory_space=pl.ANY),
                  pl.BlockSpec((None, None, d, 2 * D_EXPERT), lambda j, be, nu, fi, tok: (layer, be[j], 0, 0)),
                  pl.BlockSpec((None, None, D_EXPERT, d), lambda j, be, nu, fi, tok: (layer, be[j], 0, 0))],
        out_specs=pl.BlockSpec((MOE_BLOCK, dp), lambda j, be, nu, fi, tok: (j, 0)),
        scratch_shapes=[pltpu.VMEM((MOE_SLOTS, MOE_BLOCK, dp), jnp.int32),
                        pltpu.SemaphoreType.DMA((MOE_SLOTS,))])
    return pl.pallas_call(
        functools.partial(_moe_body, n_assign=tok_sorted.shape[0]), grid_spec=grid_spec,
        out_shape=jax.ShapeDtypeStruct((nb * MOE_BLOCK, dp), jnp.int32),
        compiler_params=_params(("arbitrary",)), name="moe_experts",
    )(blk_exp, n_used, blk_first, tok_sorted, h, w13, w2)


def _combine_body(dest_ref, out_hbm, x_ref, g_ref, tg_ref, *rest, tok_off, t, n_tiles, with_norm):
    if with_norm:
        ng_ref, sh_ref, sc_ref, o_ref, h_ref, buf_a, buf_b, sem_a, sem_b = rest
    else:
        o_ref, buf_a, buf_b, sem_a, sem_b = rest
    f = pl.program_id(0) * pl.num_programs(1) + pl.program_id(1)
    slot = f % 2

    def rows(tile, slot, start):
        base = (tok_off + tile * t) * TOP_K

        def body(r, carry):
            ca = _row_copy(out_hbm, dest_ref[base + TOP_K * r], buf_a, slot, r, sem_a)
            cb = _row_copy(out_hbm, dest_ref[base + TOP_K * r + 1], buf_b, slot, r, sem_b)
            if start:
                ca.start()
                cb.start()
            else:
                ca.wait()
                cb.wait()
            return carry

        lax.fori_loop(0, t, body, 0, unroll=8)

    @pl.when(f == 0)
    def _():
        rows(0, 0, True)

    @pl.when(f + 1 < n_tiles)
    def _():
        rows(f + 1, 1 - slot, True)

    rows(f, slot, False)
    d = x_ref.shape[1]
    half = d // 2
    tg0, tg1 = tg_ref[:, :LANES], tg_ref[:, LANES:]
    ss = jnp.zeros((t, LANES), F32)
    for c in range(half // LANES):
        wcols = slice(c * LANES, (c + 1) * LANES)
        a_lo, a_hi = _unpack_halves(buf_a[slot, :, wcols])
        b_lo, b_hi = _unpack_halves(buf_b[slot, :, wcols])
        for cols, ya, yb in ((wcols, a_lo, b_lo),
                             (slice(half + c * LANES, half + (c + 1) * LANES), a_hi, b_hi)):
            y = x_ref[:, cols] + g_ref[:, cols] * (tg0 * ya + tg1 * yb)
            o_ref[:, cols] = y
            ss = ss + y * y
    if with_norm:
        rs = lax.rsqrt(jnp.sum(ss, axis=-1, keepdims=True) * (1.0 / d) + NORM_EPS)
        rs = jnp.broadcast_to(rs, (t, LANES))
        for c in range(d // LANES):
            cols = slice(c * LANES, (c + 1) * LANES)
            h_ref[:, cols] = (o_ref[:, cols] * rs * ng_ref[:, cols] * (1.0 + sc_ref[:, cols])
                              + sh_ref[:, cols]).astype(h_ref.dtype)


def moe_combine(x, gate, out, dest, tok_gate, tok_off, norm=None):
    bx, lx, d = x.shape
    t = _tile(lx, 128)
    nt = lx // t
    tg = jnp.repeat(tok_gate[tok_off:tok_off + bx * lx], LANES, axis=1).reshape(bx, lx, TOP_K * LANES)
    row = pl.BlockSpec((None, t, d), lambda b, i, dst: (b, i, 0))
    mod = pl.BlockSpec((None, 1, d), lambda b, i, dst: (b, 0, 0))
    in_specs = [pl.BlockSpec(memory_space=pl.ANY), row, mod,
                pl.BlockSpec((None, t, TOP_K * LANES), lambda b, i, dst: (b, i, 0))]
    args = [dest, out, x, gate, tg]
    out_specs, out_shape = row, jax.ShapeDtypeStruct((bx, lx, d), F32)
    if norm is not None:
        in_specs += [pl.BlockSpec((1, d), lambda b, i, dst: (0, 0)), mod, mod]
        args += [norm[0].reshape(1, d), norm[1], norm[2]]
        out_specs, out_shape = [row, row], [out_shape, jax.ShapeDtypeStruct((bx, lx, d), BF16)]
    grid_spec = pltpu.PrefetchScalarGridSpec(
        num_scalar_prefetch=1, grid=(bx, nt), in_specs=in_specs, out_specs=out_specs,
        scratch_shapes=[pltpu.VMEM((2, t, d // 2), jnp.int32), pltpu.VMEM((2, t, d // 2), jnp.int32),
                        pltpu.SemaphoreType.DMA((2,)), pltpu.SemaphoreType.DMA((2,))])
    return pl.pallas_call(
        functools.partial(_combine_body, tok_off=tok_off, t=t, n_tiles=bx * nt, with_norm=norm is not None),
        grid_spec=grid_spec, out_shape=out_shape,
        compiler_params=_params(("arbitrary", "arbitrary")), name="moe_combine",
    )(*args)


def moe_route(h, logits, w13, w2, layer):
    m = h.shape[0]
    g_logit = logits[:, :N_GROUPS]
    e_logit = logits[:, N_GROUPS:N_GROUPS + N_EXPERTS].reshape(m, N_GROUPS, EXPERTS_PER_GROUP)
    g_idx = jnp.argmax(g_logit, axis=-1)
    p_grp = jnp.take_along_axis(jax.nn.softmax(g_logit, axis=-1), g_idx[:, None], axis=1)
    e_sel = jnp.take_along_axis(e_logit, g_idx[:, None, None], axis=1)[:, 0]
    top_v, top_i = lax.top_k(e_sel, TOP_K)
    gate = (p_grp * jax.nn.softmax(top_v, axis=-1)).reshape(-1)
    e_flat = (g_idx[:, None] * EXPERTS_PER_GROUP + top_i).reshape(-1).astype(jnp.int32)

    a = m * TOP_K
    nb = -(-a // MOE_BLOCK) + N_EXPERTS
    order = jnp.argsort(e_flat).astype(jnp.int32)
    rank = jnp.argsort(order).astype(jnp.int32)
    e_s = e_flat[order]
    experts = jnp.arange(N_EXPERTS, dtype=jnp.int32)
    start = jnp.searchsorted(e_s, experts, side='left').astype(jnp.int32)
    counts = jnp.searchsorted(e_s, experts, side='right').astype(jnp.int32) - start
    padded = -(-counts // MOE_BLOCK) * MOE_BLOCK
    pad_end = jnp.cumsum(padded)
    pad_start = pad_end - padded
    blk_ids = jnp.arange(nb, dtype=jnp.int32)
    blk_raw = jnp.searchsorted(pad_end, blk_ids * MOE_BLOCK, side='right')
    blk_exp = jnp.minimum(blk_raw, N_EXPERTS - 1).astype(jnp.int32)
    n_used = (pad_end[-1] // MOE_BLOCK).astype(jnp.int32).reshape(1)
    blk_first = jnp.where(blk_raw < N_EXPERTS,
                          start[blk_exp] + blk_ids * MOE_BLOCK - pad_start[blk_exp], 0).astype(jnp.int32)
    tok_sorted = order // TOP_K
    dest = (pad_start[e_flat] + rank - start[e_flat]).astype(jnp.int32)

    out = moe_experts(h, tok_sorted, blk_first, blk_exp, n_used, w13, w2, layer)
    return out, dest, gate.reshape(m, TOP_K)


def _prep_w_in(w):
    sizes = (512, 512, 1024, 1024, 32, 768, 512, 64, 1024, 1024)
    offs = [0]
    for s in sizes:
        offs.append(offs[-1] + s)
    q, k, v, g, al, qdn, kvdn, kr, xb, gb = [w[:, offs[i]:offs[i + 1]] for i in range(10)]
    zpad = lambda t, n: jnp.pad(t, ((0, 0), (0, n - t.shape[1])))
    return jnp.concatenate([qdn, zpad(al, AL_W), zpad(kr, KR_W), v, g, xb, gb, q, k, kvdn],
                           axis=1).astype(BF16)


def _prep_w_uq(w):
    w = w.reshape(MLA_Q_RANK, MLA_HEADS, MLA_NOPE + MLA_ROPE)
    w = jnp.pad(w, ((0, 0), (0, 0), (0, HEAD_W - MLA_NOPE - MLA_ROPE)))
    return w.reshape(MLA_Q_RANK, MLA_HEADS * HEAD_W).astype(BF16)


def _head_gain(g):
    return jnp.pad(g, (0, HEAD_W - g.shape[0])).reshape(1, HEAD_W)


def _rope_tables(rows):
    n = MLA_ROPE // 4
    inv = ROPE_BASE ** (-jnp.arange(n, dtype=F32) / n)
    r = jnp.repeat(jnp.arange(rows, dtype=F32), GRID_W)
    c = jnp.tile(jnp.arange(GRID_W, dtype=F32), rows)
    ar, ac = r[:, None] * inv, c[:, None] * inv
    l = ar.shape[0]
    cos = jnp.concatenate([jnp.cos(ar), jnp.cos(ar), jnp.cos(ac), jnp.cos(ac),
                           jnp.ones((l, LANES - MLA_ROPE), F32)], axis=1)
    sin = jnp.concatenate([-jnp.sin(ar), jnp.sin(ar), -jnp.sin(ac), jnp.sin(ac),
                           jnp.zeros((l, LANES - MLA_ROPE), F32)], axis=1)
    return cos, sin


def _gla_gate_w(w_a2, b_a, d):
    hq = GLA_HEADS * GLA_DK
    w = jnp.zeros((AL_W, hq), F32).at[d * GLA_GATE_RANK:(d + 1) * GLA_GATE_RANK].set(w_a2[d])
    return w.astype(BF16), b_a[d].reshape(1, hq)


def _lru_gate_w(w_gate, d):
    return jnp.concatenate([w_gate[d, 0], w_gate[d, 1]], axis=-1).astype(BF16)


def _layer(x, xc, mods, rope, prm, moe_w13, moe_w2, layer, last, h_pre=None, nxt=None):
    (norm1_g, norm2_g, w_in, gla_w_a2, gla_b_a, gla_norm_g,
     mla_q_norm_g, mla_kv_norm_g, mla_w_uq, mla_w_ukv, mla_q_head_g, mla_k_head_g,
     lru_conv_w, lru_conv_b, lru_w_gate, lru_b_gate, lru_lambda, w_out,
     moe_w_grp, moe_b_grp, moe_w_exp, moe_b_exp) = prm
    bsz, l, d = x.shape
    lc = xc.shape[1]
    ctx_out = not last
    lat = lambda k: mods[:bsz, k].reshape(bsz, 1, d)
    ctx = lambda k: jnp.broadcast_to(mods[bsz, k].reshape(1, 1, d), (bsz, 1, d))

    w_in_p = _prep_w_in(w_in)
    if h_pre is None:
        h = norm_mod(x, norm1_g, lat(0), lat(1))
        hc = norm_mod(xc, norm1_g, ctx(0), ctx(1))
    else:
        h, hc = h_pre
    p = matmul(h.reshape(bsz * l, d), w_in_p).reshape(bsz, l, D_INP)
    pc = matmul(hc.reshape(bsz * lc, d), w_in_p).reshape(bsz, lc, D_INP)

    waf, baf = _gla_gate_w(gla_w_a2, gla_b_a, 0)
    wab, bab = _gla_gate_w(gla_w_a2, gla_b_a, 1)
    s_zero = jnp.zeros((bsz, GLA_HEADS, GLA_DK, GLA_DV), F32)
    ocf, s_f = gla_pass(pc, waf, baf, s_zero, False)
    of, _ = gla_pass(p, waf, baf, s_f, False)
    if ctx_out:
        yac, s_b = gla_pass(pc, wab, bab, s_zero, True, ocf, gla_norm_g)
    else:
        _, s_b = gla_pass(pc, wab, bab, s_zero, True)
    ya, _ = gla_pass(p, wab, bab, s_b, True, of, gla_norm_g)

    wgf, wgb = _lru_gate_w(lru_w_gate, 0), _lru_gate_w(lru_w_gate, 1)
    h_zero = jnp.zeros((bsz, 1, D_LRU), F32)
    lru = functools.partial(lru_pass, cw=lru_conv_w, cb=lru_conv_b)
    hcf, h0f = lru(pc, wg=wgf, bg=lru_b_gate[0], lam=lru_lambda[0], h0=h_zero, reverse=False)
    hf, _ = lru(p, wg=wgf, bg=lru_b_gate[0], lam=lru_lambda[0], h0=h0f, reverse=False)
    if ctx_out:
        ylc, h0b = lru(pc, wg=wgb, bg=lru_b_gate[1], lam=lru_lambda[1], h0=h_zero, reverse=True, h_fwd=hcf)
    else:
        _, h0b = lru(pc, wg=wgb, bg=lru_b_gate[1], lam=lru_lambda[1], h0=h_zero, reverse=True)
    yl, _ = lru(p, wg=wgb, bg=lru_b_gate[1], lam=lru_lambda[1], h0=h0b, reverse=True, h_fwd=hf)

    cos, sin = rope
    cos_c = jnp.ones((lc, LANES), F32)
    sin_c = jnp.zeros((lc, LANES), F32)
    w_uq = _prep_w_uq(mla_w_uq)
    w_ukv = mla_w_ukv.astype(BF16)
    qg, kg = _head_gain(mla_q_head_g), _head_gain(mla_k_head_g)
    kc, vc = kv_proj(pc, mla_kv_norm_g, w_ukv, kg, cos_c, sin_c)
    kl, vl = kv_proj(p, mla_kv_norm_g, w_ukv, kg, cos, sin)
    ql = q_proj(p, mla_q_norm_g, w_uq, qg, cos, sin)
    yb = attention(ql, kc, vc, kl, vl)

    w_out_b = w_out.astype(BF16)
    x = out_proj(ya, yb, yl, w_out_b, x, lat(2))
    if ctx_out:
        qc = q_proj(pc, mla_q_norm_g, w_uq, qg, cos_c, sin_c)
        ybc = attention(qc, kc, vc)
        xc = out_proj(yac, ybc, ylc, w_out_b, xc, ctx(2))

    nr = LANES
    wr = jnp.pad(jnp.concatenate([moe_w_grp, moe_w_exp], axis=1), ((0, 0), (0, nr - N_GROUPS - N_EXPERTS)))
    br = jnp.pad(jnp.concatenate([moe_b_grp, moe_b_exp]), (0, nr - N_GROUPS - N_EXPERTS)).reshape(1, nr)
    ctx1 = lambda k: mods[bsz, k].reshape(1, 1, d)
    h2, lg = norm_router(x, norm2_g, lat(3), lat(4), wr, br,
                         ctx=(xc, ctx1(3), ctx1(4)) if ctx_out else None)
    out, dest, tok_gate = moe_route(h2, lg, moe_w13, moe_w2, layer)
    h_next = None
    if ctx_out:
        g_next, m_next = nxt
        nlat = lambda k: m_next[:bsz, k].reshape(bsz, 1, d)
        nctx = lambda k: jnp.broadcast_to(m_next[bsz, k].reshape(1, 1, d), (bsz, 1, d))
        xc, hc_next = moe_combine(xc, ctx(5), out, dest, tok_gate, 0, norm=(g_next, nctx(0), nctx(1)))
        x, hl_next = moe_combine(x, lat(5), out, dest, tok_gate, bsz * lc, norm=(g_next, nlat(0), nlat(1)))
        h_next = (hl_next, hc_next)
    else:
        x = moe_combine(x, lat(5), out, dest, tok_gate, 0)
    return x, xc, h_next


def kernel(x, c, ctx, c_ctx, w_mod, b_mod, norm1_g, norm2_g, w_in, gla_w_a2, gla_b_a, gla_norm_g,
           mla_q_norm_g, mla_kv_norm_g, mla_w_uq, mla_w_ukv, mla_q_head_g, mla_k_head_g,
           lru_conv_w, lru_conv_b, lru_w_gate, lru_b_gate, lru_lambda, w_out,
           moe_w_grp, moe_b_grp, moe_w_exp, moe_b_exp, moe_w13, moe_w2):
    bsz, l, d = x.shape
    depth = w_mod.shape[0]
    assert bsz + 1 <= SUBLANES and l % GRID_W == 0
    cond = jnp.zeros((SUBLANES, d), F32).at[:bsz].set(c).at[bsz].set(c_ctx)
    mods = adaln_all(cond, w_mod, b_mod).reshape(depth, SUBLANES, 6, d)
    rope = _rope_tables(l // GRID_W)
    per_layer = (norm1_g, norm2_g, w_in, gla_w_a2, gla_b_a, gla_norm_g,
                 mla_q_norm_g, mla_kv_norm_g, mla_w_uq, mla_w_ukv, mla_q_head_g, mla_k_head_g,
                 lru_conv_w, lru_conv_b, lru_w_gate, lru_b_gate, lru_lambda, w_out,
                 moe_w_grp, moe_b_grp, moe_w_exp, moe_b_exp)
    xc = ctx
    h_pre = None
    for i in range(depth):
        last = i == depth - 1
        x, xc, h_pre = _layer(x, xc, mods[i], rope, tuple(t[i] for t in per_layer), moe_w13, moe_w2, i,
                              last=last, h_pre=h_pre, nxt=None if last else (norm1_g[i + 1], mods[i + 1]))
    return x
```

```python
import functools

import jax
import jax.numpy as jnp
from jax import lax
from jax.experimental import pallas as pl
from jax.experimental.pallas import tpu as pltpu

F32 = jnp.float32
BF16 = jnp.bfloat16
HI = lax.Precision.HIGHEST

D_MODEL = 4096
GRID_W = 64
NORM_EPS = 1e-6
D_GLA = D_MODEL // 4
D_MLA = D_MODEL // 2
D_LRU = D_MODEL // 4
GLA_HEADS = 4
GLA_DK = 128
GLA_DV = 256
GLA_GATE_RANK = 16
GLA_TAU = 16.0
GLA_CHUNK = 64
GLA_QSCALE = GLA_DK ** -0.5
MLA_HEADS = 16
MLA_NOPE = 128
MLA_ROPE = 64
MLA_V = 128
MLA_Q_RANK = 768
MLA_KV_RANK = 512
MLA_SCALE = (MLA_NOPE + MLA_ROPE) ** -0.5
LOG2E = 1.4426950408889634
ROPE_BASE = 10000.0
LRU_BLOCKS = 8
LRU_BW = 128
LRU_CONV = 4
LRU_C = 8.0
N_GROUPS = 8
EXPERTS_PER_GROUP = 8
N_EXPERTS = 64
TOP_K = 2
D_EXPERT = 256

LANES = 128
SUBLANES = 8
VMEM_BYTES = 64 << 20
VMEM_LIMIT = VMEM_BYTES - (8 << 20)
MOE_BLOCK = 192
MOE_SLOTS = 3

QDN_W, AL_W, KR_W = MLA_Q_RANK, LANES, LANES
COL_QDN = 0
COL_AL = 768
COL_KR = 896
COL_V = 1024
COL_G = 2048
COL_XB = 3072
COL_GB = 4096
COL_Q = 5120
COL_K = 5632
COL_KVDN = 6144
D_INP = 6656
HEAD_W = 256
VT_ROWS = MLA_V + 16
HEADS_PER_STEP = 4


def _params(sem, vmem=VMEM_LIMIT):
    return pltpu.CompilerParams(dimension_semantics=sem, vmem_limit_bytes=vmem)


def _tile(n, pref):
    t = min(n, pref)
    while n % t or t % SUBLANES:
        t -= 1
    return t


def _silu(x):
    return x * jax.nn.sigmoid(x)


def _softplus(x):
    return jnp.maximum(x, 0.0) + jnp.log1p(jnp.exp(-jnp.abs(x)))


def _pack_halves(x):
    w = x.shape[1] // 2
    bits = lax.bitcast_convert_type(x.astype(BF16).astype(F32), jnp.int32)
    return bits[:, w:] | lax.shift_right_logical(bits[:, :w], jnp.int32(16))


def _unpack_halves(p):
    lo = lax.bitcast_convert_type(lax.shift_left(p, jnp.int32(16)), F32)
    hi = lax.bitcast_convert_type(p & jnp.int32(-65536), F32)
    return lo, hi


def _adaln_body(c_ref, w_ref, b_ref, o_ref):
    s = _silu(c_ref[...]).astype(BF16)
    o_ref[...] = jnp.dot(s, w_ref[...].astype(BF16), preferred_element_type=F32) + b_ref[...]


def adaln_all(cond, w_mod, b_mod):
    depth, d, n = w_mod.shape
    tn = 1024
    return pl.pallas_call(
        _adaln_body,
        grid=(depth, n // tn),
        in_specs=[pl.BlockSpec((SUBLANES, d), lambda l, j: (0, 0)),
                  pl.BlockSpec((None, d, tn), lambda l, j: (l, 0, j)),
                  pl.BlockSpec((None, 1, tn), lambda l, j: (l, 0, j))],
        out_specs=pl.BlockSpec((None, SUBLANES, tn), lambda l, j: (l, 0, j)),
        out_shape=jax.ShapeDtypeStruct((depth, SUBLANES, n), F32),
        compiler_params=_params(("arbitrary", "arbitrary")),
        name="adaln",
    )(cond, w_mod, b_mod.reshape(depth, 1, n))


def _norm_mod(x, g, shift, scale):
    y = x * lax.rsqrt(jnp.mean(x * x, axis=-1, keepdims=True) + NORM_EPS) * g
    return y * (1.0 + scale) + shift


def _norm_mod_body(x_ref, g_ref, sh_ref, sc_ref, o_ref):
    o_ref[...] = _norm_mod(x_ref[...], g_ref[...], sh_ref[...], sc_ref[...]).astype(o_ref.dtype)


def _norm_router_body(*refs, n_ctx):
    if n_ctx:
        (xc_ref, shc_ref, scc_ref, x_ref, sh_ref, sc_ref, g_ref, wr_ref, br_ref, o_ref, lg_ref) = refs
        is_ctx = pl.program_id(0) < n_ctx
        x = jnp.where(is_ctx, xc_ref[...], x_ref[...])
        sh = jnp.where(is_ctx, shc_ref[...], sh_ref[...])
        sc = jnp.where(is_ctx, scc_ref[...], sc_ref[...])
    else:
        x_ref, sh_ref, sc_ref, g_ref, wr_ref, br_ref, o_ref, lg_ref = refs
        x, sh, sc = x_ref[...], sh_ref[...], sc_ref[...]
    h = _norm_mod(x, g_ref[...], sh, sc)
    o_ref[...] = _pack_halves(h)
    lg_ref[...] = jnp.dot(h.astype(BF16), wr_ref[...], preferred_element_type=F32) + br_ref[...]


def norm_mod(x, g, shift, scale):
    bx, lx, d = x.shape
    tm = _tile(lx, 256)
    row = pl.BlockSpec((None, tm, d), lambda b, i: (b, i, 0))
    vec = pl.BlockSpec((1, d), lambda b, i: (0, 0))
    mod = pl.BlockSpec((None, 1, d), lambda b, i: (b, 0, 0))
    return pl.pallas_call(
        _norm_mod_body, grid=(bx, lx // tm),
        in_specs=[row, vec, mod, mod], out_specs=row,
        out_shape=jax.ShapeDtypeStruct((bx, lx, d), BF16),
        compiler_params=_params(("arbitrary", "arbitrary")), name="norm_mod",
    )(x, g.reshape(1, d), shift, scale)


def norm_router(x, g, shift, scale, wr, br, ctx=None):
    bsz, l, d = x.shape
    nr = wr.shape[1]
    lc = ctx[0].shape[1] if ctx is not None else 0
    tm = _tile(l, 256) if ctx is None else _tile(lc, 256)
    assert l % tm == 0
    n_ctx = bsz * lc // tm
    n_lat = bsz * l // tm
    per_b = l // tm
    lat = lambda f: jnp.maximum(f - n_ctx, 0)
    vec = lambda w: pl.BlockSpec((1, w), lambda f: (0, 0))
    in_specs, args = [], []
    if ctx is not None:
        xc, shc, scc = ctx
        one = pl.BlockSpec((None, 1, d), lambda f: (0, 0, 0))
        in_specs += [pl.BlockSpec((tm, d), lambda f: (jnp.minimum(f, n_ctx - 1), 0)), one, one]
        args += [xc.reshape(bsz * lc, d), shc, scc]
    mod = pl.BlockSpec((None, 1, d), lambda f: (lat(f) // per_b, 0, 0))
    in_specs += [pl.BlockSpec((tm, d), lambda f: (lat(f), 0)), mod, mod, vec(d),
                 pl.BlockSpec((d, nr), lambda f: (0, 0)), vec(nr)]
    args += [x.reshape(bsz * l, d), shift, scale, g.reshape(1, d), wr.astype(BF16), br]
    total = bsz * (lc + l)
    return pl.pallas_call(
        functools.partial(_norm_router_body, n_ctx=n_ctx), grid=(n_ctx + n_lat,), in_specs=in_specs,
        out_specs=[pl.BlockSpec((tm, d // 2), lambda f: (f, 0)), pl.BlockSpec((tm, nr), lambda f: (f, 0))],
        out_shape=[jax.ShapeDtypeStruct((total, d // 2), jnp.int32), jax.ShapeDtypeStruct((total, nr), F32)],
        compiler_params=_params(("arbitrary",)), name="norm_router",
    )(*args)


def _mm_body(a_ref, w_ref, o_ref):
    o_ref[...] = jnp.dot(a_ref[...], w_ref[...], preferred_element_type=F32).astype(o_ref.dtype)


def matmul(a, w, out_dtype=F32, tm_pref=1024, tn=512):
    m, k = a.shape
    n = w.shape[1]
    tm = _tile(m, tm_pref)
    return pl.pallas_call(
        _mm_body, grid=(m // tm, n // tn),
        in_specs=[pl.BlockSpec((tm, k), lambda i, j: (i, 0)),
                  pl.BlockSpec((k, tn), lambda i, j: (0, j))],
        out_specs=pl.BlockSpec((tm, tn), lambda i, j: (i, j)),
        out_shape=jax.ShapeDtypeStruct((m, n), out_dtype),
        compiler_params=_params(("arbitrary", "arbitrary")), name="matmul",
    )(a, w)


def _wout_body(ya_ref, yb_ref, yl_ref, w_ref, x_ref, g_ref, o_ref):
    acc = jnp.dot(ya_ref[...], w_ref[0:D_GLA, :], preferred_element_type=F32)
    acc += jnp.dot(yb_ref[...], w_ref[D_GLA:D_GLA + D_MLA, :], preferred_element_type=F32)
    acc += jnp.dot(yl_ref[...], w_ref[D_GLA + D_MLA:, :], preferred_element_type=F32)
    o_ref[...] = x_ref[...] + g_ref[...] * acc


def out_proj(ya, yb, yl, w, x, gate):
    bx, lx, d = x.shape
    tm = _tile(lx, 1024)
    tn = 512
    return pl.pallas_call(
        _wout_body, grid=(bx, lx // tm, d // tn),
        in_specs=[pl.BlockSpec((None, tm, D_GLA), lambda b, i, j: (b, i, 0)),
                  pl.BlockSpec((None, tm, D_MLA), lambda b, i, j: (b, i, 0)),
                  pl.BlockSpec((None, tm, D_LRU), lambda b, i, j: (b, i, 0)),
                  pl.BlockSpec((d, tn), lambda b, i, j: (0, j)),
                  pl.BlockSpec((None, tm, tn), lambda b, i, j: (b, i, j)),
                  pl.BlockSpec((None, 1, tn), lambda b, i, j: (b, 0, j))],
        out_specs=pl.BlockSpec((None, tm, tn), lambda b, i, j: (b, i, j)),
        out_shape=jax.ShapeDtypeStruct((bx, lx, d), F32),
        compiler_params=_params(("arbitrary", "arbitrary", "arbitrary")), name="out_proj",
    )(ya, yb, yl, w, x, gate)


def _gla_body(*refs, reverse, final, nchunk):
    if final:
        (q_ref, k_ref, v_ref, al_ref, wa_ref, ba_ref, s0_ref, of_ref, g_ref, ng_ref,
         o_ref, sfin_ref, s_scr) = refs
    else:
        q_ref, k_ref, v_ref, al_ref, wa_ref, ba_ref, s0_ref, o_ref, sfin_ref, s_scr = refs
    C = GLA_CHUNK

    @pl.when(pl.program_id(1) == 0)
    def _():
        s_scr[...] = s0_ref[...]

    r_i = lax.broadcasted_iota(jnp.int32, (C, C), 0)
    c_i = lax.broadcasted_iota(jnp.int32, (C, C), 1)
    keep = (c_i >= r_i) if reverse else (c_i <= r_i)
    tri = jnp.where(keep, 1.0, 0.0).astype(BF16)
    ref_row = C - 1 - C // 2 if reverse else C // 2
    last_row = 0 if reverse else C - 1

    order = range(nchunk - 1, -1, -1) if reverse else range(nchunk)
    for ci in order:
        rows = slice(ci * C, (ci + 1) * C)
        z = jnp.dot(al_ref[rows, :].astype(BF16), wa_ref[...], preferred_element_type=F32) + ba_ref[...]
        logd = (jnp.minimum(z, 0.0) - jnp.log1p(jnp.exp(-jnp.abs(z)))) * (1.0 / GLA_TAU)
        l_hi = logd.astype(BF16)
        rem = logd - l_hi.astype(F32)
        l_mid = rem.astype(BF16)
        l_lo = (rem - l_mid.astype(F32)).astype(BF16)
        b = (jnp.dot(tri, l_hi, preferred_element_type=F32) + jnp.dot(tri, l_mid, preferred_element_type=F32)
             + jnp.dot(tri, l_lo, preferred_element_type=F32))
        b_ref = b[ref_row:ref_row + 1, :]
        b_last = b[last_row:last_row + 1, :]
        q = q_ref[rows, :] * GLA_QSCALE
        k = k_ref[rows, :]
        qe = (q * jnp.exp(b - b_ref)).astype(BF16)
        ke = (k * jnp.exp(b_ref - b)).astype(BF16)
        qin = (q * jnp.exp(b)).astype(BF16)
        kst = k * jnp.exp(b_last - b)
        for h in range(GLA_HEADS):
            hs = slice(h * GLA_DK, (h + 1) * GLA_DK)
            vs = slice(h * GLA_DV, (h + 1) * GLA_DV)
            v = v_ref[rows, vs].astype(BF16)
            att = lax.dot_general(qe[:, hs], ke[:, hs], (((1,), (1,)), ((), ())),
                                  preferred_element_type=F32)
            att = jnp.where(keep, att, 0.0).astype(BF16)
            s = s_scr[h]
            o = (jnp.dot(att, v, preferred_element_type=F32)
                 + jnp.dot(qin[:, hs], s.astype(BF16), preferred_element_type=F32))
            d_col = jnp.exp(jnp.broadcast_to(b_last[:, hs], (GLA_DK, GLA_DK))).T
            s_scr[h] = (jnp.concatenate([d_col, d_col], axis=1) * s
                        + jnp.dot(kst[:, hs].T.astype(BF16), v, preferred_element_type=F32))
            if final:
                o = o + of_ref[rows, vs]
                y = o * lax.rsqrt(jnp.mean(o * o, axis=-1, keepdims=True) + NORM_EPS) * ng_ref[...]
                o_ref[rows, vs] = (y * _silu(g_ref[rows, vs])).astype(o_ref.dtype)
            else:
                o_ref[rows, vs] = o
    sfin_ref[...] = s_scr[...]


def gla_pass(p, wa, ba, s0, reverse, o_fwd=None, norm_g=None):
    bsz, l, _ = p.shape
    t = _tile(l, 256)
    nt = l // t
    final = o_fwd is not None

    def rix(i):
        return nt - 1 - i if reverse else i

    def col(width, off):
        return pl.BlockSpec((None, t, width), lambda b, i: (b, rix(i), off // width))

    hq = GLA_HEADS * GLA_DK
    state = pl.BlockSpec((None, GLA_HEADS, GLA_DK, GLA_DV), lambda b, i: (b, 0, 0, 0))
    in_specs = [col(hq, COL_Q), col(hq, COL_K), col(D_GLA, COL_V), col(AL_W, COL_AL),
                pl.BlockSpec((AL_W, hq), lambda b, i: (0, 0)),
                pl.BlockSpec((1, hq), lambda b, i: (0, 0)), state]
    args = [p, p, p, p, wa, ba, s0]
    orow = pl.BlockSpec((None, t, D_GLA), lambda b, i: (b, rix(i), 0))
    if final:
        in_specs += [orow, col(D_GLA, COL_G), pl.BlockSpec((1, GLA_DV), lambda b, i: (0, 0))]
        args += [o_fwd, p, norm_g.reshape(1, GLA_DV)]
    return pl.pallas_call(
        functools.partial(_gla_body, reverse=reverse, final=final, nchunk=t // GLA_CHUNK),
        grid=(bsz, nt), in_specs=in_specs, out_specs=[orow, state],
        out_shape=[jax.ShapeDtypeStruct((bsz, l, D_GLA), BF16 if final else F32),
                   jax.ShapeDtypeStruct((bsz, GLA_HEADS, GLA_DK, GLA_DV), F32)],
        scratch_shapes=[pltpu.VMEM((GLA_HEADS, GLA_DK, GLA_DV), F32)],
        compiler_params=_params(("arbitrary", "arbitrary")), name="gla",
    )(*args)


def _lru_body(*refs, reverse, final, t, nt):
    if final:
        (x_ref, xp_ref, xn_ref, cw_ref, cb_ref, wg_ref, bg_ref, lam_ref, h0_ref, hf_ref, gb_ref,
         o_ref, hfin_ref, h_scr, hbuf) = refs
    else:
        (x_ref, xp_ref, xn_ref, cw_ref, cb_ref, wg_ref, bg_ref, lam_ref, h0_ref,
         o_ref, hfin_ref, h_scr) = refs
        hbuf = o_ref
    i = pl.program_id(1)
    ti = nt - 1 - i if reverse else i

    @pl.when(i == 0)
    def _():
        h_scr[...] = h0_ref[...]

    ngroup = t // SUBLANES
    grp = lambda v: v.reshape(v.shape[0] // SUBLANES, SUBLANES, v.shape[1])
    sub = lax.broadcasted_iota(jnp.int32, (1, SUBLANES, 1), 1)
    x3 = grp(x_ref[...])
    xp3 = grp(jnp.where(ti == 0, 0.0, xp_ref[...]))
    xn3 = grp(jnp.where(ti == nt - 1, 0.0, xn_ref[...]))

    def row_shift(k):
        rot = pltpu.roll(x3, k % SUBLANES, axis=1)
        if k > 0:
            nb_rot = jnp.concatenate([pltpu.roll(xp3, k % SUBLANES, axis=1), rot[:-1]], axis=0)
            return jnp.where(sub >= k, rot, nb_rot)
        nb_rot = jnp.concatenate([rot[1:], pltpu.roll(xn3, k % SUBLANES, axis=1)], axis=0)
        return jnp.where(sub < SUBLANES + k, rot, nb_rot)

    cw = cw_ref[...]
    xc = (cb_ref[...] + row_shift(2) * cw[0:1, :] + row_shift(1) * cw[1:2, :]
          + x3 * cw[2:3, :] + row_shift(-1) * cw[3:4, :]).reshape(t, D_LRU)

    zs = []
    for n in range(LRU_BLOCKS):
        zs.append(jnp.dot(xc[:, n * LRU_BW:(n + 1) * LRU_BW].astype(BF16), wg_ref[n],
                          preferred_element_type=F32))
    z_r = jnp.concatenate([z[:, :LRU_BW] for z in zs], axis=1) + bg_ref[0:1, :]
    z_i = jnp.concatenate([z[:, LRU_BW:] for z in zs], axis=1) + bg_ref[1:2, :]
    log_a = -LRU_C * jax.nn.sigmoid(z_r) * _softplus(-lam_ref[...])
    a = jnp.exp(log_a)
    th = jnp.tanh(log_a)
    one_minus_a2 = -2.0 * th / (1.0 - th)
    u = jnp.sqrt(one_minus_a2) * jax.nn.sigmoid(z_i) * xc

    a, u = grp(a), grp(u)
    s = 1
    while s < SUBLANES:
        if reverse:
            ok = sub < SUBLANES - s
            a_s = pltpu.roll(a, SUBLANES - s, axis=1)
            u_s = pltpu.roll(u, SUBLANES - s, axis=1)
        else:
            ok = sub >= s
            a_s = pltpu.roll(a, s, axis=1)
            u_s = pltpu.roll(u, s, axis=1)
        u = jnp.where(ok, a * u_s + u, u)
        a = jnp.where(ok, a * a_s, a)
        s *= 2
    edge = 0 if reverse else SUBLANES - 1
    h_prev = h_scr[...]
    for gi in (range(ngroup - 1, -1, -1) if reverse else range(ngroup)):
        rows = slice(gi * SUBLANES, (gi + 1) * SUBLANES)
        hg = a[gi] * h_prev + u[gi]
        h_prev = hg[edge:edge + 1, :]
        hbuf[rows, :] = hg
    h_scr[...] = h_prev
    hfin_ref[...] = h_prev
    if final:
        g = gb_ref[...]
        gelu = 0.5 * g * (1.0 + jnp.tanh(0.7978845608028654 * (g + 0.044715 * g * g * g)))
        o_ref[...] = (gelu * (hf_ref[...] + hbuf[...])).astype(o_ref.dtype)


def lru_pass(p, cw, cb, wg, bg, lam, h0, reverse, h_fwd=None):
    bsz, l, _ = p.shape
    t = _tile(l, 256)
    nt = l // t
    tb = t // SUBLANES
    nb8 = l // SUBLANES
    final = h_fwd is not None
    xoff = COL_XB // D_LRU

    def rix(i):
        return nt - 1 - i if reverse else i

    vec = lambda r: pl.BlockSpec((r, D_LRU), lambda b, i: (0, 0))
    one = pl.BlockSpec((None, 1, D_LRU), lambda b, i: (b, 0, 0))
    orow = pl.BlockSpec((None, t, D_LRU), lambda b, i: (b, rix(i), 0))
    in_specs = [pl.BlockSpec((None, t, D_LRU), lambda b, i: (b, rix(i), xoff)),
                pl.BlockSpec((None, SUBLANES, D_LRU),
                             lambda b, i: (b, jnp.maximum(rix(i) * tb - 1, 0), xoff)),
                pl.BlockSpec((None, SUBLANES, D_LRU),
                             lambda b, i: (b, jnp.minimum((rix(i) + 1) * tb, nb8 - 1), xoff)),
                vec(LRU_CONV), vec(1),
                pl.BlockSpec((LRU_BLOCKS, LRU_BW, 2 * LRU_BW), lambda b, i: (0, 0, 0)),
                vec(2), vec(1), one]
    args = [p, p, p, cw, cb.reshape(1, D_LRU), wg, bg, lam.reshape(1, D_LRU), h0]
    if final:
        in_specs += [orow, pl.BlockSpec((None, t, D_LRU), lambda b, i: (b, rix(i), COL_GB // D_LRU))]
        args += [h_fwd, p]
    return pl.pallas_call(
        functools.partial(_lru_body, reverse=reverse, final=final, t=t, nt=nt),
        grid=(bsz, nt), in_specs=in_specs, out_specs=[orow, one],
        out_shape=[jax.ShapeDtypeStruct((bsz, l, D_LRU), BF16 if final else F32),
                   jax.ShapeDtypeStruct((bsz, 1, D_LRU), F32)],
        scratch_shapes=[pltpu.VMEM((1, D_LRU), F32)] + ([pltpu.VMEM((t, D_LRU), F32)] if final else []),
        compiler_params=_params(("arbitrary", "arbitrary")), name="lru",
    )(*args)


def _rms(x, width):
    return x * lax.rsqrt(jnp.sum(x * x, axis=-1, keepdims=True) * (1.0 / width) + NORM_EPS)


def _rope(r, cos, sin):
    lane = lax.broadcasted_iota(jnp.int32, r.shape, 1)
    first = (lane % 32) < 16
    rot = jnp.where(first, pltpu.roll(r, LANES - 16, axis=1), pltpu.roll(r, 16, axis=1))
    return r * cos + rot * sin


def _qproj_body(x_ref, gn_ref, w_ref, hg_ref, cos_ref, sin_ref, o_ref, xn_scr):
    @pl.when(pl.program_id(2) == 0)
    def _():
        xn_scr[...] = (_rms(x_ref[...], MLA_Q_RANK) * gn_ref[...]).astype(BF16)

    a = jnp.dot(xn_scr[...], w_ref[...], preferred_element_type=F32)
    for hh in range(HEADS_PER_STEP):
        c0 = hh * HEAD_W
        qn = _rms(a[:, c0:c0 + MLA_NOPE], MLA_NOPE) * hg_ref[:, :MLA_NOPE]
        qr = _rms(a[:, c0 + MLA_NOPE:c0 + HEAD_W], MLA_ROPE) * hg_ref[:, MLA_NOPE:]
        qr = _rope(qr, cos_ref[...], sin_ref[...])
        o_ref[hh] = (jnp.concatenate([qn, qr], axis=1) * (MLA_SCALE * LOG2E)).astype(o_ref.dtype)


def q_proj(p, gn, w, hg, cos, sin):
    bsz, l, _ = p.shape
    tm = _tile(l, 512)
    hw = HEADS_PER_STEP * HEAD_W
    return pl.pallas_call(
        _qproj_body, grid=(bsz, l // tm, MLA_HEADS // HEADS_PER_STEP),
        in_specs=[pl.BlockSpec((None, tm, QDN_W), lambda b, i, h: (b, i, COL_QDN // QDN_W)),
                  pl.BlockSpec((1, QDN_W), lambda b, i, h: (0, 0)),
                  pl.BlockSpec((QDN_W, hw), lambda b, i, h: (0, h)),
                  pl.BlockSpec((1, HEAD_W), lambda b, i, h: (0, 0)),
                  pl.BlockSpec((tm, LANES), lambda b, i, h: (i, 0)),
                  pl.BlockSpec((tm, LANES), lambda b, i, h: (i, 0))],
        out_specs=pl.BlockSpec((None, HEADS_PER_STEP, tm, HEAD_W), lambda b, i, h: (b, h, i, 0)),
        out_shape=jax.ShapeDtypeStruct((bsz, MLA_HEADS, l, HEAD_W), BF16),
        scratch_shapes=[pltpu.VMEM((tm, QDN_W), BF16)],
        compiler_params=_params(("arbitrary", "arbitrary", "arbitrary")), name="q_proj",
    )(p, gn.reshape(1, QDN_W), w, hg, cos, sin)


def _kvproj_body(x_ref, kr_ref, gn_ref, w_ref, hg_ref, cos_ref, sin_ref, k_ref, v_ref, xn_scr, kr_scr):
    @pl.when(pl.program_id(2) == 0)
    def _():
        xn_scr[...] = (_rms(x_ref[...], MLA_KV_RANK) * gn_ref[...]).astype(BF16)
        kr = _rms(kr_ref[...], MLA_ROPE) * hg_ref[:, MLA_NOPE:]
        kr_scr[...] = _rope(kr, cos_ref[...], sin_ref[...]).astype(BF16)

    a = jnp.dot(xn_scr[...], w_ref[...], preferred_element_type=F32)
    for hh in range(HEADS_PER_STEP):
        c0 = hh * HEAD_W
        kn = _rms(a[:, c0:c0 + MLA_NOPE], MLA_NOPE) * hg_ref[:, :MLA_NOPE]
        k_ref[hh, :, :MLA_NOPE] = kn.astype(k_ref.dtype)
        k_ref[hh, :, MLA_NOPE:] = kr_scr[...]
        v_ref[hh, :MLA_V, :] = a[:, c0 + MLA_NOPE:c0 + HEAD_W].T.astype(v_ref.dtype)
        v_ref[hh, MLA_V:, :] = jnp.ones((VT_ROWS - MLA_V, a.shape[0]), v_ref.dtype)


def kv_proj(p, gn, w, hg, cos, sin):
    bsz, l, _ = p.shape
    tm = _tile(l, 512)
    hw = HEADS_PER_STEP * HEAD_W
    return pl.pallas_call(
        _kvproj_body, grid=(bsz, l // tm, MLA_HEADS // HEADS_PER_STEP),
        in_specs=[pl.BlockSpec((None, tm, MLA_KV_RANK), lambda b, i, h: (b, i, COL_KVDN // MLA_KV_RANK)),
                  pl.BlockSpec((None, tm, KR_W), lambda b, i, h: (b, i, COL_KR // KR_W)),
                  pl.BlockSpec((1, MLA_KV_RANK), lambda b, i, h: (0, 0)),
                  pl.BlockSpec((MLA_KV_RANK, hw), lambda b, i, h: (0, h)),
                  pl.BlockSpec((1, HEAD_W), lambda b, i, h: (0, 0)),
                  pl.BlockSpec((tm, LANES), lambda b, i, h: (i, 0)),
                  pl.BlockSpec((tm, LANES), lambda b, i, h: (i, 0))],
        out_specs=[pl.BlockSpec((None, HEADS_PER_STEP, tm, HEAD_W), lambda b, i, h: (b, h, i, 0)),
                   pl.BlockSpec((None, HEADS_PER_STEP, VT_ROWS, tm), lambda b, i, h: (b, h, 0, i))],
        out_shape=[jax.ShapeDtypeStruct((bsz, MLA_HEADS, l, HEAD_W), BF16),
                   jax.ShapeDtypeStruct((bsz, MLA_HEADS, VT_ROWS, l), BF16)],
        scratch_shapes=[pltpu.VMEM((tm, MLA_KV_RANK), BF16), pltpu.VMEM((tm, KR_W), BF16)],
        compiler_params=_params(("arbitrary", "arbitrary", "arbitrary")), name="kv_proj",
    )(p, p, gn.reshape(1, MLA_KV_RANK), w, hg, cos, sin)


def _scores_t(k, q):
    return lax.dot_general(k, q, (((1,), (1,)), ((), ())), preferred_element_type=F32)


def _attn_body(*refs, tk, n_lat):
    if n_lat:
        q_ref, kc_ref, vc_ref, k_ref, v_ref, o_ref, m_scr, acc_scr, s_scr = refs
    else:
        q_ref, kc_ref, vc_ref, o_ref = refs

    def qk(slot, off):
        s_scr[slot] = _scores_t(k_ref[pl.ds(off, tk), :], q_ref[...])

    if n_lat:
        qk(0, 0)

    sc = _scores_t(kc_ref[...], q_ref[...])
    m0 = jnp.max(sc, axis=0, keepdims=True)
    acc = jnp.dot(vc_ref[...], jnp.exp2(sc - m0).astype(BF16), preferred_element_type=F32)

    if n_lat:
        m_scr[...] = m0
        acc_scr[...] = acc

        def consume(slot, off):
            s = s_scr[slot]
            m = m_scr[...]
            m_new = jnp.maximum(m, jnp.max(s, axis=0, keepdims=True))
            p = jnp.exp2(s - m_new).astype(BF16)
            m_scr[...] = m_new
            acc_scr[...] = (jnp.exp2(m - m_new) * acc_scr[...]
                            + jnp.dot(v_ref[:, pl.ds(off, tk)], p, preferred_element_type=F32))

        def pair(jj, carry):
            off = pl.multiple_of(jj * (2 * tk), 2 * tk)
            qk(1, off + tk)
            consume(0, off)
            qk(0, off + 2 * tk)
            consume(1, off + tk)
            return carry

        lax.fori_loop(0, n_lat // 2 - 1, pair, 0)
        off = (n_lat - 2) * tk
        qk(1, off + tk)
        consume(0, off)
        consume(1, off + tk)
        acc = acc_scr[...]
    o_ref[...] = (acc[:MLA_V] / acc[MLA_V:MLA_V + 1]).T.astype(o_ref.dtype)


def attention(q, kc, vc, k=None, v=None):
    bsz, nh, lq, _ = q.shape
    lc = kc.shape[2]
    tq = _tile(lq, 1024)
    keys = lambda n: pl.BlockSpec((None, None, n, HEAD_W), lambda b, h, i: (b, h, 0, 0))
    vals = lambda n: pl.BlockSpec((None, None, VT_ROWS, n), lambda b, h, i: (b, h, 0, 0))
    in_specs = [pl.BlockSpec((None, None, tq, HEAD_W), lambda b, h, i: (b, h, i, 0)), keys(lc), vals(lc)]
    args = [q, kc, vc]
    tk = n_lat = 0
    scratch = []
    if k is not None:
        lk = k.shape[2]
        tk = _tile(lk // 2, 1024)
        assert tk % LANES == 0 and lk % (2 * tk) == 0
        n_lat = lk // tk
        in_specs += [keys(lk), vals(lk)]
        args += [k, v]
        scratch = [pltpu.VMEM((1, tq), F32), pltpu.VMEM((VT_ROWS, tq), F32), pltpu.VMEM((2, tk, tq), F32)]
    return pl.pallas_call(
        functools.partial(_attn_body, tk=tk, n_lat=n_lat),
        grid=(bsz, nh, lq // tq), in_specs=in_specs,
        out_specs=pl.BlockSpec((None, tq, MLA_V), lambda b, h, i: (b, i, h)),
        out_shape=jax.ShapeDtypeStruct((bsz, lq, nh * MLA_V), BF16),
        scratch_shapes=scratch,
        compiler_params=_params(("arbitrary", "arbitrary", "arbitrary")), name="attention",
    )(*args)


def _row_copy(src_hbm, row, buf, slot, r, sem):
    return pltpu.make_async_copy(src_hbm.at[pl.ds(row, 1), :], buf.at[slot, pl.ds(r, 1), :], sem.at[slot])


def _moe_body(be_ref, nu_ref, first_ref, tok_ref, h_hbm, w13_ref, w2_ref, o_ref, xbuf, sem, *, n_assign):
    j = pl.program_id(0)
    nu = nu_ref[0]
    slot = j % MOE_SLOTS
    ahead = MOE_SLOTS - 1

    def rows(blk, slot, start):
        base = first_ref[blk]

        def body(r, carry):
            tok = tok_ref[jnp.minimum(base + r, n_assign - 1)]
            cp = _row_copy(h_hbm, tok, xbuf, slot, r, sem)
            if start:
                cp.start()
            else:
                cp.wait()
            return carry

        lax.fori_loop(0, MOE_BLOCK, body, 0, unroll=8)

    for first in range(ahead):
        @pl.when((j == 0) & (first < nu))
        def _():
            rows(first, first, True)

    @pl.when(j + ahead < nu)
    def _():
        rows(j + ahead, (j + ahead) % MOE_SLOTS, True)

    @pl.when(j < nu)
    def _():
        rows(j, slot, False)
        x_lo, x_hi = _unpack_halves(xbuf[slot])
        half = w13_ref.shape[0] // 2
        hid = (jnp.dot(x_lo.astype(BF16), w13_ref[:half, :].astype(BF16), preferred_element_type=F32)
               + jnp.dot(x_hi.astype(BF16), w13_ref[half:, :].astype(BF16), preferred_element_type=F32))
        act = (_silu(hid[:, :D_EXPERT]) * hid[:, D_EXPERT:]).astype(BF16)
        o_ref[...] = _pack_halves(jnp.dot(act, w2_ref[...].astype(BF16), preferred_element_type=F32))

    @pl.when(j >= nu)
    def _():
        o_ref[...] = jnp.zeros_like(o_ref)


def moe_experts(h, tok_sorted, blk_first, blk_exp, n_used, w13, w2, layer):
    m, dp = h.shape
    d = 2 * dp
    nb = blk_first.shape[0]
    grid_spec = pltpu.PrefetchScalarGridSpec(
        num_scalar_prefetch=4, grid=(nb,),
        in_specs=[pl.BlockSpec(memory_space=pl.ANY),
                  pl.BlockSpec((None, None, d, 2 * D_EXPERT), lambda j, be, nu, fi, tok: (layer, be[j], 0, 0)),
                  pl.BlockSpec((None, None, D_EXPERT, d), lambda j, be, nu, fi, tok: (layer, be[j], 0, 0))],
        out_specs=pl.BlockSpec((MOE_BLOCK, dp), lambda j, be, nu, fi, tok: (j, 0)),
        scratch_shapes=[pltpu.VMEM((MOE_SLOTS, MOE_BLOCK, dp), jnp.int32),
                        pltpu.SemaphoreType.DMA((MOE_SLOTS,))])
    return pl.pallas_call(
        functools.partial(_moe_body, n_assign=tok_sorted.shape[0]), grid_spec=grid_spec,
        out_shape=jax.ShapeDtypeStruct((nb * MOE_BLOCK, dp), jnp.int32),
        compiler_params=_params(("arbitrary",)), name="moe_experts",
    )(blk_exp, n_used, blk_first, tok_sorted, h, w13, w2)


def _combine_body(dest_ref, out_hbm, x_ref, g_ref, tg_ref, *rest, tok_off, t, n_tiles, with_norm):
    if with_norm:
        ng_ref, sh_ref, sc_ref, o_ref, h_ref, buf_a, buf_b, sem_a, sem_b = rest
    else:
        o_ref, buf_a, buf_b, sem_a, sem_b = rest
    f = pl.program_id(0) * pl.num_programs(1) + pl.program_id(1)
    slot = f % 2

    def rows(tile, slot, start):
        base = (tok_off + tile * t) * TOP_K

        def body(r, carry):
            ca = _row_copy(out_hbm, dest_ref[base + TOP_K * r], buf_a, slot, r, sem_a)
            cb = _row_copy(out_hbm, dest_ref[base + TOP_K * r + 1], buf_b, slot, r, sem_b)
            if start:
                ca.start()
                cb.start()
            else:
                ca.wait()
                cb.wait()
            return carry

        lax.fori_loop(0, t, body, 0, unroll=8)

    @pl.when(f == 0)
    def _():
        rows(0, 0, True)

    @pl.when(f + 1 < n_tiles)
    def _():
        rows(f + 1, 1 - slot, True)

    rows(f, slot, False)
    d = x_ref.shape[1]
    half = d // 2
    tg0, tg1 = tg_ref[:, :LANES], tg_ref[:, LANES:]
    ss = jnp.zeros((t, LANES), F32)
    for c in range(half // LANES):
        wcols = slice(c * LANES, (c + 1) * LANES)
        a_lo, a_hi = _unpack_halves(buf_a[slot, :, wcols])
        b_lo, b_hi = _unpack_halves(buf_b[slot, :, wcols])
        for cols, ya, yb in ((wcols, a_lo, b_lo),
                             (slice(half + c * LANES, half + (c + 1) * LANES), a_hi, b_hi)):
            y = x_ref[:, cols] + g_ref[:, cols] * (tg0 * ya + tg1 * yb)
            o_ref[:, cols] = y
            ss = ss + y * y
    if with_norm:
        rs = lax.rsqrt(jnp.sum(ss, axis=-1, keepdims=True) * (1.0 / d) + NORM_EPS)
        rs = jnp.broadcast_to(rs, (t, LANES))
        for c in range(d // LANES):
            cols = slice(c * LANES, (c + 1) * LANES)
            h_ref[:, cols] = (o_ref[:, cols] * rs * ng_ref[:, cols] * (1.0 + sc_ref[:, cols])
                              + sh_ref[:, cols]).astype(h_ref.dtype)


def moe_combine(x, gate, out, dest, tok_gate, tok_off, norm=None):
    bx, lx, d = x.shape
    t = _tile(lx, 128)
    nt = lx // t
    tg = jnp.repeat(tok_gate[tok_off:tok_off + bx * lx], LANES, axis=1).reshape(bx, lx, TOP_K * LANES)
    row = pl.BlockSpec((None, t, d), lambda b, i, dst: (b, i, 0))
    mod = pl.BlockSpec((None, 1, d), lambda b, i, dst: (b, 0, 0))
    in_specs = [pl.BlockSpec(memory_space=pl.ANY), row, mod,
                pl.BlockSpec((None, t, TOP_K * LANES), lambda b, i, dst: (b, i, 0))]
    args = [dest, out, x, gate, tg]
    out_specs, out_shape = row, jax.ShapeDtypeStruct((bx, lx, d), F32)
    if norm is not None:
        in_specs += [pl.BlockSpec((1, d), lambda b, i, dst: (0, 0)), mod, mod]
        args += [norm[0].reshape(1, d), norm[1], norm[2]]
        out_specs, out_shape = [row, row], [out_shape, jax.ShapeDtypeStruct((bx, lx, d), BF16)]
    grid_spec = pltpu.PrefetchScalarGridSpec(
        num_scalar_prefetch=1, grid=(bx, nt), in_specs=in_specs, out_specs=out_specs,
        scratch_shapes=[pltpu.VMEM((2, t, d // 2), jnp.int32), pltpu.VMEM((2, t, d // 2), jnp.int32),
                        pltpu.SemaphoreType.DMA((2,)), pltpu.SemaphoreType.DMA((2,))])
    return pl.pallas_call(
        functools.partial(_combine_body, tok_off=tok_off, t=t, n_tiles=bx * nt, with_norm=norm is not None),
        grid_spec=grid_spec, out_shape=out_shape,
        compiler_params=_params(("arbitrary", "arbitrary")), name="moe_combine",
    )(*args)


def moe_route(h, logits, w13, w2, layer):
    m = h.shape[0]
    g_logit = logits[:, :N_GROUPS]
    e_logit = logits[:, N_GROUPS:N_GROUPS + N_EXPERTS].reshape(m, N_GROUPS, EXPERTS_PER_GROUP)
    g_idx = jnp.argmax(g_logit, axis=-1)
    p_grp = jnp.take_along_axis(jax.nn.softmax(g_logit, axis=-1), g_idx[:, None], axis=1)
    e_sel = jnp.take_along_axis(e_logit, g_idx[:, None, None], axis=1)[:, 0]
    top_v, top_i = lax.top_k(e_sel, TOP_K)
    gate = (p_grp * jax.nn.softmax(top_v, axis=-1)).reshape(-1)
    e_flat = (g_idx[:, None] * EXPERTS_PER_GROUP + top_i).reshape(-1).astype(jnp.int32)

    a = m * TOP_K
    nb = -(-a // MOE_BLOCK) + N_EXPERTS
    order = jnp.argsort(e_flat).astype(jnp.int32)
    rank = jnp.argsort(order).astype(jnp.int32)
    e_s = e_flat[order]
    experts = jnp.arange(N_EXPERTS, dtype=jnp.int32)
    start = jnp.searchsorted(e_s, experts, side='left').astype(jnp.int32)
    counts = jnp.searchsorted(e_s, experts, side='right').astype(jnp.int32) - start
    padded = -(-counts // MOE_BLOCK) * MOE_BLOCK
    pad_end = jnp.cumsum(padded)
    pad_start = pad_end - padded
    blk_ids = jnp.arange(nb, dtype=jnp.int32)
    blk_raw = jnp.searchsorted(pad_end, blk_ids * MOE_BLOCK, side='right')
    blk_exp = jnp.minimum(blk_raw, N_EXPERTS - 1).astype(jnp.int32)
    n_used = (pad_end[-1] // MOE_BLOCK).astype(jnp.int32).reshape(1)
    blk_first = jnp.where(blk_raw < N_EXPERTS,
                          start[blk_exp] + blk_ids * MOE_BLOCK - pad_start[blk_exp], 0).astype(jnp.int32)
    tok_sorted = order // TOP_K
    dest = (pad_start[e_flat] + rank - start[e_flat]).astype(jnp.int32)

    out = moe_experts(h, tok_sorted, blk_first, blk_exp, n_used, w13, w2, layer)
    return out, dest, gate.reshape(m, TOP_K)


def _prep_w_in(w):
    sizes = (512, 512, 1024, 1024, 32, 768, 512, 64, 1024, 1024)
    offs = [0]
    for s in sizes:
        offs.append(offs[-1] + s)
    q, k, v, g, al, qdn, kvdn, kr, xb, gb = [w[:, offs[i]:offs[i + 1]] for i in range(10)]
    zpad = lambda t, n: jnp.pad(t, ((0, 0), (0, n - t.shape[1])))
    return jnp.concatenate([qdn, zpad(al, AL_W), zpad(kr, KR_W), v, g, xb, gb, q, k, kvdn],
                           axis=1).astype(BF16)


def _prep_w_uq(w):
    w = w.reshape(MLA_Q_RANK, MLA_HEADS, MLA_NOPE + MLA_ROPE)
    w = jnp.pad(w, ((0, 0), (0, 0), (0, HEAD_W - MLA_NOPE - MLA_ROPE)))
    return w.reshape(MLA_Q_RANK, MLA_HEADS * HEAD_W).astype(BF16)


def _head_gain(g):
    return jnp.pad(g, (0, HEAD_W - g.shape[0])).reshape(1, HEAD_W)


def _rope_tables(rows):
    n = MLA_ROPE // 4
    inv = ROPE_BASE ** (-jnp.arange(n, dtype=F32) / n)
    r = jnp.repeat(jnp.arange(rows, dtype=F32), GRID_W)
    c = jnp.tile(jnp.arange(GRID_W, dtype=F32), rows)
    ar, ac = r[:, None] * inv, c[:, None] * inv
    l = ar.shape[0]
    cos = jnp.concatenate([jnp.cos(ar), jnp.cos(ar), jnp.cos(ac), jnp.cos(ac),
                           jnp.ones((l, LANES - MLA_ROPE), F32)], axis=1)
    sin = jnp.concatenate([-jnp.sin(ar), jnp.sin(ar), -jnp.sin(ac), jnp.sin(ac),
                           jnp.zeros((l, LANES - MLA_ROPE), F32)], axis=1)
    return cos, sin


def _gla_gate_w(w_a2, b_a, d):
    hq = GLA_HEADS * GLA_DK
    w = jnp.zeros((AL_W, hq), F32).at[d * GLA_GATE_RANK:(d + 1) * GLA_GATE_RANK].set(w_a2[d])
    return w.astype(BF16), b_a[d].reshape(1, hq)


def _lru_gate_w(w_gate, d):
    return jnp.concatenate([w_gate[d, 0], w_gate[d, 1]], axis=-1).astype(BF16)


def _layer(x, xc, mods, rope, prm, moe_w13, moe_w2, layer, last, h_pre=None, nxt=None):
    (norm1_g, norm2_g, w_in, gla_w_a2, gla_b_a, gla_norm_g,
     mla_q_norm_g, mla_kv_norm_g, mla_w_uq, mla_w_ukv, mla_q_head_g, mla_k_head_g,
     lru_conv_w, lru_conv_b, lru_w_gate, lru_b_gate, lru_lambda, w_out,
     moe_w_grp, moe_b_grp, moe_w_exp, moe_b_exp) = prm
    bsz, l, d = x.shape
    lc = xc.shape[1]
    ctx_out = not last
    lat = lambda k: mods[:bsz, k].reshape(bsz, 1, d)
    ctx = lambda k: jnp.broadcast_to(mods[bsz, k].reshape(1, 1, d), (bsz, 1, d))

    w_in_p = _prep_w_in(w_in)
    if h_pre is None:
        h = norm_mod(x, norm1_g, lat(0), lat(1))
        hc = norm_mod(xc, norm1_g, ctx(0), ctx(1))
    else:
        h, hc = h_pre
    p = matmul(h.reshape(bsz * l, d), w_in_p).reshape(bsz, l, D_INP)
    pc = matmul(hc.reshape(bsz * lc, d), w_in_p).reshape(bsz, lc, D_INP)

    waf, baf = _gla_gate_w(gla_w_a2, gla_b_a, 0)
    wab, bab = _gla_gate_w(gla_w_a2, gla_b_a, 1)
    s_zero = jnp.zeros((bsz, GLA_HEADS, GLA_DK, GLA_DV), F32)
    ocf, s_f = gla_pass(pc, waf, baf, s_zero, False)
    of, _ = gla_pass(p, waf, baf, s_f, False)
    if ctx_out:
        yac, s_b = gla_pass(pc, wab, bab, s_zero, True, ocf, gla_norm_g)
    else:
        _, s_b = gla_pass(pc, wab, bab, s_zero, True)
    ya, _ = gla_pass(p, wab, bab, s_b, True, of, gla_norm_g)

    wgf, wgb = _lru_gate_w(lru_w_gate, 0), _lru_gate_w(lru_w_gate, 1)
    h_zero = jnp.zeros((bsz, 1, D_LRU), F32)
    lru = functools.partial(lru_pass, cw=lru_conv_w, cb=lru_conv_b)
    hcf, h0f = lru(pc, wg=wgf, bg=lru_b_gate[0], lam=lru_lambda[0], h0=h_zero, reverse=False)
    hf, _ = lru(p, wg=wgf, bg=lru_b_gate[0], lam=lru_lambda[0], h0=h0f, reverse=False)
    if ctx_out:
        ylc, h0b = lru(pc, wg=wgb, bg=lru_b_gate[1], lam=lru_lambda[1], h0=h_zero, reverse=True, h_fwd=hcf)
    else:
        _, h0b = lru(pc, wg=wgb, bg=lru_b_gate[1], lam=lru_lambda[1], h0=h_zero, reverse=True)
    yl, _ = lru(p, wg=wgb, bg=lru_b_gate[1], lam=lru_lambda[1], h0=h0b, reverse=True, h_fwd=hf)

    cos, sin = rope
    cos_c = jnp.ones((lc, LANES), F32)
    sin_c = jnp.zeros((lc, LANES), F32)
    w_uq = _prep_w_uq(mla_w_uq)
    w_ukv = mla_w_ukv.astype(BF16)
    qg, kg = _head_gain(mla_q_head_g), _head_gain(mla_k_head_g)
    kc, vc = kv_proj(pc, mla_kv_norm_g, w_ukv, kg, cos_c, sin_c)
    kl, vl = kv_proj(p, mla_kv_norm_g, w_ukv, kg, cos, sin)
    ql = q_proj(p, mla_q_norm_g, w_uq, qg, cos, sin)
    yb = attention(ql, kc, vc, kl, vl)

    w_out_b = w_out.astype(BF16)
    x = out_proj(ya, yb, yl, w_out_b, x, lat(2))
    if ctx_out:
        qc = q_proj(pc, mla_q_norm_g, w_uq, qg, cos_c, sin_c)
        ybc = attention(qc, kc, vc)
        xc = out_proj(yac, ybc, ylc, w_out_b, xc, ctx(2))

    nr = LANES
    wr = jnp.pad(jnp.concatenate([moe_w_grp, moe_w_exp], axis=1), ((0, 0), (0, nr - N_GROUPS - N_EXPERTS)))
    br = jnp.pad(jnp.concatenate([moe_b_grp, moe_b_exp]), (0, nr - N_GROUPS - N_EXPERTS)).reshape(1, nr)
    ctx1 = lambda k: mods[bsz, k].reshape(1, 1, d)
    h2, lg = norm_router(x, norm2_g, lat(3), lat(4), wr, br,
                         ctx=(xc, ctx1(3), ctx1(4)) if ctx_out else None)
    out, dest, tok_gate = moe_route(h2, lg, moe_w13, moe_w2, layer)
    h_next = None
    if ctx_out:
        g_next, m_next = nxt
        nlat = lambda k: m_next[:bsz, k].reshape(bsz, 1, d)
        nctx = lambda k: jnp.broadcast_to(m_next[bsz, k].reshape(1, 1, d), (bsz, 1, d))
        xc, hc_next = moe_combine(xc, ctx(5), out, dest, tok_gate, 0, norm=(g_next, nctx(0), nctx(1)))
        x, hl_next = moe_combine(x, lat(5), out, dest, tok_gate, bsz * lc, norm=(g_next, nlat(0), nlat(1)))
        h_next = (hl_next, hc_next)
    else:
        x = moe_combine(x, lat(5), out, dest, tok_gate, 0)
    return x, xc, h_next


def kernel(x, c, ctx, c_ctx, w_mod, b_mod, norm1_g, norm2_g, w_in, gla_w_a2, gla_b_a, gla_norm_g,
           mla_q_norm_g, mla_kv_norm_g, mla_w_uq, mla_w_ukv, mla_q_head_g, mla_k_head_g,
           lru_conv_w, lru_conv_b, lru_w_gate, lru_b_gate, lru_lambda, w_out,
           moe_w_grp, moe_b_grp, moe_w_exp, moe_b_exp, moe_w13, moe_w2):
    bsz, l, d = x.shape
    depth = w_mod.shape[0]
    assert bsz + 1 <= SUBLANES and l % GRID_W == 0
    cond = jnp.zeros((SUBLANES, d), F32).at[:bsz].set(c).at[bsz].set(c_ctx)
    mods = adaln_all(cond, w_mod, b_mod).reshape(depth, SUBLANES, 6, d)
    rope = _rope_tables(l // GRID_W)
    per_layer = (norm1_g, norm2_g, w_in, gla_w_a2, gla_b_a, gla_norm_g,
                 mla_q_norm_g, mla_kv_norm_g, mla_w_uq, mla_w_ukv, mla_q_head_g, mla_k_head_g,
                 lru_conv_w, lru_conv_b, lru_w_gate, lru_b_gate, lru_lambda, w_out,
                 moe_w_grp, moe_b_grp, moe_w_exp, moe_b_exp)
    xc = ctx
    h_pre = None
    for i in range(depth):
        last = i == depth - 1
        x, xc, h_pre = _layer(x, xc, mods[i], rope, tuple(t[i] for t in per_layer), moe_w13, moe_w2, i,
                              last=last, h_pre=h_pre, nxt=None if last else (norm1_g[i + 1], mods[i + 1]))
    return x
```

```python
import functools

import jax
import jax.numpy as jnp
from jax import lax
from jax.experimental import pallas as pl
from jax.experimental.pallas import tpu as pltpu

F32 = jnp.float32
BF16 = jnp.bfloat16
HI = lax.Precision.HIGHEST

D_MODEL = 4096
GRID_W = 64
NORM_EPS = 1e-6
D_GLA = D_MODEL // 4
D_MLA = D_MODEL // 2
D_LRU = D_MODEL // 4
GLA_HEADS = 4
GLA_DK = 128
GLA_DV = 256
GLA_GATE_RANK = 16
GLA_TAU = 16.0
GLA_CHUNK = 64
GLA_QSCALE = GLA_DK ** -0.5
MLA_HEADS = 16
MLA_NOPE = 128
MLA_ROPE = 64
MLA_V = 128
MLA_Q_RANK = 768
MLA_KV_RANK = 512
MLA_SCALE = (MLA_NOPE + MLA_ROPE) ** -0.5
LOG2E = 1.4426950408889634
ROPE_BASE = 10000.0
LRU_BLOCKS = 8
LRU_BW = 128
LRU_CONV = 4
LRU_C = 8.0
N_GROUPS = 8
EXPERTS_PER_GROUP = 8
N_EXPERTS = 64
TOP_K = 2
D_EXPERT = 256

LANES = 128
SUBLANES = 8
VMEM_BYTES = 64 << 20
VMEM_LIMIT = VMEM_BYTES - (8 << 20)
MOE_BLOCK = 192
MOE_SLOTS = 3

QDN_W, AL_W, KR_W = MLA_Q_RANK, LANES, LANES
COL_QDN = 0
COL_AL = 768
COL_KR = 896
COL_V = 1024
COL_G = 2048
COL_XB = 3072
COL_GB = 4096
COL_Q = 5120
COL_K = 5632
COL_KVDN = 6144
D_INP = 6656
HEAD_W = 256
VT_ROWS = MLA_V + 16
HEADS_PER_STEP = 4


def _params(sem, vmem=VMEM_LIMIT):
    return pltpu.CompilerParams(dimension_semantics=sem, vmem_limit_bytes=vmem)


def _tile(n, pref):
    t = min(n, pref)
    while n % t or t % SUBLANES:
        t -= 1
    return t


def _silu(x):
    return x * jax.nn.sigmoid(x)


def _softplus(x):
    return jnp.maximum(x, 0.0) + jnp.log1p(jnp.exp(-jnp.abs(x)))


def _pack_halves(x):
    w = x.shape[1] // 2
    bits = lax.bitcast_convert_type(x.astype(BF16).astype(F32), jnp.int32)
    return bits[:, w:] | lax.shift_right_logical(bits[:, :w], jnp.int32(16))


def _unpack_halves(p):
    lo = lax.bitcast_convert_type(lax.shift_left(p, jnp.int32(16)), F32)
    hi = lax.bitcast_convert_type(p & jnp.int32(-65536), F32)
    return lo, hi


def _adaln_body(c_ref, w_ref, b_ref, o_ref):
    s = _silu(c_ref[...]).astype(BF16)
    o_ref[...] = jnp.dot(s, w_ref[...].astype(BF16), preferred_element_type=F32) + b_ref[...]


def adaln_all(cond, w_mod, b_mod):
    depth, d, n = w_mod.shape
    tn = 1024
    return pl.pallas_call(
        _adaln_body,
        grid=(depth, n // tn),
        in_specs=[pl.BlockSpec((SUBLANES, d), lambda l, j: (0, 0)),
                  pl.BlockSpec((None, d, tn), lambda l, j: (l, 0, j)),
                  pl.BlockSpec((None, 1, tn), lambda l, j: (l, 0, j))],
        out_specs=pl.BlockSpec((None, SUBLANES, tn), lambda l, j: (l, 0, j)),
        out_shape=jax.ShapeDtypeStruct((depth, SUBLANES, n), F32),
        compiler_params=_params(("arbitrary", "arbitrary")),
        name="adaln",
    )(cond, w_mod, b_mod.reshape(depth, 1, n))


def _norm_mod(x, g, shift, scale):
    y = x * lax.rsqrt(jnp.mean(x * x, axis=-1, keepdims=True) + NORM_EPS) * g
    return y * (1.0 + scale) + shift


def _norm_mod_body(x_ref, g_ref, sh_ref, sc_ref, o_ref):
    o_ref[...] = _norm_mod(x_ref[...], g_ref[...], sh_ref[...], sc_ref[...]).astype(o_ref.dtype)


def _norm_router_body(*refs, n_ctx):
    if n_ctx:
        (xc_ref, shc_ref, scc_ref, x_ref, sh_ref, sc_ref, g_ref, wr_ref, br_ref, o_ref, lg_ref) = refs
        is_ctx = pl.program_id(0) < n_ctx
        x = jnp.where(is_ctx, xc_ref[...], x_ref[...])
        sh = jnp.where(is_ctx, shc_ref[...], sh_ref[...])
        sc = jnp.where(is_ctx, scc_ref[...], sc_ref[...])
    else:
        x_ref, sh_ref, sc_ref, g_ref, wr_ref, br_ref, o_ref, lg_ref = refs
        x, sh, sc = x_ref[...], sh_ref[...], sc_ref[...]
    h = _norm_mod(x, g_ref[...], sh, sc)
    o_ref[...] = _pack_halves(h)
    lg_ref[...] = jnp.dot(h.astype(BF16), wr_ref[...], preferred_element_type=F32) + br_ref[...]


def norm_mod(x, g, shift, scale):
    bx, lx, d = x.shape
    tm = _tile(lx, 256)
    row = pl.BlockSpec((None, tm, d), lambda b, i: (b, i, 0))
    vec = pl.BlockSpec((1, d), lambda b, i: (0, 0))
    mod = pl.BlockSpec((None, 1, d), lambda b, i: (b, 0, 0))
    return pl.pallas_call(
        _norm_mod_body, grid=(bx, lx // tm),
        in_specs=[row, vec, mod, mod], out_specs=row,
        out_shape=jax.ShapeDtypeStruct((bx, lx, d), BF16),
        compiler_params=_params(("arbitrary", "arbitrary")), name="norm_mod",
    )(x, g.reshape(1, d), shift, scale)


def norm_router(x, g, shift, scale, wr, br, ctx=None):
    bsz, l, d = x.shape
    nr = wr.shape[1]
    lc = ctx[0].shape[1] if ctx is not None else 0
    tm = _tile(l, 256) if ctx is None else _tile(lc, 256)
    assert l % tm == 0
    n_ctx = bsz * lc // tm
    n_lat = bsz * l // tm
    per_b = l // tm
    lat = lambda f: jnp.maximum(f - n_ctx, 0)
    vec = lambda w: pl.BlockSpec((1, w), lambda f: (0, 0))
    in_specs, args = [], []
    if ctx is not None:
        xc, shc, scc = ctx
        one = pl.BlockSpec((None, 1, d), lambda f: (0, 0, 0))
        in_specs += [pl.BlockSpec((tm, d), lambda f: (jnp.minimum(f, n_ctx - 1), 0)), one, one]
        args += [xc.reshape(bsz * lc, d), shc, scc]
    mod = pl.BlockSpec((None, 1, d), lambda f: (lat(f) // per_b, 0, 0))
    in_specs += [pl.BlockSpec((tm, d), lambda f: (lat(f), 0)), mod, mod, vec(d),
                 pl.BlockSpec((d, nr), lambda f: (0, 0)), vec(nr)]
    args += [x.reshape(bsz * l, d), shift, scale, g.reshape(1, d), wr.astype(BF16), br]
    total = bsz * (lc + l)
    return pl.pallas_call(
        functools.partial(_norm_router_body, n_ctx=n_ctx), grid=(n_ctx + n_lat,), in_specs=in_specs,
        out_specs=[pl.BlockSpec((tm, d // 2), lambda f: (f, 0)), pl.BlockSpec((tm, nr), lambda f: (f, 0))],
        out_shape=[jax.ShapeDtypeStruct((total, d // 2), jnp.int32), jax.ShapeDtypeStruct((total, nr), F32)],
        compiler_params=_params(("arbitrary",)), name="norm_router",
    )(*args)


def _mm_body(a_ref, w_ref, o_ref):
    o_ref[...] = jnp.dot(a_ref[...], w_ref[...], preferred_element_type=F32).astype(o_ref.dtype)


def matmul(a, w, out_dtype=F32, tm_pref=1024, tn=512):
    m, k = a.shape
    n = w.shape[1]
    tm = _tile(m, tm_pref)
    return pl.pallas_call(
        _mm_body, grid=(m // tm, n // tn),
        in_specs=[pl.BlockSpec((tm, k), lambda i, j: (i, 0)),
                  pl.BlockSpec((k, tn), lambda i, j: (0, j))],
        out_specs=pl.BlockSpec((tm, tn), lambda i, j: (i, j)),
        out_shape=jax.ShapeDtypeStruct((m, n), out_dtype),
        compiler_params=_params(("arbitrary", "arbitrary")), name="matmul",
    )(a, w)


def _wout_body(ya_ref, yb_ref, yl_ref, w_ref, x_ref, g_ref, o_ref):
    acc = jnp.dot(ya_ref[...], w_ref[0:D_GLA, :], preferred_element_type=F32)
    acc += jnp.dot(yb_ref[...], w_ref[D_GLA:D_GLA + D_MLA, :], preferred_element_type=F32)
    acc += jnp.dot(yl_ref[...], w_ref[D_GLA + D_MLA:, :], preferred_element_type=F32)
    o_ref[...] = x_ref[...] + g_ref[...] * acc


def out_proj(ya, yb, yl, w, x, gate):
    bx, lx, d = x.shape
    tm = _tile(lx, 1024)
    tn = 512
    return pl.pallas_call(
        _wout_body, grid=(bx, lx // tm, d // tn),
        in_specs=[pl.BlockSpec((None, tm, D_GLA), lambda b, i, j: (b, i, 0)),
                  pl.BlockSpec((None, tm, D_MLA), lambda b, i, j: (b, i, 0)),
                  pl.BlockSpec((None, tm, D_LRU), lambda b, i, j: (b, i, 0)),
                  pl.BlockSpec((d, tn), lambda b, i, j: (0, j)),
                  pl.BlockSpec((None, tm, tn), lambda b, i, j: (b, i, j)),
                  pl.BlockSpec((None, 1, tn), lambda b, i, j: (b, 0, j))],
        out_specs=pl.BlockSpec((None, tm, tn), lambda b, i, j: (b, i, j)),
        out_shape=jax.ShapeDtypeStruct((bx, lx, d), F32),
        compiler_params=_params(("arbitrary", "arbitrary", "arbitrary")), name="out_proj",
    )(ya, yb, yl, w, x, gate)


def _gla_body(*refs, reverse, final, nchunk):
    if final:
        (q_ref, k_ref, v_ref, al_ref, wa_ref, ba_ref, s0_ref, of_ref, g_ref, ng_ref,
         o_ref, sfin_ref, s_scr) = refs
    else:
        q_ref, k_ref, v_ref, al_ref, wa_ref, ba_ref, s0_ref, o_ref, sfin_ref, s_scr = refs
    C = GLA_CHUNK

    @pl.when(pl.program_id(1) == 0)
    def _():
        s_scr[...] = s0_ref[...]

    r_i = lax.broadcasted_iota(jnp.int32, (C, C), 0)
    c_i = lax.broadcasted_iota(jnp.int32, (C, C), 1)
    keep = (c_i >= r_i) if reverse else (c_i <= r_i)
    tri = jnp.where(keep, 1.0, 0.0).astype(BF16)
    ref_row = C - 1 - C // 2 if reverse else C // 2
    last_row = 0 if reverse else C - 1

    order = range(nchunk - 1, -1, -1) if reverse else range(nchunk)
    for ci in order:
        rows = slice(ci * C, (ci + 1) * C)
        z = jnp.dot(al_ref[rows, :].astype(BF16), wa_ref[...], preferred_element_type=F32) + ba_ref[...]
        logd = (jnp.minimum(z, 0.0) - jnp.log1p(jnp.exp(-jnp.abs(z)))) * (1.0 / GLA_TAU)
        l_hi = logd.astype(BF16)
        rem = logd - l_hi.astype(F32)
        l_mid = rem.astype(BF16)
        l_lo = (rem - l_mid.astype(F32)).astype(BF16)
        b = (jnp.dot(tri, l_hi, preferred_element_type=F32) + jnp.dot(tri, l_mid, preferred_element_type=F32)
             + jnp.dot(tri, l_lo, preferred_element_type=F32))
        b_ref = b[ref_row:ref_row + 1, :]
        b_last = b[last_row:last_row + 1, :]
        q = q_ref[rows, :] * GLA_QSCALE
        k = k_ref[rows, :]
        qe = (q * jnp.exp(b - b_ref)).astype(BF16)
        ke = (k * jnp.exp(b_ref - b)).astype(BF16)
        qin = (q * jnp.exp(b)).astype(BF16)
        kst = k * jnp.exp(b_last - b)
        for h in range(GLA_HEADS):
            hs = slice(h * GLA_DK, (h + 1) * GLA_DK)
            vs = slice(h * GLA_DV, (h + 1) * GLA_DV)
            v = v_ref[rows, vs].astype(BF16)
            att = lax.dot_general(qe[:, hs], ke[:, hs], (((1,), (1,)), ((), ())),
                                  preferred_element_type=F32)
            att = jnp.where(keep, att, 0.0).astype(BF16)
            s = s_scr[h]
            o = (jnp.dot(att, v, preferred_element_type=F32)
                 + jnp.dot(qin[:, hs], s.astype(BF16), preferred_element_type=F32))
            d_col = jnp.exp(jnp.broadcast_to(b_last[:, hs], (GLA_DK, GLA_DK))).T
            s_scr[h] = (jnp.concatenate([d_col, d_col], axis=1) * s
                        + jnp.dot(kst[:, hs].T.astype(BF16), v, preferred_element_type=F32))
            if final:
                o = o + of_ref[rows, vs]
                y = o * lax.rsqrt(jnp.mean(o * o, axis=-1, keepdims=True) + NORM_EPS) * ng_ref[...]
                o_ref[rows, vs] = (y * _silu(g_ref[rows, vs])).astype(o_ref.dtype)
            else:
                o_ref[rows, vs] = o
    sfin_ref[...] = s_scr[...]


def gla_pass(p, wa, ba, s0, reverse, o_fwd=None, norm_g=None):
    bsz, l, _ = p.shape
    t = _tile(l, 256)
    nt = l // t
    final = o_fwd is not None

    def rix(i):
        return nt - 1 - i if reverse else i

    def col(width, off):
        return pl.BlockSpec((None, t, width), lambda b, i: (b, rix(i), off // width))

    hq = GLA_HEADS * GLA_DK
    state = pl.BlockSpec((None, GLA_HEADS, GLA_DK, GLA_DV), lambda b, i: (b, 0, 0, 0))
    in_specs = [col(hq, COL_Q), col(hq, COL_K), col(D_GLA, COL_V), col(AL_W, COL_AL),
                pl.BlockSpec((AL_W, hq), lambda b, i: (0, 0)),
                pl.BlockSpec((1, hq), lambda b, i: (0, 0)), state]
    args = [p, p, p, p, wa, ba, s0]
    orow = pl.BlockSpec((None, t, D_GLA), lambda b, i: (b, rix(i), 0))
    if final:
        in_specs += [orow, col(D_GLA, COL_G), pl.BlockSpec((1, GLA_DV), lambda b, i: (0, 0))]
        args += [o_fwd, p, norm_g.reshape(1, GLA_DV)]
    return pl.pallas_call(
        functools.partial(_gla_body, reverse=reverse, final=final, nchunk=t // GLA_CHUNK),
        grid=(bsz, nt), in_specs=in_specs, out_specs=[orow, state],
        out_shape=[jax.ShapeDtypeStruct((bsz, l, D_GLA), BF16 if final else F32),
                   jax.ShapeDtypeStruct((bsz, GLA_HEADS, GLA_DK, GLA_DV), F32)],
        scratch_shapes=[pltpu.VMEM((GLA_HEADS, GLA_DK, GLA_DV), F32)],
        compiler_params=_params(("arbitrary", "arbitrary")), name="gla",
    )(*args)


def _lru_body(*refs, reverse, final, t, nt):
    if final:
        (x_ref, xp_ref, xn_ref, cw_ref, cb_ref, wg_ref, bg_ref, lam_ref, h0_ref, hf_ref, gb_ref,
         o_ref, hfin_ref, h_scr, hbuf) = refs
    else:
        (x_ref, xp_ref, xn_ref, cw_ref, cb_ref, wg_ref, bg_ref, lam_ref, h0_ref,
         o_ref, hfin_ref, h_scr) = refs
        hbuf = o_ref
    i = pl.program_id(1)
    ti = nt - 1 - i if reverse else i

    @pl.when(i == 0)
    def _():
        h_scr[...] = h0_ref[...]

    ngroup = t // SUBLANES
    grp = lambda v: v.reshape(v.shape[0] // SUBLANES, SUBLANES, v.shape[1])
    sub = lax.broadcasted_iota(jnp.int32, (1, SUBLANES, 1), 1)
    x3 = grp(x_ref[...])
    xp3 = grp(jnp.where(ti == 0, 0.0, xp_ref[...]))
    xn3 = grp(jnp.where(ti == nt - 1, 0.0, xn_ref[...]))

    def row_shift(k):
        rot = pltpu.roll(x3, k % SUBLANES, axis=1)
        if k > 0:
            nb_rot = jnp.concatenate([pltpu.roll(xp3, k % SUBLANES, axis=1), rot[:-1]], axis=0)
            return jnp.where(sub >= k, rot, nb_rot)
        nb_rot = jnp.concatenate([rot[1:], pltpu.roll(xn3, k % SUBLANES, axis=1)], axis=0)
        return jnp.where(sub < SUBLANES + k, rot, nb_rot)

    cw = cw_ref[...]
    xc = (cb_ref[...] + row_shift(2) * cw[0:1, :] + row_shift(1) * cw[1:2, :]
          + x3 * cw[2:3, :] + row_shift(-1) * cw[3:4, :]).reshape(t, D_LRU)

    zs = []
    for n in range(LRU_BLOCKS):
        zs.append(jnp.dot(xc[:, n * LRU_BW:(n + 1) * LRU_BW].astype(BF16), wg_ref[n],
                          preferred_element_type=F32))
    z_r = jnp.concatenate([z[:, :LRU_BW] for z in zs], axis=1) + bg_ref[0:1, :]
    z_i = jnp.concatenate([z[:, LRU_BW:] for z in zs], axis=1) + bg_ref[1:2, :]
    log_a = -LRU_C * jax.nn.sigmoid(z_r) * _softplus(-lam_ref[...])
    a = jnp.exp(log_a)
    th = jnp.tanh(log_a)
    one_minus_a2 = -2.0 * th / (1.0 - th)
    u = jnp.sqrt(one_minus_a2) * jax.nn.sigmoid(z_i) * xc

    a, u = grp(a), grp(u)
    s = 1
    while s < SUBLANES:
        if reverse:
            ok = sub < SUBLANES - s
            a_s = pltpu.roll(a, SUBLANES - s, axis=1)
            u_s = pltpu.roll(u, SUBLANES - s, axis=1)
        else:
            ok = sub >= s
            a_s = pltpu.roll(a, s, axis=1)
            u_s = pltpu.roll(u, s, axis=1)
        u = jnp.where(ok, a * u_s + u, u)
        a = jnp.where(ok, a * a_s, a)
        s *= 2
    edge = 0 if reverse else SUBLANES - 1
    h_prev = h_scr[...]
    for gi in (range(ngroup - 1, -1, -1) if reverse else range(ngroup)):
        rows = slice(gi * SUBLANES, (gi + 1) * SUBLANES)
        hg = a[gi] * h_prev + u[gi]
        h_prev = hg[edge:edge + 1, :]
        hbuf[rows, :] = hg
    h_scr[...] = h_prev
    hfin_ref[...] = h_prev
    if final:
        g = gb_ref[...]
        gelu = 0.5 * g * (1.0 + jnp.tanh(0.7978845608028654 * (g + 0.044715 * g * g * g)))
        o_ref[...] = (gelu * (hf_ref[...] + hbuf[...])).astype(o_ref.dtype)


def lru_pass(p, cw, cb, wg, bg, lam, h0, reverse, h_fwd=None):
    bsz, l, _ = p.shape
    t = _tile(l, 256)
    nt = l // t
    tb = t // SUBLANES
    nb8 = l // SUBLANES
    final = h_fwd is not None
    xoff = COL_XB // D_LRU

    def rix(i):
        return nt - 1 - i if reverse else i

    vec = lambda r: pl.BlockSpec((r, D_LRU), lambda b, i: (0, 0))
    one = pl.BlockSpec((None, 1, D_LRU), lambda b, i: (b, 0, 0))
    orow = pl.BlockSpec((None, t, D_LRU), lambda b, i: (b, rix(i), 0))
    in_specs = [pl.BlockSpec((None, t, D_LRU), lambda b, i: (b, rix(i), xoff)),
                pl.BlockSpec((None, SUBLANES, D_LRU),
                             lambda b, i: (b, jnp.maximum(rix(i) * tb - 1, 0), xoff)),
                pl.BlockSpec((None, SUBLANES, D_LRU),
                             lambda b, i: (b, jnp.minimum((rix(i) + 1) * tb, nb8 - 1), xoff)),
                vec(LRU_CONV), vec(1),
                pl.BlockSpec((LRU_BLOCKS, LRU_BW, 2 * LRU_BW), lambda b, i: (0, 0, 0)),
                vec(2), vec(1), one]
    args = [p, p, p, cw, cb.reshape(1, D_LRU), wg, bg, lam.reshape(1, D_LRU), h0]
    if final:
        in_specs += [orow, pl.BlockSpec((None, t, D_LRU), lambda b, i: (b, rix(i), COL_GB // D_LRU))]
        args += [h_fwd, p]
    return pl.pallas_call(
        functools.partial(_lru_body, reverse=reverse, final=final, t=t, nt=nt),
        grid=(bsz, nt), in_specs=in_specs, out_specs=[orow, one],
        out_shape=[jax.ShapeDtypeStruct((bsz, l, D_LRU), BF16 if final else F32),
                   jax.ShapeDtypeStruct((bsz, 1, D_LRU), F32)],
        scratch_shapes=[pltpu.VMEM((1, D_LRU), F32)] + ([pltpu.VMEM((t, D_LRU), F32)] if final else []),
        compiler_params=_params(("arbitrary", "arbitrary")), name="lru",
    )(*args)


def _rms(x, width):
    return x * lax.rsqrt(jnp.sum(x * x, axis=-1, keepdims=True) * (1.0 / width) + NORM_EPS)


def _rope(r, cos, sin):
    lane = lax.broadcasted_iota(jnp.int32, r.shape, 1)
    first = (lane % 32) < 16
    rot = jnp.where(first, pltpu.roll(r, LANES - 16, axis=1), pltpu.roll(r, 16, axis=1))
    return r * cos + rot * sin


def _qproj_body(x_ref, gn_ref, w_ref, hg_ref, cos_ref, sin_ref, o_ref, xn_scr):
    @pl.when(pl.program_id(2) == 0)
    def _():
        xn_scr[...] = (_rms(x_ref[...], MLA_Q_RANK) * gn_ref[...]).astype(BF16)

    a = jnp.dot(xn_scr[...], w_ref[...], preferred_element_type=F32)
    for hh in range(HEADS_PER_STEP):
        c0 = hh * HEAD_W
        qn = _rms(a[:, c0:c0 + MLA_NOPE], MLA_NOPE) * hg_ref[:, :MLA_NOPE]
        qr = _rms(a[:, c0 + MLA_NOPE:c0 + HEAD_W], MLA_ROPE) * hg_ref[:, MLA_NOPE:]
        qr = _rope(qr, cos_ref[...], sin_ref[...])
        o_ref[hh] = (jnp.concatenate([qn, qr], axis=1) * (MLA_SCALE * LOG2E)).astype(o_ref.dtype)


def q_proj(p, gn, w, hg, cos, sin):
    bsz, l, _ = p.shape
    tm = _tile(l, 512)
    hw = HEADS_PER_STEP * HEAD_W
    return pl.pallas_call(
        _qproj_body, grid=(bsz, l // tm, MLA_HEADS // HEADS_PER_STEP),
        in_specs=[pl.BlockSpec((None, tm, QDN_W), lambda b, i, h: (b, i, COL_QDN // QDN_W)),
                  pl.BlockSpec((1, QDN_W), lambda b, i, h: (0, 0)),
                  pl.BlockSpec((QDN_W, hw), lambda b, i, h: (0, h)),
                  pl.BlockSpec((1, HEAD_W), lambda b, i, h: (0, 0)),
                  pl.BlockSpec((tm, LANES), lambda b, i, h: (i, 0)),
                  pl.BlockSpec((tm, LANES), lambda b, i, h: (i, 0))],
        out_specs=pl.BlockSpec((None, HEADS_PER_STEP, tm, HEAD_W), lambda b, i, h: (b, h, i, 0)),
        out_shape=jax.ShapeDtypeStruct((bsz, MLA_HEADS, l, HEAD_W), BF16),
        scratch_shapes=[pltpu.VMEM((tm, QDN_W), BF16)],
        compiler_params=_params(("arbitrary", "arbitrary", "arbitrary")), name="q_proj",
    )(p, gn.reshape(1, QDN_W), w, hg, cos, sin)


def _kvproj_body(x_ref, kr_ref, gn_ref, w_ref, hg_ref, cos_ref, sin_ref, k_ref, v_ref, xn_scr, kr_scr):
    @pl.when(pl.program_id(2) == 0)
    def _():
        xn_scr[...] = (_rms(x_ref[...], MLA_KV_RANK) * gn_ref[...]).astype(BF16)
        kr = _rms(kr_ref[...], MLA_ROPE) * hg_ref[:, MLA_NOPE:]
        kr_scr[...] = _rope(kr, cos_ref[...], sin_ref[...]).astype(BF16)

    a = jnp.dot(xn_scr[...], w_ref[...], preferred_element_type=F32)
    for hh in range(HEADS_PER_STEP):
        c0 = hh * HEAD_W
        kn = _rms(a[:, c0:c0 + MLA_NOPE], MLA_NOPE) * hg_ref[:, :MLA_NOPE]
        k_ref[hh, :, :MLA_NOPE] = kn.astype(k_ref.dtype)
        k_ref[hh, :, MLA_NOPE:] = kr_scr[...]
        v_ref[hh, :MLA_V, :] = a[:, c0 + MLA_NOPE:c0 + HEAD_W].T.astype(v_ref.dtype)
        v_ref[hh, MLA_V:, :] = jnp.ones((VT_ROWS - MLA_V, a.shape[0]), v_ref.dtype)


def kv_proj(p, gn, w, hg, cos, sin):
    bsz, l, _ = p.shape
    tm = _tile(l, 512)
    hw = HEADS_PER_STEP * HEAD_W
    return pl.pallas_call(
        _kvproj_body, grid=(bsz, l // tm, MLA_HEADS // HEADS_PER_STEP),
        in_specs=[pl.BlockSpec((None, tm, MLA_KV_RANK), lambda b, i, h: (b, i, COL_KVDN // MLA_KV_RANK)),
                  pl.BlockSpec((None, tm, KR_W), lambda b, i, h: (b, i, COL_KR // KR_W)),
                  pl.BlockSpec((1, MLA_KV_RANK), lambda b, i, h: (0, 0)),
                  pl.BlockSpec((MLA_KV_RANK, hw), lambda b, i, h: (0, h)),
                  pl.BlockSpec((1, HEAD_W), lambda b, i, h: (0, 0)),
                  pl.BlockSpec((tm, LANES), lambda b, i, h: (i, 0)),
                  pl.BlockSpec((tm, LANES), lambda b, i, h: (i, 0))],
        out_specs=[pl.BlockSpec((None, HEADS_PER_STEP, tm, HEAD_W), lambda b, i, h: (b, h, i, 0)),
                   pl.BlockSpec((None, HEADS_PER_STEP, VT_ROWS, tm), lambda b, i, h: (b, h, 0, i))],
        out_shape=[jax.ShapeDtypeStruct((bsz, MLA_HEADS, l, HEAD_W), BF16),
                   jax.ShapeDtypeStruct((bsz, MLA_HEADS, VT_ROWS, l), BF16)],
        scratch_shapes=[pltpu.VMEM((tm, MLA_KV_RANK), BF16), pltpu.VMEM((tm, KR_W), BF16)],
        compiler_params=_params(("arbitrary", "arbitrary", "arbitrary")), name="kv_proj",
    )(p, p, gn.reshape(1, MLA_KV_RANK), w, hg, cos, sin)


def _scores_t(k, q):
    return lax.dot_general(k, q, (((1,), (1,)), ((), ())), preferred_element_type=F32)


def _attn_body(*refs, tk, n_lat):
    if n_lat:
        q_ref, kc_ref, vc_ref, k_ref, v_ref, o_ref, m_scr, acc_scr, s_scr = refs
    else:
        q_ref, kc_ref, vc_ref, o_ref = refs

    def qk(slot, off):
        s_scr[slot] = _scores_t(k_ref[pl.ds(off, tk), :], q_ref[...])

    if n_lat:
        qk(0, 0)

    sc = _scores_t(kc_ref[...], q_ref[...])
    m0 = jnp.max(sc, axis=0, keepdims=True)
    acc = jnp.dot(vc_ref[...], jnp.exp2(sc - m0).astype(BF16), preferred_element_type=F32)

    if n_lat:
        m_scr[...] = m0
        acc_scr[...] = acc

        def consume(slot, off):
            s = s_scr[slot]
            m = m_scr[...]
            m_new = jnp.maximum(m, jnp.max(s, axis=0, keepdims=True))
            p = jnp.exp2(s - m_new).astype(BF16)
            m_scr[...] = m_new
            acc_scr[...] = (jnp.exp2(m - m_new) * acc_scr[...]
                            + jnp.dot(v_ref[:, pl.ds(off, tk)], p, preferred_element_type=F32))

        def pair(jj, carry):
            off = pl.multiple_of(jj * (2 * tk), 2 * tk)
            qk(1, off + tk)
            consume(0, off)
            qk(0, off + 2 * tk)
            consume(1, off + tk)
            return carry

        lax.fori_loop(0, n_lat // 2 - 1, pair, 0)
        off = (n_lat - 2) * tk
        qk(1, off + tk)
        consume(0, off)
        consume(1, off + tk)
        acc = acc_scr[...]
    o_ref[...] = (acc[:MLA_V] / acc[MLA_V:MLA_V + 1]).T.astype(o_ref.dtype)


def attention(q, kc, vc, k=None, v=None):
    bsz, nh, lq, _ = q.shape
    lc = kc.shape[2]
    tq = _tile(lq, 1024)
    once = dict(pipeline_mode=pl.Buffered(1))
    keys = lambda n: pl.BlockSpec((None, None, n, HEAD_W), lambda b, h, i: (b, h, 0, 0), **once)
    vals = lambda n: pl.BlockSpec((None, None, VT_ROWS, n), lambda b, h, i: (b, h, 0, 0), **once)
    in_specs = [pl.BlockSpec((None, None, tq, HEAD_W), lambda b, h, i: (b, h, i, 0)), keys(lc), vals(lc)]
    args = [q, kc, vc]
    tk = n_lat = 0
    scratch = []
    if k is not None:
        lk = k.shape[2]
        tk = _tile(lk // 2, 1024)
        assert tk % LANES == 0 and lk % (2 * tk) == 0
        n_lat = lk // tk
        in_specs += [keys(lk), vals(lk)]
        args += [k, v]
        scratch = [pltpu.VMEM((1, tq), F32), pltpu.VMEM((VT_ROWS, tq), F32), pltpu.VMEM((2, tk, tq), F32)]
    return pl.pallas_call(
        functools.partial(_attn_body, tk=tk, n_lat=n_lat),
        grid=(bsz, nh, lq // tq), in_specs=in_specs,
        out_specs=pl.BlockSpec((None, tq, MLA_V), lambda b, h, i: (b, i, h)),
        out_shape=jax.ShapeDtypeStruct((bsz, lq, nh * MLA_V), BF16),
        scratch_shapes=scratch,
        compiler_params=_params(("arbitrary", "arbitrary", "arbitrary")), name="attention",
    )(*args)


def _row_copy(src_hbm, row, buf, slot, r, sem):
    return pltpu.make_async_copy(src_hbm.at[pl.ds(row, 1), :], buf.at[slot, pl.ds(r, 1), :], sem.at[slot])


def _moe_body(be_ref, nu_ref, first_ref, tok_ref, h_hbm, w13_ref, w2_ref, o_ref, xbuf, sem, *, n_assign):
    j = pl.program_id(0)
    nu = nu_ref[0]
    slot = j % MOE_SLOTS
    ahead = MOE_SLOTS - 1

    def rows(blk, slot, start):
        base = first_ref[blk]

        def body(r, carry):
            tok = tok_ref[jnp.minimum(base + r, n_assign - 1)]
            cp = _row_copy(h_hbm, tok, xbuf, slot, r, sem)
            if start:
                cp.start()
            else:
                cp.wait()
            return carry

        lax.fori_loop(0, MOE_BLOCK, body, 0, unroll=8)

    for first in range(ahead):
        @pl.when((j == 0) & (first < nu))
        def _():
            rows(first, first, True)

    @pl.when(j + ahead < nu)
    def _():
        rows(j + ahead, (j + ahead) % MOE_SLOTS, True)

    @pl.when(j < nu)
    def _():
        rows(j, slot, False)
        x_lo, x_hi = _unpack_halves(xbuf[slot])
        half = w13_ref.shape[0] // 2
        hid = (jnp.dot(x_lo.astype(BF16), w13_ref[:half, :].astype(BF16), preferred_element_type=F32)
               + jnp.dot(x_hi.astype(BF16), w13_ref[half:, :].astype(BF16), preferred_element_type=F32))
        act = (_silu(hid[:, :D_EXPERT]) * hid[:, D_EXPERT:]).astype(BF16)
        o_ref[...] = _pack_halves(jnp.dot(act, w2_ref[...].astype(BF16), preferred_element_type=F32))

    @pl.when(j >= nu)
    def _():
        o_ref[...] = jnp.zeros_like(o_ref)


def moe_experts(h, tok_sorted, blk_first, blk_exp, n_used, w13, w2, layer):
    m, dp = h.shape
    d = 2 * dp
    nb = blk_first.shape[0]
    grid_spec = pltpu.PrefetchScalarGridSpec(
        num_scalar_prefetch=4, grid=(nb,),
        in_specs=[pl.BlockSpec(memory_space=pl.ANY),
                  pl.BlockSpec((None, None, d, 2 * D_EXPERT), lambda j, be, nu, fi, tok: (layer, be[j], 0, 0)),
                  pl.BlockSpec((None, None, D_EXPERT, d), lambda j, be, nu, fi, tok: (layer, be[j], 0, 0))],
        out_specs=pl.BlockSpec((MOE_BLOCK, dp), lambda j, be, nu, fi, tok: (j, 0)),
        scratch_shapes=[pltpu.VMEM((MOE_SLOTS, MOE_BLOCK, dp), jnp.int32),
                        pltpu.SemaphoreType.DMA((MOE_SLOTS,))])
    return pl.pallas_call(
        functools.partial(_moe_body, n_assign=tok_sorted.shape[0]), grid_spec=grid_spec,
        out_shape=jax.ShapeDtypeStruct((nb * MOE_BLOCK, dp), jnp.int32),
        compiler_params=_params(("arbitrary",)), name="moe_experts",
    )(blk_exp, n_used, blk_first, tok_sorted, h, w13, w2)


def _combine_body(dest_ref, out_hbm, x_ref, g_ref, tg_ref, *rest, tok_off, t, n_tiles, with_norm):
    if with_norm:
        ng_ref, sh_ref, sc_ref, o_ref, h_ref, buf_a, buf_b, sem_a, sem_b = rest
    else:
        o_ref, buf_a, buf_b, sem_a, sem_b = rest
    f = pl.program_id(0) * pl.num_programs(1) + pl.program_id(1)
    slot = f % 2

    def rows(tile, slot, start):
        base = (tok_off + tile * t) * TOP_K

        def body(r, carry):
            ca = _row_copy(out_hbm, dest_ref[base + TOP_K * r], buf_a, slot, r, sem_a)
            cb = _row_copy(out_hbm, dest_ref[base + TOP_K * r + 1], buf_b, slot, r, sem_b)
            if start:
                ca.start()
                cb.start()
            else:
                ca.wait()
                cb.wait()
            return carry

        lax.fori_loop(0, t, body, 0, unroll=8)

    @pl.when(f == 0)
    def _():
        rows(0, 0, True)

    @pl.when(f + 1 < n_tiles)
    def _():
        rows(f + 1, 1 - slot, True)

    rows(f, slot, False)
    d = x_ref.shape[1]
    half = d // 2
    tg0, tg1 = tg_ref[:, :LANES], tg_ref[:, LANES:]
    ss = jnp.zeros((t, LANES), F32)
    for c in range(half // LANES):
        wcols = slice(c * LANES, (c + 1) * LANES)
        a_lo, a_hi = _unpack_halves(buf_a[slot, :, wcols])
        b_lo, b_hi = _unpack_halves(buf_b[slot, :, wcols])
        for cols, ya, yb in ((wcols, a_lo, b_lo),
                             (slice(half + c * LANES, half + (c + 1) * LANES), a_hi, b_hi)):
            y = x_ref[:, cols] + g_ref[:, cols] * (tg0 * ya + tg1 * yb)
            o_ref[:, cols] = y
            ss = ss + y * y
    if with_norm:
        rs = lax.rsqrt(jnp.sum(ss, axis=-1, keepdims=True) * (1.0 / d) + NORM_EPS)
        rs = jnp.broadcast_to(rs, (t, LANES))
        for c in range(d // LANES):
            cols = slice(c * LANES, (c + 1) * LANES)
            h_ref[:, cols] = (o_ref[:, cols] * rs * ng_ref[:, cols] * (1.0 + sc_ref[:, cols])
                              + sh_ref[:, cols]).astype(h_ref.dtype)


def moe_combine(x, gate, out, dest, tok_gate, tok_off, norm=None):
    bx, lx, d = x.shape
    t = _tile(lx, 128)
    nt = lx // t
    tg = jnp.repeat(tok_gate[tok_off:tok_off + bx * lx], LANES, axis=1).reshape(bx, lx, TOP_K * LANES)
    row = pl.BlockSpec((None, t, d), lambda b, i, dst: (b, i, 0))
    mod = pl.BlockSpec((None, 1, d), lambda b, i, dst: (b, 0, 0))
    in_specs = [pl.BlockSpec(memory_space=pl.ANY), row, mod,
                pl.BlockSpec((None, t, TOP_K * LANES), lambda b, i, dst: (b, i, 0))]
    args = [dest, out, x, gate, tg]
    out_specs, out_shape = row, jax.ShapeDtypeStruct((bx, lx, d), F32)
    if norm is not None:
        in_specs += [pl.BlockSpec((1, d), lambda b, i, dst: (0, 0)), mod, mod]
        args += [norm[0].reshape(1, d), norm[1], norm[2]]
        out_specs, out_shape = [row, row], [out_shape, jax.ShapeDtypeStruct((bx, lx, d), BF16)]
    grid_spec = pltpu.PrefetchScalarGridSpec(
        num_scalar_prefetch=1, grid=(bx, nt), in_specs=in_specs, out_specs=out_specs,
        scratch_shapes=[pltpu.VMEM((2, t, d // 2), jnp.int32), pltpu.VMEM((2, t, d // 2), jnp.int32),
                        pltpu.SemaphoreType.DMA((2,)), pltpu.SemaphoreType.DMA((2,))])
    return pl.pallas_call(
        functools.partial(_combine_body, tok_off=tok_off, t=t, n_tiles=bx * nt, with_norm=norm is not None),
        grid_spec=grid_spec, out_shape=out_shape,
        compiler_params=_params(("arbitrary", "arbitrary")), name="moe_combine",
    )(*args)


def moe_route(h, logits, w13, w2, layer):
    m = h.shape[0]
    g_logit = logits[:, :N_GROUPS]
    e_logit = logits[:, N_GROUPS:N_GROUPS + N_EXPERTS].reshape(m, N_GROUPS, EXPERTS_PER_GROUP)
    g_idx = jnp.argmax(g_logit, axis=-1)
    p_grp = jnp.take_along_axis(jax.nn.softmax(g_logit, axis=-1), g_idx[:, None], axis=1)
    e_sel = jnp.take_along_axis(e_logit, g_idx[:, None, None], axis=1)[:, 0]
    top_v, top_i = lax.top_k(e_sel, TOP_K)
    gate = (p_grp * jax.nn.softmax(top_v, axis=-1)).reshape(-1)
    e_flat = (g_idx[:, None] * EXPERTS_PER_GROUP + top_i).reshape(-1).astype(jnp.int32)

    a = m * TOP_K
    nb = -(-a // MOE_BLOCK) + N_EXPERTS
    order = jnp.argsort(e_flat).astype(jnp.int32)
    rank = jnp.argsort(order).astype(jnp.int32)
    e_s = e_flat[order]
    experts = jnp.arange(N_EXPERTS, dtype=jnp.int32)
    start = jnp.searchsorted(e_s, experts, side='left').astype(jnp.int32)
    counts = jnp.searchsorted(e_s, experts, side='right').astype(jnp.int32) - start
    padded = -(-counts // MOE_BLOCK) * MOE_BLOCK
    pad_end = jnp.cumsum(padded)
    pad_start = pad_end - padded
    blk_ids = jnp.arange(nb, dtype=jnp.int32)
    blk_raw = jnp.searchsorted(pad_end, blk_ids * MOE_BLOCK, side='right')
    blk_exp = jnp.minimum(blk_raw, N_EXPERTS - 1).astype(jnp.int32)
    n_used = (pad_end[-1] // MOE_BLOCK).astype(jnp.int32).reshape(1)
    blk_first = jnp.where(blk_raw < N_EXPERTS,
                          start[blk_exp] + blk_ids * MOE_BLOCK - pad_start[blk_exp], 0).astype(jnp.int32)
    tok_sorted = order // TOP_K
    dest = (pad_start[e_flat] + rank - start[e_flat]).astype(jnp.int32)

    out = moe_experts(h, tok_sorted, blk_first, blk_exp, n_used, w13, w2, layer)
    return out, dest, gate.reshape(m, TOP_K)


def _prep_w_in(w):
    sizes = (512, 512, 1024, 1024, 32, 768, 512, 64, 1024, 1024)
    offs = [0]
    for s in sizes:
        offs.append(offs[-1] + s)
    q, k, v, g, al, qdn, kvdn, kr, xb, gb = [w[:, offs[i]:offs[i + 1]] for i in range(10)]
    zpad = lambda t, n: jnp.pad(t, ((0, 0), (0, n - t.shape[1])))
    return jnp.concatenate([qdn, zpad(al, AL_W), zpad(kr, KR_W), v, g, xb, gb, q, k, kvdn],
                           axis=1).astype(BF16)


def _prep_w_uq(w):
    w = w.reshape(MLA_Q_RANK, MLA_HEADS, MLA_NOPE + MLA_ROPE)
    w = jnp.pad(w, ((0, 0), (0, 0), (0, HEAD_W - MLA_NOPE - MLA_ROPE)))
    return w.reshape(MLA_Q_RANK, MLA_HEADS * HEAD_W).astype(BF16)


def _head_gain(g):
    return jnp.pad(g, (0, HEAD_W - g.shape[0])).reshape(1, HEAD_W)


def _rope_tables(rows):
    n = MLA_ROPE // 4
    inv = ROPE_BASE ** (-jnp.arange(n, dtype=F32) / n)
    r = jnp.repeat(jnp.arange(rows, dtype=F32), GRID_W)
    c = jnp.tile(jnp.arange(GRID_W, dtype=F32), rows)
    ar, ac = r[:, None] * inv, c[:, None] * inv
    l = ar.shape[0]
    cos = jnp.concatenate([jnp.cos(ar), jnp.cos(ar), jnp.cos(ac), jnp.cos(ac),
                           jnp.ones((l, LANES - MLA_ROPE), F32)], axis=1)
    sin = jnp.concatenate([-jnp.sin(ar), jnp.sin(ar), -jnp.sin(ac), jnp.sin(ac),
                           jnp.zeros((l, LANES - MLA_ROPE), F32)], axis=1)
    return cos, sin


def _gla_gate_w(w_a2, b_a, d):
    hq = GLA_HEADS * GLA_DK
    w = jnp.zeros((AL_W, hq), F32).at[d * GLA_GATE_RANK:(d + 1) * GLA_GATE_RANK].set(w_a2[d])
    return w.astype(BF16), b_a[d].reshape(1, hq)


def _lru_gate_w(w_gate, d):
    return jnp.concatenate([w_gate[d, 0], w_gate[d, 1]], axis=-1).astype(BF16)


def _layer(x, xc, mods, rope, prm, moe_w13, moe_w2, layer, last, h_pre=None, nxt=None):
    (norm1_g, norm2_g, w_in, gla_w_a2, gla_b_a, gla_norm_g,
     mla_q_norm_g, mla_kv_norm_g, mla_w_uq, mla_w_ukv, mla_q_head_g, mla_k_head_g,
     lru_conv_w, lru_conv_b, lru_w_gate, lru_b_gate, lru_lambda, w_out,
     moe_w_grp, moe_b_grp, moe_w_exp, moe_b_exp) = prm
    bsz, l, d = x.shape
    lc = xc.shape[1]
    ctx_out = not last
    lat = lambda k: mods[:bsz, k].reshape(bsz, 1, d)
    ctx = lambda k: jnp.broadcast_to(mods[bsz, k].reshape(1, 1, d), (bsz, 1, d))

    w_in_p = _prep_w_in(w_in)
    if h_pre is None:
        h = norm_mod(x, norm1_g, lat(0), lat(1))
        hc = norm_mod(xc, norm1_g, ctx(0), ctx(1))
    else:
        h, hc = h_pre
    p = matmul(h.reshape(bsz * l, d), w_in_p).reshape(bsz, l, D_INP)
    pc = matmul(hc.reshape(bsz * lc, d), w_in_p).reshape(bsz, lc, D_INP)

    waf, baf = _gla_gate_w(gla_w_a2, gla_b_a, 0)
    wab, bab = _gla_gate_w(gla_w_a2, gla_b_a, 1)
    s_zero = jnp.zeros((bsz, GLA_HEADS, GLA_DK, GLA_DV), F32)
    ocf, s_f = gla_pass(pc, waf, baf, s_zero, False)
    of, _ = gla_pass(p, waf, baf, s_f, False)
    if ctx_out:
        yac, s_b = gla_pass(pc, wab, bab, s_zero, True, ocf, gla_norm_g)
    else:
        _, s_b = gla_pass(pc, wab, bab, s_zero, True)
    ya, _ = gla_pass(p, wab, bab, s_b, True, of, gla_norm_g)

    wgf, wgb = _lru_gate_w(lru_w_gate, 0), _lru_gate_w(lru_w_gate, 1)
    h_zero = jnp.zeros((bsz, 1, D_LRU), F32)
    lru = functools.partial(lru_pass, cw=lru_conv_w, cb=lru_conv_b)
    hcf, h0f = lru(pc, wg=wgf, bg=lru_b_gate[0], lam=lru_lambda[0], h0=h_zero, reverse=False)
    hf, _ = lru(p, wg=wgf, bg=lru_b_gate[0], lam=lru_lambda[0], h0=h0f, reverse=False)
    if ctx_out:
        ylc, h0b = lru(pc, wg=wgb, bg=lru_b_gate[1], lam=lru_lambda[1], h0=h_zero, reverse=True, h_fwd=hcf)
    else:
        _, h0b = lru(pc, wg=wgb, bg=lru_b_gate[1], lam=lru_lambda[1], h0=h_zero, reverse=True)
    yl, _ = lru(p, wg=wgb, bg=lru_b_gate[1], lam=lru_lambda[1], h0=h0b, reverse=True, h_fwd=hf)

    cos, sin = rope
    cos_c = jnp.ones((lc, LANES), F32)
    sin_c = jnp.zeros((lc, LANES), F32)
    w_uq = _prep_w_uq(mla_w_uq)
    w_ukv = mla_w_ukv.astype(BF16)
    qg, kg = _head_gain(mla_q_head_g), _head_gain(mla_k_head_g)
    kc, vc = kv_proj(pc, mla_kv_norm_g, w_ukv, kg, cos_c, sin_c)
    kl, vl = kv_proj(p, mla_kv_norm_g, w_ukv, kg, cos, sin)
    ql = q_proj(p, mla_q_norm_g, w_uq, qg, cos, sin)
    yb = attention(ql, kc, vc, kl, vl)

    w_out_b = w_out.astype(BF16)
    x = out_proj(ya, yb, yl, w_out_b, x, lat(2))
    if ctx_out:
        qc = q_proj(pc, mla_q_norm_g, w_uq, qg, cos_c, sin_c)
        ybc = attention(qc, kc, vc)
        xc = out_proj(yac, ybc, ylc, w_out_b, xc, ctx(2))

    nr = LANES
    wr = jnp.pad(jnp.concatenate([moe_w_grp, moe_w_exp], axis=1), ((0, 0), (0, nr - N_GROUPS - N_EXPERTS)))
    br = jnp.pad(jnp.concatenate([moe_b_grp, moe_b_exp]), (0, nr - N_GROUPS - N_EXPERTS)).reshape(1, nr)
    ctx1 = lambda k: mods[bsz, k].reshape(1, 1, d)
    h2, lg = norm_router(x, norm2_g, lat(3), lat(4), wr, br,
                         ctx=(xc, ctx1(3), ctx1(4)) if ctx_out else None)
    out, dest, tok_gate = moe_route(h2, lg, moe_w13, moe_w2, layer)
    h_next = None
    if ctx_out:
        g_next, m_next = nxt
        nlat = lambda k: m_next[:bsz, k].reshape(bsz, 1, d)
        nctx = lambda k: jnp.broadcast_to(m_next[bsz, k].reshape(1, 1, d), (bsz, 1, d))
        xc, hc_next = moe_combine(xc, ctx(5), out, dest, tok_gate, 0, norm=(g_next, nctx(0), nctx(1)))
        x, hl_next = moe_combine(x, lat(5), out, dest, tok_gate, bsz * lc, norm=(g_next, nlat(0), nlat(1)))
        h_next = (hl_next, hc_next)
    else:
        x = moe_combine(x, lat(5), out, dest, tok_gate, 0)
    return x, xc, h_next


def kernel(x, c, ctx, c_ctx, w_mod, b_mod, norm1_g, norm2_g, w_in, gla_w_a2, gla_b_a, gla_norm_g,
           mla_q_norm_g, mla_kv_norm_g, mla_w_uq, mla_w_ukv, mla_q_head_g, mla_k_head_g,
           lru_conv_w, lru_conv_b, lru_w_gate, lru_b_gate, lru_lambda, w_out,
           moe_w_grp, moe_b_grp, moe_w_exp, moe_b_exp, moe_w13, moe_w2):
    bsz, l, d = x.shape
    depth = w_mod.shape[0]
    assert bsz + 1 <= SUBLANES and l % GRID_W == 0
    cond = jnp.zeros((SUBLANES, d), F32).at[:bsz].set(c).at[bsz].set(c_ctx)
    mods = adaln_all(cond, w_mod, b_mod).reshape(depth, SUBLANES, 6, d)
    rope = _rope_tables(l // GRID_W)
    per_layer = (norm1_g, norm2_g, w_in, gla_w_a2, gla_b_a, gla_norm_g,
                 mla_q_norm_g, mla_kv_norm_g, mla_w_uq, mla_w_ukv, mla_q_head_g, mla_k_head_g,
                 lru_conv_w, lru_conv_b, lru_w_gate, lru_b_gate, lru_lambda, w_out,
                 moe_w_grp, moe_b_grp, moe_w_exp, moe_b_exp)
    xc = ctx
    h_pre = None
    for i in range(depth):
        last = i == depth - 1
        x, xc, h_pre = _layer(x, xc, mods[i], rope, tuple(t[i] for t in per_layer), moe_w13, moe_w2, i,
                              last=last, h_pre=h_pre, nxt=None if last else (norm1_g[i + 1], mods[i + 1]))
    return x
```
